```python
import jax
import jax.numpy as jnp
from jax import lax
import numpy as np

D_MODEL = 2048
BATCH = 4
SEQ = 2048
DEPTH = 2

GRID_W = 64
CTX_LEN = 256
D_FF = 5632
FFN_RES = 0.5
N_MOD = 9
ROPE_DIM = 64
ROPE_BASE = 10000.0
EPS = 1e-6
RET_HEADS = 6
RET_DK = ROPE_DIM
RET_DV = 128
RET_CHUNK = 128
CONV_CH = 512
CONV_W = 31
MLA_HEADS = 6
MLA_Q_RANK = 512
MLA_KV_RANK = 256
MLA_NOPE = 128
MLA_ROPE = ROPE_DIM
MLA_DV = 128
MLA_SCALE = (MLA_NOPE + MLA_ROPE) ** -0.5
Q_BLOCK = 128
RET_W = RET_HEADS * RET_DV
MLA_W = MLA_HEADS * MLA_DV
MIX_W = RET_W + CONV_CH + MLA_W
IN_SIZES = (
    ('rq', RET_HEADS * RET_DK),
    ('rk', RET_HEADS * RET_DK),
    ('rv', RET_W),
    ('rg', RET_W),
    ('glu', 2 * CONV_CH),
    ('cq', MLA_Q_RANK),
    ('ckv', MLA_KV_RANK),
    ('kr', MLA_ROPE),
)
IN_W = 2 * RET_HEADS * RET_DK + 2 * RET_W + 2 * CONV_CH + MLA_Q_RANK + MLA_KV_RANK + MLA_ROPE

kernel_name = 'hybrid_diffusion_retention_conformer_mla_block'


def rms_norm(x, g):
    xf = x.astype(jnp.float32)
    y = xf * lax.rsqrt(jnp.mean(xf * xf, axis=-1, keepdims=True) + EPS)
    return (y * g.astype(jnp.float32)).astype(x.dtype)


def layer_norm(x, g, b):
    xf = x.astype(jnp.float32)
    mu = jnp.mean(xf, axis=-1, keepdims=True)
    var = jnp.mean(jnp.square(xf - mu), axis=-1, keepdims=True)
    y = (xf - mu) * lax.rsqrt(var + EPS) * g.astype(jnp.float32) + b.astype(jnp.float32)
    return y.astype(x.dtype)


def modulate(h, shift, scale):
    return h * (1.0 + scale) + shift


def split_cols(p):
    out = {}
    off = 0
    for name, size in IN_SIZES:
        out[name] = p[..., off:off + size]
        off += size
    return out


def axial_rope_tables(rows, rot_dim):
    quarter = rot_dim // 4
    inv = ROPE_BASE ** (-jnp.arange(quarter, dtype=jnp.float32) / quarter)
    row = jnp.repeat(jnp.arange(rows, dtype=jnp.float32), GRID_W)
    col = jnp.tile(jnp.arange(GRID_W, dtype=jnp.float32), rows)
    ang = jnp.concatenate([row[:, None] * inv, col[:, None] * inv], axis=-1)
    return jnp.cos(ang), jnp.sin(ang)


def apply_rope(x, cos, sin):
    h = x.shape[-1] // 2
    x1 = x[..., :h].astype(jnp.float32)
    x2 = x[..., h:].astype(jnp.float32)
    out = jnp.concatenate([x1 * cos - x2 * sin, x1 * sin + x2 * cos], axis=-1)
    return out.astype(x.dtype)


def ffn_half(x, g_pre, g_post, shift, scale, gate, w_in, w_out):
    h = modulate(rms_norm(x, g_pre), shift, scale)
    a, u = jnp.split(h @ w_in, 2, axis=-1)
    y = (jax.nn.silu(a) * u) @ w_out
    return x + FFN_RES * gate * rms_norm(y, g_post)


def ret_log_gamma():
    return jnp.log1p(-jnp.exp2(-5.0 - jnp.arange(RET_HEADS, dtype=jnp.float32)))


def ret_heads(t, d, cos, sin):
    B, T, _ = t.shape
    t = t.reshape(B, T, RET_HEADS, d)
    if cos is not None:
        t = apply_rope(t, cos[:, None], sin[:, None])
    return jnp.swapaxes(t, 1, 2).astype(jnp.float32)


def retention_scan(q, k, v, log_g, s0, strict):
    B, H, T, _ = q.shape
    C = RET_CHUNK
    N = T // C

    def chunks(t):
        return jnp.moveaxis(t.reshape(B, H, N, C, t.shape[-1]), 2, 0)

    i = jnp.arange(C, dtype=jnp.float32)
    diff = i[:, None] - i[None, :]
    mask = diff > 0 if strict else diff >= 0
    decay = jnp.where(mask, jnp.exp(log_g[:, None, None] * jnp.maximum(diff, 0.0)), 0.0)
    q_dec = jnp.exp(log_g[:, None] * (i + 1.0))[:, :, None]
    k_dec = jnp.exp(log_g[:, None] * (C - 1.0 - i))[:, :, None]
    c_dec = jnp.exp(log_g * C)[:, None, None]

    def step(S, blk):
        qb, kb, vb = blk
        a = jnp.einsum('bhid,bhjd->bhij', qb, kb) * decay
        o = jnp.einsum('bhij,bhjv->bhiv', a, vb) + jnp.einsum('bhid,bhdv->bhiv', qb * q_dec, S)
        S = c_dec * S + jnp.einsum('bhjd,bhjv->bhdv', kb * k_dec, vb)
        return S, o

    S, o = lax.scan(step, s0, (chunks(q), chunks(k), chunks(v)))
    return jnp.moveaxis(o, 0, 2).reshape(B, H, T, v.shape[-1]), S


def bi_retention(q, k, v, log_g, s0_f, s0_b):
    o_f, s_f = retention_scan(q, k, v, log_g, s0_f, False)
    o_b, s_b = retention_scan(jnp.flip(q, 2), jnp.flip(k, 2), jnp.flip(v, 2), log_g, s0_b, True)
    return o_f + jnp.flip(o_b, 2), s_f, s_b


def ret_final_states(k, v, log_g):
    L = k.shape[2]
    m = jnp.arange(L, dtype=jnp.float32)
    w_f = jnp.exp(log_g[:, None] * (L - 1.0 - m))
    w_b = jnp.exp(log_g[:, None] * m)
    s_f = jnp.einsum('bhld,bhlv,hl->bhdv', k, v, w_f)
    s_b = jnp.einsum('bhld,bhlv,hl->bhdv', k, v, w_b)
    return s_f, s_b


def ret_output(o, g, gn):
    mu = jnp.mean(o, axis=-1, keepdims=True)
    var = jnp.mean(jnp.square(o - mu), axis=-1, keepdims=True)
    o = (o - mu) * lax.rsqrt(var + EPS)
    B, H, T, dv = o.shape
    o = jnp.swapaxes(o, 1, 2).reshape(B, T, H * dv) * gn.astype(jnp.float32)
    return (jax.nn.silu(g.astype(jnp.float32)) * o).astype(g.dtype)


def conformer_conv(glu_in, dw, dw_b, ln_g, ln_b, pw):
    a, b = jnp.split(glu_in, 2, axis=-1)
    u = a * jax.nn.sigmoid(b)
    u = lax.conv_general_dilated(
        u, dw[:, None, :].astype(u.dtype), window_strides=(1,),
        padding=((CONV_W // 2, CONV_W // 2),),
        dimension_numbers=('NWC', 'WIO', 'NWC'),
        feature_group_count=CONV_CH) + dw_b
    u = jax.nn.silu(layer_norm(u, ln_g, ln_b))
    return u @ pw


def mla_q(cq, q_norm, w_uq, cos, sin):
    B, T, _ = cq.shape
    q = (rms_norm(cq, q_norm) @ w_uq).reshape(B, T, MLA_HEADS, MLA_NOPE + MLA_ROPE)
    if cos is not None:
        q = jnp.concatenate([q[..., :MLA_NOPE], apply_rope(q[..., MLA_NOPE:], cos[:, None], sin[:, None])], axis=-1)
    return q


def mla_kv(ckv, kr, kv_norm, w_ukv, cos, sin):
    B, T, _ = ckv.shape
    kv = (rms_norm(ckv, kv_norm) @ w_ukv).reshape(B, T, MLA_HEADS, MLA_NOPE + MLA_DV)
    k_nope, v = kv[..., :MLA_NOPE], kv[..., MLA_NOPE:]
    if cos is not None:
        kr = apply_rope(kr, cos, sin)
    k = jnp.concatenate([k_nope, jnp.broadcast_to(kr[:, :, None, :], (B, T, MLA_HEADS, MLA_ROPE))], axis=-1)
    return k, v


def softmax_attend(q, k, v):
    s = jnp.einsum('bqhd,bkhd->bhqk', q.astype(jnp.float32), k.astype(jnp.float32)) * MLA_SCALE
    p = jax.nn.softmax(s, axis=-1)
    return jnp.einsum('bhqk,bkhv->bqhv', p, v.astype(jnp.float32)).astype(v.dtype)


def latent_attention(q, k_lat, v_lat, k_ctx, v_ctx):
    k = jnp.concatenate([k_ctx, k_lat], axis=1)
    v = jnp.concatenate([v_ctx, v_lat], axis=1)
    B, S, H, d = q.shape
    qb = jnp.moveaxis(q.reshape(B, S // Q_BLOCK, Q_BLOCK, H, d), 1, 0)
    o = lax.map(lambda blk: softmax_attend(blk, k, v), qb)
    return jnp.moveaxis(o, 0, 1).reshape(B, S, H * MLA_DV)


def token_mix(hx, hc, cos, sin, w_in, w_out, ret_gn, cv_dw, cv_dw_b, cv_ln_g, cv_ln_b, cv_pw,
              q_norm, w_uq, kv_norm, w_ukv, ctx_out):
    B, L, _ = hc.shape
    log_g = ret_log_gamma()
    px = split_cols(hx @ w_in)
    if ctx_out:
        pc = split_cols(hc @ w_in)
    else:
        wb = split_cols(w_in)
        pc = {n: hc @ wb[n] for n in ('rk', 'rv', 'ckv', 'kr')}

    kc = ret_heads(pc['rk'], RET_DK, None, None) * RET_DK ** -0.5
    vc = ret_heads(pc['rv'], RET_DV, None, None)
    qx = ret_heads(px['rq'], RET_DK, cos, sin)
    kx = ret_heads(px['rk'], RET_DK, cos, sin) * RET_DK ** -0.5
    vx = ret_heads(px['rv'], RET_DV, None, None)
    if ctx_out:
        qc = ret_heads(pc['rq'], RET_DK, None, None)
        zero = jnp.zeros((B, RET_HEADS, RET_DK, RET_DV), jnp.float32)
        oc_ret, s_f, s_b = bi_retention(qc, kc, vc, log_g, zero, zero)
    else:
        s_f, s_b = ret_final_states(kc, vc, log_g)
    ox_ret, _, _ = bi_retention(qx, kx, vx, log_g, s_f, s_b)
    yx_ret = ret_output(ox_ret, px['rg'], ret_gn)

    yx_conv = conformer_conv(px['glu'], cv_dw, cv_dw_b, cv_ln_g, cv_ln_b, cv_pw)

    k_c, v_c = mla_kv(pc['ckv'], pc['kr'], kv_norm, w_ukv, None, None)
    k_x, v_x = mla_kv(px['ckv'], px['kr'], kv_norm, w_ukv, cos, sin)
    q_x = mla_q(px['cq'], q_norm, w_uq, cos, sin)
    yx_mla = latent_attention(q_x, k_x, v_x, k_c, v_c)

    yx = jnp.concatenate([yx_ret, yx_conv, yx_mla], axis=-1) @ w_out
    if not ctx_out:
        return yx, None

    yc_ret = ret_output(oc_ret, pc['rg'], ret_gn)
    yc_conv = conformer_conv(pc['glu'], cv_dw, cv_dw_b, cv_ln_g, cv_ln_b, cv_pw)
    q_c = mla_q(pc['cq'], q_norm, w_uq, None, None)
    yc_mla = softmax_attend(q_c, k_c, v_c).reshape(B, L, MLA_W)
    yc = jnp.concatenate([yc_ret, yc_conv, yc_mla], axis=-1) @ w_out
    return yx, yc


def setup_inputs(seed: int = 0) -> dict:
    key = jax.random.key(seed)
    ks = jax.random.split(key, 24)
    f32 = jnp.float32
    L, D = DEPTH, D_MODEL

    def w(k, shape, fan_in, scale=1.0):
        return jax.random.normal(k, shape, f32) * (scale * fan_in ** -0.5)

    def gain(k, shape):
        return 1.0 + 0.02 * jax.random.normal(k, shape, f32)

    def small(k, shape):
        return 0.01 * jax.random.normal(k, shape, f32)

    return {
        'x': jax.random.normal(ks[0], (BATCH, SEQ, D), f32),
        'c': jax.random.normal(ks[1], (BATCH, D), f32),
        'ctx': jax.random.normal(ks[2], (BATCH, CTX_LEN, D), f32),
        'c_ctx': jax.random.normal(ks[3], (D,), f32),
        'w_ada': w(ks[4], (L, D, N_MOD * D), D, 0.5),
        'b_ada': small(ks[5], (L, N_MOD * D)),
        'norm_g': gain(ks[6], (L, 6, D)),
        'w_ffa_in': w(ks[7], (L, D, 2 * D_FF), D),
        'w_ffa_out': w(ks[8], (L, D_FF, D), D_FF),
        'w_ffb_in': w(ks[9], (L, D, 2 * D_FF), D),
        'w_ffb_out': w(ks[10], (L, D_FF, D), D_FF),
        'w_in': w(ks[11], (L, D, IN_W), D),
        'w_out': w(ks[12], (L, MIX_W, D), MIX_W),
        'ret_gn': gain(ks[13], (L, RET_W)),
        'cv_dw': w(ks[14], (L, CONV_W, CONV_CH), CONV_W),
        'cv_dw_b': small(ks[15], (L, CONV_CH)),
        'cv_ln_g': gain(ks[16], (L, CONV_CH)),
        'cv_ln_b': small(ks[17], (L, CONV_CH)),
        'cv_pw': w(ks[18], (L, CONV_CH, CONV_CH), CONV_CH),
        'mla_q_norm': gain(ks[19], (L, MLA_Q_RANK)),
        'w_uq': w(ks[20], (L, MLA_Q_RANK, MLA_HEADS * (MLA_NOPE + MLA_ROPE)), MLA_Q_RANK),
        'mla_kv_norm': gain(ks[21], (L, MLA_KV_RANK)),
        'w_ukv': w(ks[22], (L, MLA_KV_RANK, MLA_HEADS * (MLA_NOPE + MLA_DV)), MLA_KV_RANK),
    }


def reference(x, c, ctx, c_ctx, w_ada, b_ada, norm_g, w_ffa_in, w_ffa_out, w_ffb_in, w_ffb_out,
              w_in, w_out, ret_gn, cv_dw, cv_dw_b, cv_ln_g, cv_ln_b, cv_pw,
              mla_q_norm, w_uq, mla_kv_norm, w_ukv):
    B, S, D = x.shape
    rows = S // GRID_W
    cos, sin = axial_rope_tables(rows, ROPE_DIM)
    sc = jax.nn.silu(c)
    scc = jax.nn.silu(c_ctx)
    cx = ctx
    for l in range(DEPTH):
        last = l == DEPTH - 1
        mx = (sc @ w_ada[l] + b_ada[l]).reshape(B, N_MOD, D)
        mc = (scc @ w_ada[l] + b_ada[l]).reshape(N_MOD, D)
        ng = norm_g[l]

        x = ffn_half(x, ng[0], ng[1], mx[:, 0:1], mx[:, 1:2], mx[:, 2:3], w_ffa_in[l], w_ffa_out[l])
        cx = ffn_half(cx, ng[0], ng[1], mc[0], mc[1], mc[2], w_ffa_in[l], w_ffa_out[l])

        hx = modulate(rms_norm(x, ng[2]), mx[:, 3:4], mx[:, 4:5])
        hc = modulate(rms_norm(cx, ng[2]), mc[3], mc[4])
        yx, yc = token_mix(hx, hc, cos, sin, w_in[l], w_out[l], ret_gn[l], cv_dw[l], cv_dw_b[l],
                           cv_ln_g[l], cv_ln_b[l], cv_pw[l], mla_q_norm[l], w_uq[l],
                           mla_kv_norm[l], w_ukv[l], not last)
        x = x + mx[:, 5:6] * rms_norm(yx, ng[3])

        x = ffn_half(x, ng[4], ng[5], mx[:, 6:7], mx[:, 7:8], mx[:, 8:9], w_ffb_in[l], w_ffb_out[l])
        if not last:
            cx = cx + mc[5] * rms_norm(yc, ng[3])
            cx = ffn_half(cx, ng[4], ng[5], mc[6], mc[7], mc[8], w_ffb_in[l], w_ffb_out[l])
    return x
```

```python
import functools

import jax
import jax.numpy as jnp
from jax import lax
from jax.experimental import pallas as pl
from jax.experimental.pallas import tpu as pltpu

F32 = jnp.float32
BF16 = jnp.bfloat16

D_MODEL = 2048
SEQ = 2048
DEPTH = 2
GRID_W = 64
CTX_LEN = 256
D_FF = 5632
FFN_RES = 0.5
N_MOD = 9
ROPE_DIM = 64
ROPE_BASE = 10000.0
EPS = 1e-6
RET_HEADS = 6
RET_DK = 64
RET_DV = 128
RET_CHUNK = 128
CONV_CH = 512
CONV_W = 31
MLA_HEADS = 6
MLA_Q_RANK = 512
MLA_KV_RANK = 256
MLA_NOPE = 128
MLA_ROPE = 64
MLA_DV = 128
MLA_SCALE = (MLA_NOPE + MLA_ROPE) ** -0.5
RET_W = RET_HEADS * RET_DV
MLA_W = MLA_HEADS * MLA_DV

LANES = 128
MIB = 1024 * 1024

P_GLU, P_CQ, P_RV, P_RG, P_RQ, P_RK, P_CKV, P_KR = 0, 1024, 1536, 2304, 3072, 3456, 3840, 4096
P_W = 4224
CTX_MOD_ROW = 4
ROWS_PAD = 8


def _cparams(sem, vmem_mib):
    return pltpu.CompilerParams(dimension_semantics=sem, vmem_limit_bytes=int(vmem_mib * MIB))


def _sigmoid(x):
    return 1.0 / (1.0 + jnp.exp(-x))


def _rms(x, g):
    ms = jnp.mean(x * x, axis=-1, keepdims=True)
    return x * lax.rsqrt(ms + EPS) * g


def _dot(a, b):
    return jnp.dot(a, b, preferred_element_type=F32)


def _dot_nt(a, b):
    return lax.dot_general(a, b, (((1,), (1,)), ((), ())), preferred_element_type=F32)


def _dot_tn(a, b):
    return lax.dot_general(a, b, (((0,), (0,)), ((), ())), preferred_element_type=F32)


def _rope(x, c, s1, s2):
    return x * c + pltpu.roll(x, 96, 1) * s1 + pltpu.roll(x, 32, 1) * s2


def _ada_kernel(s_ref, w_ref, b_ref, o_ref):
    s = s_ref[...]
    s = s * _sigmoid(s)
    o_ref[0] = _dot(s.astype(BF16), w_ref[0].astype(BF16)) + b_ref[0]


def _ada(s_in, w_ada, b_ada):
    depth, d, n = w_ada.shape
    tn = 1024
    return pl.pallas_call(
        _ada_kernel,
        grid=(depth, n // tn),
        in_specs=[
            pl.BlockSpec((ROWS_PAD, d), lambda l, j: (0, 0)),
            pl.BlockSpec((1, d, tn), lambda l, j: (l, 0, j)),
            pl.BlockSpec((1, 1, tn), lambda l, j: (l, 0, j)),
        ],
        out_specs=pl.BlockSpec((1, ROWS_PAD, tn), lambda l, j: (l, 0, j)),
        out_shape=jax.ShapeDtypeStruct((depth, ROWS_PAD, n), F32),
        compiler_params=_cparams(("parallel", "parallel"), 40),
        name="ada",
    )(s_in, w_ada, b_ada.reshape(depth, 1, n))


def _ffn_kernel(x_ref, mod_ref, g_ref, wa_ref, wu_ref, wo_ref, o_ref, h_ref, *, k0, nj):
    j = pl.program_id(1)

    @pl.when(j == 0)
    def _():
        y = _rms(x_ref[...], g_ref[0:1, :])
        h = y * (1.0 + mod_ref[0, k0 + 1:k0 + 2, :]) + mod_ref[0, k0:k0 + 1, :]
        h_ref[...] = h.astype(BF16)
        o_ref[...] = jnp.zeros_like(o_ref)

    h = h_ref[...]
    a = _dot(h, wa_ref[...])
    u = _dot(h, wu_ref[...])
    act = (a * _sigmoid(a) * u).astype(BF16)
    o_ref[...] += _dot(act, wo_ref[...])

    @pl.when(j == nj - 1)
    def _():
        yn = _rms(o_ref[...], g_ref[1:2, :])
        o_ref[...] = x_ref[...] + (FFN_RES * mod_ref[0, k0 + 2:k0 + 3, :]) * yn


def _ffn(x, mod, g2, w_in, w_out, k0, mod_row, tm, tf):
    r, d = x.shape
    nj = D_FF // tf
    return pl.pallas_call(
        functools.partial(_ffn_kernel, k0=k0, nj=nj),
        grid=(r // tm, nj),
        in_specs=[
            pl.BlockSpec((tm, d), lambda i, j: (i, 0)),
            pl.BlockSpec((1, N_MOD, d), lambda i, j: (mod_row(i, tm), 0, 0)),
            pl.BlockSpec((2, d), lambda i, j: (0, 0)),
            pl.BlockSpec((d, tf), lambda i, j: (0, j)),
            pl.BlockSpec((d, tf), lambda i, j: (0, nj + j)),
            pl.BlockSpec((tf, d), lambda i, j: (j, 0)),
        ],
        out_specs=pl.BlockSpec((tm, d), lambda i, j: (i, 0)),
        out_shape=jax.ShapeDtypeStruct((r, d), F32),
        scratch_shapes=[pltpu.VMEM((tm, d), BF16)],
        compiler_params=_cparams(("parallel", "arbitrary"), 56),
        name="ffn",
    )(x, mod, g2, w_in, w_in, w_out)


def _inproj_kernel(x_ref, mod_ref, g_ref, w_ref, o_ref, h_ref):
    @pl.when(pl.program_id(1) == 0)
    def _():
        y = _rms(x_ref[...], g_ref[...])
        h = y * (1.0 + mod_ref[0, 4:5, :]) + mod_ref[0, 3:4, :]
        h_ref[...] = h.astype(BF16)

    o_ref[...] = _dot(h_ref[...], w_ref[...])


def _inproj(x, mod, g, w, mod_row, tm, tn):
    r, d = x.shape
    return pl.pallas_call(
        _inproj_kernel,
        grid=(r // tm, P_W // tn),
        in_specs=[
            pl.BlockSpec((tm, d), lambda i, j: (i, 0)),
            pl.BlockSpec((1, N_MOD, d), lambda i, j: (mod_row(i, tm), 0, 0)),
            pl.BlockSpec((1, d), lambda i, j: (0, 0)),
            pl.BlockSpec((d, tn), lambda i, j: (0, j)),
        ],
        out_specs=pl.BlockSpec((tm, tn), lambda i, j: (i, j)),
        out_shape=jax.ShapeDtypeStruct((r, P_W), F32),
        scratch_shapes=[pltpu.VMEM((tm, d), BF16)],
        compiler_params=_cparams(("parallel", "arbitrary"), 56),
        name="inproj",
    )(x, mod, g, w)


def _outproj_kernel(x_ref, yr_ref, yc_ref, ym_ref, mod_ref, g_ref, w_ref, o_ref):
    y = _dot(yr_ref[...], w_ref[0:RET_W, :])
    y += _dot(yc_ref[...], w_ref[RET_W:RET_W + CONV_CH, :])
    y += _dot(ym_ref[...], w_ref[RET_W + CONV_CH:, :])
    o_ref[...] = x_ref[...] + mod_ref[0, 5:6, :] * _rms(y, g_ref[...])


def _outproj(x, y_ret, y_conv, y_mla, mod, g, w, mod_row, tm):
    r, d = x.shape
    return pl.pallas_call(
        _outproj_kernel,
        grid=(r // tm,),
        in_specs=[
            pl.BlockSpec((tm, d), lambda i: (i, 0)),
            pl.BlockSpec((tm, RET_W), lambda i: (i, 0)),
            pl.BlockSpec((tm, CONV_CH), lambda i: (i, 0)),
            pl.BlockSpec((tm, MLA_W), lambda i: (i, 0)),
            pl.BlockSpec((1, N_MOD, d), lambda i: (mod_row(i, tm), 0, 0)),
            pl.BlockSpec((1, d), lambda i: (0, 0)),
            pl.BlockSpec((d, d), lambda i: (0, 0)),
        ],
        out_specs=pl.BlockSpec((tm, d), lambda i: (i, 0)),
        out_shape=jax.ShapeDtypeStruct((r, d), F32),
        compiler_params=_cparams(("parallel",), 56),
        name="outproj",
    )(x, y_ret, y_conv, y_mla, mod, g, w)


def _ret_kernel(ql_ref, kl_ref, vl_ref, gl_ref, qc_ref, kc_ref, vc_ref, gc_ref,
                rc_ref, rs1_ref, rs2_ref, dm_ref, kf_ref, kb_ref, qf_ref, qb_ref, cd_ref, gn_ref,
                yl_ref, yc_ref, qr_ref, kr_ref, sf_ref, *, n_lat, n_ctx):
    c_len = RET_CHUNK
    lane = lax.broadcasted_iota(jnp.int32, (1, LANES), 1)
    head_masks = (lane < RET_DK, lane >= RET_DK)
    row = lax.broadcasted_iota(jnp.int32, (LANES, 2 * RET_DV), 0)
    col = lax.broadcasted_iota(jnp.int32, (LANES, 2 * RET_DV), 1)
    block_diag = (row < RET_DK) == (col < RET_DV)
    kf, kb, qf, qb, cd = kf_ref[0], kb_ref[0], qf_ref[0], qb_ref[0], cd_ref[0]
    gn = gn_ref[...]
    zero_state = jnp.zeros((LANES, 2 * RET_DV), F32)

    def kv_state(k, v16, dec):
        return jnp.where(block_diag, _dot_tn((k * dec).astype(BF16), v16), 0.0)

    def chunk_out(q, k, v16, g, sf16, sb16):
        k16 = k.astype(BF16)
        parts = []
        for e in range(2):
            q16 = jnp.where(head_masks[e], q, 0.0).astype(BF16)
            a = _dot_nt(q16, k16) * dm_ref[0, e]
            parts.append(_dot(a.astype(BF16), v16[:, e * RET_DV:(e + 1) * RET_DV]))
        o = jnp.concatenate(parts, axis=1)
        o = o + _dot((q * qf).astype(BF16), sf16) + _dot((q * qb).astype(BF16), sb16)
        normed = []
        for e in range(2):
            oe = o[:, e * RET_DV:(e + 1) * RET_DV]
            dlt = oe - jnp.mean(oe, axis=-1, keepdims=True)
            var = jnp.mean(dlt * dlt, axis=-1, keepdims=True)
            normed.append(dlt * lax.rsqrt(var + EPS))
        on = jnp.concatenate(normed, axis=1) * gn
        return ((g * _sigmoid(g)) * on).astype(BF16)

    cq = [qc_ref[c * c_len:(c + 1) * c_len, :] for c in range(n_ctx)]
    ck = [kc_ref[c * c_len:(c + 1) * c_len, :] for c in range(n_ctx)]
    cv = [vc_ref[c * c_len:(c + 1) * c_len, :].astype(BF16) for c in range(n_ctx)]
    kvf = [kv_state(ck[c], cv[c], kf) for c in range(n_ctx)]
    kvb = [kv_state(ck[c], cv[c], kb) for c in range(n_ctx)]
    sf_list = [zero_state]
    for c in range(n_ctx):
        sf_list.append(cd * sf_list[c] + kvf[c])
    sb_list = [zero_state] * (n_ctx + 1)
    for c in range(n_ctx - 1, -1, -1):
        sb_list[c] = cd * sb_list[c + 1] + kvb[c]
    for c in range(n_ctx):
        yc_ref[c * c_len:(c + 1) * c_len, :] = chunk_out(
            cq[c], ck[c], cv[c], gc_ref[c * c_len:(c + 1) * c_len, :],
            sf_list[c].astype(BF16), sb_list[c + 1].astype(BF16))
    s0f, s0b = sf_list[n_ctx], sb_list[0]

    def fwd(c, sf):
        r0 = pl.multiple_of(c * c_len, c_len)
        rows = pl.ds(r0, c_len)
        rc, rs1, rs2 = rc_ref[rows, :], rs1_ref[rows, :], rs2_ref[rows, :]
        q = _rope(ql_ref[rows, :], rc, rs1, rs2)
        k = _rope(kl_ref[rows, :], rc, rs1, rs2)
        qr_ref[rows, :] = q
        kr_ref[rows, :] = k
        sf_ref[c] = sf.astype(BF16)
        return cd * sf + kv_state(k, vl_ref[rows, :].astype(BF16), kf)

    lax.fori_loop(0, n_lat, fwd, s0f)

    def bwd(i, sb):
        c = n_lat - 1 - i
        r0 = pl.multiple_of(c * c_len, c_len)
        rows = pl.ds(r0, c_len)
        q, k = qr_ref[rows, :], kr_ref[rows, :]
        v16 = vl_ref[rows, :].astype(BF16)
        yl_ref[rows, :] = chunk_out(q, k, v16, gl_ref[rows, :], sf_ref[c], sb.astype(BF16))
        return cd * sb + kv_state(k, v16, kb)

    lax.fori_loop(0, n_lat, bwd, s0b)


def _retention(p_lat, p_ctx, tabs, gn, batch):
    t, lc = SEQ, CTX_LEN
    n_pairs = RET_HEADS // 2
    n_lat, n_ctx = t // RET_CHUNK, lc // RET_CHUNK
    qw, vw = 2 * RET_DK, 2 * RET_DV

    def pspec(rows, width, col0):
        return pl.BlockSpec((rows, width), lambda b, hp: (b, col0 // width + hp))

    tab3 = pl.BlockSpec((1, LANES, LANES), lambda b, hp: (hp, 0, 0))
    rope_spec = pl.BlockSpec((t, LANES), lambda b, hp: (0, 0))
    return pl.pallas_call(
        functools.partial(_ret_kernel, n_lat=n_lat, n_ctx=n_ctx),
        grid=(batch, n_pairs),
        in_specs=[
            pspec(t, qw, P_RQ), pspec(t, qw, P_RK), pspec(t, vw, P_RV), pspec(t, vw, P_RG),
            pspec(lc, qw, P_RQ), pspec(lc, qw, P_RK), pspec(lc, vw, P_RV), pspec(lc, vw, P_RG),
            rope_spec, rope_spec, rope_spec,
            pl.BlockSpec((1, 2, LANES, LANES), lambda b, hp: (hp, 0, 0, 0)),
            tab3, tab3, tab3, tab3,
            pl.BlockSpec((1, LANES, vw), lambda b, hp: (hp, 0, 0)),
            pl.BlockSpec((1, vw), lambda b, hp: (0, hp)),
        ],
        out_specs=[
            pl.BlockSpec((t, vw), lambda b, hp: (b, hp)),
            pl.BlockSpec((lc, vw), lambda b, hp: (b, hp)),
        ],
        out_shape=[
            jax.ShapeDtypeStruct((batch * t, RET_W), BF16),
            jax.ShapeDtypeStruct((batch * lc, RET_W), BF16),
        ],
        scratch_shapes=[
            pltpu.VMEM((t, LANES), F32),
            pltpu.VMEM((t, LANES), F32),
            pltpu.VMEM((n_lat, LANES, vw), BF16),
        ],
        compiler_params=_cparams(("parallel", "parallel"), 48),
        name="retention",
    )(p_lat, p_lat, p_lat, p_lat, p_ctx, p_ctx, p_ctx, p_ctx,
      tabs["ret_c"], tabs["ret_s1"], tabs["ret_s2"],
      tabs["dm"], tabs["kf"], tabs["kb"], tabs["qf"], tabs["qb"], tabs["cd"], gn)


def _conv_kernel(glu_ref, dw_ref, misc_ref, pw_ref, o_ref, pad_ref, *, t, rt):
    half = CONV_W // 2
    lo = 16
    zeros = jnp.zeros((lo, CONV_CH), F32)
    pad_ref[0:lo, :] = zeros
    pad_ref[lo + t:lo + t + lo, :] = zeros

    def glu(r, carry):
        rows = pl.ds(pl.multiple_of(r * rt, rt), rt)
        a = glu_ref[rows, 0:CONV_CH]
        b = glu_ref[rows, CONV_CH:2 * CONV_CH]
        pad_ref[pl.ds(pl.multiple_of(r * rt + lo, 8), rt), :] = a * _sigmoid(b)
        return carry

    lax.fori_loop(0, t // rt, glu, 0)

    def conv(r, carry):
        base = pl.multiple_of(r * rt, rt)
        acc = jnp.zeros((rt, CONV_CH), F32) + misc_ref[0:1, :]
        win = pad_ref.at[pl.ds(base, rt + 2 * lo), :]
        for k in range(CONV_W):
            acc = acc + win[lo - half + k:lo - half + k + rt, :] * dw_ref[k:k + 1, :]
        dlt = acc - jnp.mean(acc, axis=-1, keepdims=True)
        var = jnp.mean(dlt * dlt, axis=-1, keepdims=True)
        y = dlt * lax.rsqrt(var + EPS) * misc_ref[1:2, :] + misc_ref[2:3, :]
        y = y * _sigmoid(y)
        o_ref[pl.ds(base, rt), :] = _dot(y.astype(BF16), pw_ref[...]).astype(BF16)
        return carry

    lax.fori_loop(0, t // rt, conv, 0)


def _conv(p, dw, misc, pw, batch, t):
    rt = 64
    return pl.pallas_call(
        functools.partial(_conv_kernel, t=t, rt=rt),
        grid=(batch,),
        in_specs=[
            pl.BlockSpec((t, 2 * CONV_CH), lambda b: (b, P_GLU // (2 * CONV_CH))),
            pl.BlockSpec((CONV_W, CONV_CH), lambda b: (0, 0)),
            pl.BlockSpec((3, CONV_CH), lambda b: (0, 0)),
            pl.BlockSpec((CONV_CH, CONV_CH), lambda b: (0, 0)),
        ],
        out_specs=pl.BlockSpec((t, CONV_CH), lambda b: (b, 0)),
        out_shape=jax.ShapeDtypeStruct((batch * t, CONV_CH), BF16),
        scratch_shapes=[pltpu.VMEM((t + 32, CONV_CH), F32)],
        compiler_params=_cparams(("parallel",), 40),
        name="conv",
    )(p, dw, misc, pw)


def _mla_proj_kernel(cq_ref, ckv_ref, kr_ref, nq_ref, nkv_ref, wq_ref, wkv_ref, *rest, rope):
    if rope:
        rc_ref, rs1_ref, rs2_ref, q_ref, k_ref, v_ref = rest
        rot = lambda z: _rope(z, rc_ref[...], rs1_ref[...], rs2_ref[...])
    else:
        q_ref, k_ref, v_ref = rest
        rot = lambda z: z
    qq = _dot(_rms(cq_ref[...], nq_ref[...]).astype(BF16), wq_ref[...])
    kv = _dot(_rms(ckv_ref[...], nkv_ref[...]).astype(BF16), wkv_ref[...])
    kr = rot(kr_ref[...]).astype(BF16)
    hw = MLA_NOPE + LANES
    for h in range(MLA_HEADS):
        q_ref[:, h * hw:h * hw + MLA_NOPE] = qq[:, h * MLA_NOPE:(h + 1) * MLA_NOPE].astype(BF16)
        qr = qq[:, MLA_HEADS * MLA_NOPE + h * LANES:MLA_HEADS * MLA_NOPE + (h + 1) * LANES]
        q_ref[:, h * hw + MLA_NOPE:(h + 1) * hw] = rot(qr).astype(BF16)
        k_ref[:, h * hw:h * hw + MLA_NOPE] = kv[:, 2 * h * LANES:(2 * h + 1) * LANES].astype(BF16)
        k_ref[:, h * hw + MLA_NOPE:(h + 1) * hw] = kr
        v_ref[:, h * MLA_DV:(h + 1) * MLA_DV] = kv[:, (2 * h + 1) * LANES:(2 * h + 2) * LANES].astype(BF16)


def _mla_proj(p, nq, nkv, wq, wkv, rope_tabs, tm):
    r = p.shape[0]
    hw = MLA_NOPE + LANES
    rope = rope_tabs is not None
    in_specs = [
        pl.BlockSpec((tm, MLA_Q_RANK), lambda i: (i, P_CQ // MLA_Q_RANK)),
        pl.BlockSpec((tm, MLA_KV_RANK), lambda i: (i, P_CKV // MLA_KV_RANK)),
        pl.BlockSpec((tm, LANES), lambda i: (i, P_KR // LANES)),
        pl.BlockSpec((1, MLA_Q_RANK), lambda i: (0, 0)),
        pl.BlockSpec((1, MLA_KV_RANK), lambda i: (0, 0)),
        pl.BlockSpec(wq.shape, lambda i: (0, 0)),
        pl.BlockSpec(wkv.shape, lambda i: (0, 0)),
    ]
    args = [p, p, p, nq, nkv, wq, wkv]
    if rope:
        n_pos = SEQ // tm
        in_specs += [pl.BlockSpec((tm, LANES), lambda i: (i % n_pos, 0))] * 3
        args += list(rope_tabs)
    return pl.pallas_call(
        functools.partial(_mla_proj_kernel, rope=rope),
        grid=(r // tm,),
        in_specs=in_specs,
        out_specs=[
            pl.BlockSpec((tm, MLA_HEADS * hw), lambda i: (i, 0)),
            pl.BlockSpec((tm, MLA_HEADS * hw), lambda i: (i, 0)),
            pl.BlockSpec((tm, MLA_W), lambda i: (i, 0)),
        ],
        out_shape=[
            jax.ShapeDtypeStruct((r, MLA_HEADS * hw), BF16),
            jax.ShapeDtypeStruct((r, MLA_HEADS * hw), BF16),
            jax.ShapeDtypeStruct((r, MLA_W), BF16),
        ],
        compiler_params=_cparams(("parallel",), 40),
        name="mla_proj",
    )(*args)


def _attn_kernel(q_ref, kl_ref, kc_ref, vl_ref, vc_ref, o_ref):
    q = q_ref[...]
    sl = _dot_nt(q, kl_ref[...])
    sc = _dot_nt(q, kc_ref[...])
    m = jnp.maximum(jnp.max(sl, axis=-1, keepdims=True), jnp.max(sc, axis=-1, keepdims=True))
    el = jnp.exp((sl - m) * MLA_SCALE)
    ec = jnp.exp((sc - m) * MLA_SCALE)
    den = jnp.sum(el, axis=-1, keepdims=True) + jnp.sum(ec, axis=-1, keepdims=True)
    o = _dot(el.astype(BF16), vl_ref[...]) + _dot(ec.astype(BF16), vc_ref[...])
    o_ref[...] = (o / den).astype(BF16)


def _attn(q, k_lat, k_ctx, v_lat, v_ctx, batch, tq):
    hw = MLA_NOPE + LANES
    nq = SEQ // tq
    return pl.pallas_call(
        _attn_kernel,
        grid=(batch, MLA_HEADS, nq),
        in_specs=[
            pl.BlockSpec((tq, hw), lambda b, h, i: (b * nq + i, h)),
            pl.BlockSpec((SEQ, hw), lambda b, h, i: (b, h)),
            pl.BlockSpec((CTX_LEN, hw), lambda b, h, i: (b, h)),
            pl.BlockSpec((SEQ, MLA_DV), lambda b, h, i: (b, h)),
            pl.BlockSpec((CTX_LEN, MLA_DV), lambda b, h, i: (b, h)),
        ],
        out_specs=pl.BlockSpec((tq, MLA_DV), lambda b, h, i: (b * nq + i, h)),
        out_shape=jax.ShapeDtypeStruct((batch * SEQ, MLA_W), BF16),
        compiler_params=_cparams(("parallel", "parallel", "arbitrary"), 48),
        name="mla_attn",
    )(q, k_lat, k_ctx, v_lat, v_ctx)


def _attn_ctx_kernel(q_ref, k_ref, v_ref, o_ref):
    s = _dot_nt(q_ref[...], k_ref[...])
    e = jnp.exp((s - jnp.max(s, axis=-1, keepdims=True)) * MLA_SCALE)
    o = _dot(e.astype(BF16), v_ref[...])
    o_ref[...] = (o / jnp.sum(e, axis=-1, keepdims=True)).astype(BF16)


def _attn_ctx(q, k, v, batch):
    hw = MLA_NOPE + LANES
    return pl.pallas_call(
        _attn_ctx_kernel,
        grid=(batch, MLA_HEADS),
        in_specs=[
            pl.BlockSpec((CTX_LEN, hw), lambda b, h: (b, h)),
            pl.BlockSpec((CTX_LEN, hw), lambda b, h: (b, h)),
            pl.BlockSpec((CTX_LEN, MLA_DV), lambda b, h: (b, h)),
        ],
        out_specs=pl.BlockSpec((CTX_LEN, MLA_DV), lambda b, h: (b, h)),
        out_shape=jax.ShapeDtypeStruct((batch * CTX_LEN, MLA_W), BF16),
        compiler_params=_cparams(("parallel", "parallel"), 32),
        name="mla_attn_ctx",
    )(q, k, v)


def _tables():
    rows = SEQ // GRID_W
    quarter = ROPE_DIM // 4
    inv = ROPE_BASE ** (-jnp.arange(quarter, dtype=F32) / quarter)
    r_idx = jnp.repeat(jnp.arange(rows, dtype=F32), GRID_W)
    c_idx = jnp.tile(jnp.arange(GRID_W, dtype=F32), rows)
    ang = jnp.concatenate([r_idx[:, None] * inv, c_idx[:, None] * inv], axis=-1)
    cos, sin = jnp.cos(ang), jnp.sin(ang)
    z = jnp.zeros_like(cos)
    tabs = {
        "ret_c": jnp.concatenate([cos, cos, cos, cos], axis=1),
        "ret_s1": jnp.concatenate([-sin, z, -sin, z], axis=1),
        "ret_s2": jnp.concatenate([z, sin, z, sin], axis=1),
        "mla_c": jnp.concatenate([cos, cos, z, z], axis=1),
        "mla_s1": jnp.concatenate([-sin, z, z, z], axis=1),
        "mla_s2": jnp.concatenate([z, sin, z, z], axis=1),
    }
    c_len = RET_CHUNK
    log_g = jnp.log1p(-jnp.exp2(-5.0 - jnp.arange(RET_HEADS, dtype=F32)))
    i = jnp.arange(c_len, dtype=F32)
    k_scale = RET_DK ** -0.5
    lg = log_g[:, None, None]
    dm = jnp.exp(lg * jnp.abs(i[:, None] - i[None, :])) * k_scale
    tabs["dm"] = dm.reshape(RET_HEADS // 2, 2, c_len, c_len)

    def lanes_by_head(per_head_rows):
        t = jnp.repeat(per_head_rows[:, :, None], RET_DK, axis=2)
        t = t.reshape(RET_HEADS // 2, 2, c_len, RET_DK)
        return jnp.concatenate([t[:, 0], t[:, 1]], axis=-1)

    tabs["kf"] = lanes_by_head(jnp.exp(log_g[:, None] * (c_len - 1.0 - i)) * k_scale)
    tabs["kb"] = lanes_by_head(jnp.exp(log_g[:, None] * i) * k_scale)
    tabs["qf"] = lanes_by_head(jnp.exp(log_g[:, None] * (i + 1.0)))
    tabs["qb"] = lanes_by_head(jnp.exp(log_g[:, None] * (c_len - i)))
    cdec = jnp.exp(log_g * c_len).reshape(RET_HEADS // 2, 2)
    cd = jnp.repeat(cdec[:, :, None], RET_DK, axis=2).reshape(RET_HEADS // 2, 2 * RET_DK, 1)
    tabs["cd"] = jnp.broadcast_to(cd, (RET_HEADS // 2, 2 * RET_DK, 2 * RET_DV))
    return tabs


def _prep_w_in(w):
    rq, rk = w[:, 0:384], w[:, 384:768]
    rv, rg = w[:, 768:1536], w[:, 1536:2304]
    glu, cq = w[:, 2304:3328], w[:, 3328:3840]
    ckv, kr = w[:, 3840:4096], w[:, 4096:4160]
    pad = jnp.zeros((w.shape[0], P_W - 4160), w.dtype)
    return jnp.concatenate([glu, cq, rv, rg, rq, rk, ckv, kr, pad], axis=1).astype(BF16)


def _prep_w_uq(w):
    w = w.reshape(MLA_Q_RANK, MLA_HEADS, MLA_NOPE + MLA_ROPE)
    nope = w[:, :, :MLA_NOPE].reshape(MLA_Q_RANK, MLA_HEADS * MLA_NOPE)
    rope = jnp.pad(w[:, :, MLA_NOPE:], ((0, 0), (0, 0), (0, LANES - MLA_ROPE)))
    return jnp.concatenate([nope, rope.reshape(MLA_Q_RANK, MLA_HEADS * LANES)], axis=1).astype(BF16)


def _lat_mod_row(i, tm):
    return (i * tm) // SEQ


def _ctx_mod_row(i, tm):
    return CTX_MOD_ROW


def kernel(x, c, ctx, c_ctx, w_ada, b_ada, norm_g, w_ffa_in, w_ffa_out, w_ffb_in, w_ffb_out,
           w_in, w_out, ret_gn, cv_dw, cv_dw_b, cv_ln_g, cv_ln_b, cv_pw,
           mla_q_norm, w_uq, mla_kv_norm, w_ukv):
    batch, seq, d = x.shape
    assert (seq, d, ctx.shape[1]) == (SEQ, D_MODEL, CTX_LEN)
    xl = x.reshape(batch * seq, d)
    xc = ctx.reshape(batch * CTX_LEN, d)
    s_in = jnp.concatenate([c, c_ctx[None, :], jnp.zeros((ROWS_PAD - batch - 1, d), F32)], axis=0)
    mod_all = _ada(s_in, w_ada, b_ada).reshape(DEPTH, ROWS_PAD, N_MOD, d)
    tabs = _tables()
    mla_rope = (tabs["mla_c"], tabs["mla_s1"], tabs["mla_s2"])

    tm_l, tm_c, tf = 512, 512, 512
    for l in range(DEPTH):
        last = l == DEPTH - 1
        mod = mod_all[l]
        ng = norm_g[l]
        wa_in, wa_out = w_ffa_in[l].astype(BF16), w_ffa_out[l].astype(BF16)
        wb_in, wb_out = w_ffb_in[l].astype(BF16), w_ffb_out[l].astype(BF16)
        wi = _prep_w_in(w_in[l])
        wo = w_out[l].astype(BF16)
        wq = _prep_w_uq(w_uq[l])
        wkv = w_ukv[l].astype(BF16)
        nq, nkv = mla_q_norm[l][None, :], mla_kv_norm[l][None, :]
        gn = ret_gn[l][None, :]
        conv_misc = jnp.stack([cv_dw_b[l], cv_ln_g[l], cv_ln_b[l]], axis=0)
        pw = cv_pw[l].astype(BF16)

        xl = _ffn(xl, mod, ng[0:2], wa_in, wa_out, 0, _lat_mod_row, tm_l, tf)
        xc = _ffn(xc, mod, ng[0:2], wa_in, wa_out, 0, _ctx_mod_row, tm_c, tf)

        p_lat = _inproj(xl, mod, ng[2:3], wi, _lat_mod_row, 1024, 1408)
        p_ctx = _inproj(xc, mod, ng[2:3], wi, _ctx_mod_row, 1024, 1408)

        y_ret_l, y_ret_c = _retention(p_lat, p_ctx, tabs, gn, batch)
        y_conv_l = _conv(p_lat, cv_dw[l], conv_misc, pw, batch, SEQ)
        q_l, k_l, v_l = _mla_proj(p_lat, nq, nkv, wq, wkv, mla_rope, 512)
        q_c, k_c, v_c = _mla_proj(p_ctx, nq, nkv, wq, wkv, None, 512)
        y_mla_l = _attn(q_l, k_l, k_c, v_l, v_c, batch, 512)

        xl = _outproj(xl, y_ret_l, y_conv_l, y_mla_l, mod, ng[3:4], wo, _lat_mod_row, 512)
        xl = _ffn(xl, mod, ng[4:6], wb_in, wb_out, 6, _lat_mod_row, tm_l, tf)
        if not last:
            y_conv_c = _conv(p_ctx, cv_dw[l], conv_misc, pw, batch, CTX_LEN)
            y_mla_c = _attn_ctx(q_c, k_c, v_c, batch)
            xc = _outproj(xc, y_ret_c, y_conv_c, y_mla_c, mod, ng[3:4], wo, _ctx_mod_row, 512)
            xc = _ffn(xc, mod, ng[4:6], wb_in, wb_out, 6, _ctx_mod_row, tm_c, tf)
    return xl.reshape(batch, seq, d)
```

```python
import functools

import jax
import jax.numpy as jnp
from jax import lax
from jax.experimental import pallas as pl
from jax.experimental.pallas import tpu as pltpu

F32 = jnp.float32
BF16 = jnp.bfloat16

D_MODEL = 2048
SEQ = 2048
DEPTH = 2
GRID_W = 64
CTX_LEN = 256
D_FF = 5632
FFN_RES = 0.5
N_MOD = 9
ROPE_DIM = 64
ROPE_BASE = 10000.0
EPS = 1e-6
RET_HEADS = 6
RET_DK = 64
RET_DV = 128
RET_CHUNK = 128
CONV_CH = 512
CONV_W = 31
MLA_HEADS = 6
MLA_Q_RANK = 512
MLA_KV_RANK = 256
MLA_NOPE = 128
MLA_ROPE = 64
MLA_DV = 128
MLA_SCALE = (MLA_NOPE + MLA_ROPE) ** -0.5
LOG2_E = 1.4426950408889634
RET_W = RET_HEADS * RET_DV
MLA_W = MLA_HEADS * MLA_DV

LANES = 128
MIB = 1024 * 1024
ROW_CHUNK = 128
RET_UNROLL = 8
SUBLANES = 8

P_GLU, P_CQ, P_RV, P_RG, P_RQ, P_RK, P_CKV, P_KR = 0, 1024, 1536, 2304, 3072, 3456, 3840, 4096
P_W = 4224
CTX_MOD_ROW = 4
ROWS_PAD = 8


def _cparams(sem, vmem_mib):
    return pltpu.CompilerParams(dimension_semantics=sem, vmem_limit_bytes=int(vmem_mib * MIB))


def _sigmoid(x):
    return 1.0 / (1.0 + jnp.exp(-x))


def _rms(x, g):
    ms = jnp.mean(x * x, axis=-1, keepdims=True)
    return x * lax.rsqrt(ms + EPS) * g


def _dot(a, b):
    return jnp.dot(a, b, preferred_element_type=F32)


def _prenorm_modulate(x_ref, h_ref, gain_row, shift_row):
    def body(i, carry):
        rows = pl.ds(pl.multiple_of(i * ROW_CHUNK, ROW_CHUNK), ROW_CHUNK)
        x = x_ref[rows, :]
        r = lax.rsqrt(jnp.mean(x * x, axis=-1, keepdims=True) + EPS)
        h_ref[rows, :] = (x * r * gain_row + shift_row).astype(BF16)
        return carry

    lax.fori_loop(0, x_ref.shape[0] // ROW_CHUNK, body, 0)


def _postnorm_residual(y_ref, x_ref, o_ref, gain_row):
    def body(i, carry):
        rows = pl.ds(pl.multiple_of(i * ROW_CHUNK, ROW_CHUNK), ROW_CHUNK)
        y = y_ref[rows, :]
        r = lax.rsqrt(jnp.mean(y * y, axis=-1, keepdims=True) + EPS)
        o_ref[rows, :] = x_ref[rows, :] + y * r * gain_row
        return carry

    lax.fori_loop(0, y_ref.shape[0] // ROW_CHUNK, body, 0)


def _dot_nt(a, b):
    return lax.dot_general(a, b, (((1,), (1,)), ((), ())), preferred_element_type=F32)


def _dot_tn(a, b):
    return lax.dot_general(a, b, (((0,), (0,)), ((), ())), preferred_element_type=F32)


def _rope(x, c, s1, s2):
    return x * c + pltpu.roll(x, 96, 1) * s1 + pltpu.roll(x, 32, 1) * s2


def _ada_kernel(s_ref, w_ref, b_ref, o_ref):
    s = s_ref[...]
    s = s * _sigmoid(s)
    o_ref[0] = _dot(s.astype(BF16), w_ref[0].astype(BF16)) + b_ref[0]


def _ada(s_in, w_ada, b_ada):
    depth, d, n = w_ada.shape
    tn = 1024
    return pl.pallas_call(
        _ada_kernel,
        grid=(depth, n // tn),
        in_specs=[
            pl.BlockSpec((ROWS_PAD, d), lambda l, j: (0, 0)),
            pl.BlockSpec((1, d, tn), lambda l, j: (l, 0, j)),
            pl.BlockSpec((1, 1, tn), lambda l, j: (l, 0, j)),
        ],
        out_specs=pl.BlockSpec((1, ROWS_PAD, tn), lambda l, j: (l, 0, j)),
        out_shape=jax.ShapeDtypeStruct((depth, ROWS_PAD, n), F32),
        compiler_params=_cparams(("parallel", "parallel"), 40),
        name="ada",
    )(s_in, w_ada, b_ada.reshape(depth, 1, n))


def _ffn_kernel(x_ref, mod_ref, g_ref, wa_ref, wu_ref, wo_ref, o_ref, h_ref, *, k0, nj):
    j = pl.program_id(1)

    @pl.when(j == 0)
    def _():
        gain = g_ref[0:1, :] * (1.0 + mod_ref[0, k0 + 1:k0 + 2, :])
        _prenorm_modulate(x_ref, h_ref, gain, mod_ref[0, k0:k0 + 1, :])
        o_ref[...] = jnp.zeros_like(o_ref)

    h = h_ref[...]
    a = _dot(h, wa_ref[...])
    u = _dot(h, wu_ref[...])
    act = (a * _sigmoid(a) * u).astype(BF16)
    o_ref[...] += _dot(act, wo_ref[...])

    @pl.when(j == nj - 1)
    def _():
        gain = g_ref[1:2, :] * (FFN_RES * mod_ref[0, k0 + 2:k0 + 3, :])
        _postnorm_residual(o_ref, x_ref, o_ref, gain)


def _ffn(x, mod, g2, w_in, w_out, layer, k0, mod_row, tm, tf, x_buffers):
    r, d = x.shape
    nj = D_FF // tf
    return pl.pallas_call(
        functools.partial(_ffn_kernel, k0=k0, nj=nj),
        grid=(r // tm, nj),
        in_specs=[
            pl.BlockSpec((tm, d), lambda i, j: (i, 0), pipeline_mode=pl.Buffered(x_buffers)),
            pl.BlockSpec((1, N_MOD, d), lambda i, j: (mod_row(i, tm), 0, 0)),
            pl.BlockSpec((2, d), lambda i, j: (0, 0)),
            pl.BlockSpec((None, d, tf), lambda i, j: (layer, 0, j)),
            pl.BlockSpec((None, d, tf), lambda i, j: (layer, 0, nj + j)),
            pl.BlockSpec((None, tf, d), lambda i, j: (layer, j, 0)),
        ],
        out_specs=pl.BlockSpec((tm, d), lambda i, j: (i, 0)),
        out_shape=jax.ShapeDtypeStruct((r, d), F32),
        scratch_shapes=[pltpu.VMEM((tm, d), BF16)],
        compiler_params=_cparams(("parallel", "arbitrary"), 60),
        name="ffn",
    )(x, mod, g2, w_in, w_in, w_out)


def _inproj_kernel(x_ref, mod_ref, g_ref, w_ref, o_ref, h_ref):
    @pl.when(pl.program_id(1) == 0)
    def _():
        gain = g_ref[...] * (1.0 + mod_ref[0, 4:5, :])
        _prenorm_modulate(x_ref, h_ref, gain, mod_ref[0, 3:4, :])

    o_ref[...] = _dot(h_ref[...], w_ref[...])


def _inproj(x, mod, g, w, layer, mod_row, tm, tn):
    r, d = x.shape
    return pl.pallas_call(
        _inproj_kernel,
        grid=(r // tm, P_W // tn),
        in_specs=[
            pl.BlockSpec((tm, d), lambda i, j: (i, 0)),
            pl.BlockSpec((1, N_MOD, d), lambda i, j: (mod_row(i, tm), 0, 0)),
            pl.BlockSpec((1, d), lambda i, j: (0, 0)),
            pl.BlockSpec((None, d, tn), lambda i, j: (layer, 0, j)),
        ],
        out_specs=pl.BlockSpec((tm, tn), lambda i, j: (i, j)),
        out_shape=jax.ShapeDtypeStruct((r, P_W), F32),
        scratch_shapes=[pltpu.VMEM((tm, d), BF16)],
        compiler_params=_cparams(("parallel", "arbitrary"), 56),
        name="inproj",
    )(x, mod, g, w)


def _outproj_kernel(x_ref, yr_ref, yc_ref, ym_ref, mod_ref, g_ref, w_ref, o_ref):
    y = _dot(yr_ref[...], w_ref[0:RET_W, :])
    y += _dot(yc_ref[...], w_ref[RET_W:RET_W + CONV_CH, :])
    y += _dot(ym_ref[...], w_ref[RET_W + CONV_CH:, :])
    o_ref[...] = y
    _postnorm_residual(o_ref, x_ref, o_ref, g_ref[...] * mod_ref[0, 5:6, :])


def _outproj(x, y_ret, y_conv, y_mla, mod, g, w, layer, mod_row, tm):
    r, d = x.shape
    return pl.pallas_call(
        _outproj_kernel,
        grid=(r // tm,),
        in_specs=[
            pl.BlockSpec((tm, d), lambda i: (i, 0)),
            pl.BlockSpec((tm, RET_W), lambda i: (i, 0)),
            pl.BlockSpec((tm, CONV_CH), lambda i: (i, 0)),
            pl.BlockSpec((tm, MLA_W), lambda i: (i, 0)),
            pl.BlockSpec((1, N_MOD, d), lambda i: (mod_row(i, tm), 0, 0)),
            pl.BlockSpec((1, d), lambda i: (0, 0)),
            pl.BlockSpec((None, d, d), lambda i: (layer, 0, 0)),
        ],
        out_specs=pl.BlockSpec((tm, d), lambda i: (i, 0)),
        out_shape=jax.ShapeDtypeStruct((r, d), F32),
        compiler_params=_cparams(("parallel",), 56),
        name="outproj",
    )(x, y_ret, y_conv, y_mla, mod, g, w)


def _ret_kernel(ql_ref, kl_ref, vl_ref, gl_ref, qc_ref, kc_ref, vc_ref, gc_ref,
                rc_ref, rs1_ref, rs2_ref, dm_ref, kf_ref, kb_ref, qf_ref, qb_ref, cd_ref, gn_ref,
                yl_ref, yc_ref, qr_ref, kr_ref, sf_ref, *, n_lat, n_ctx):
    c_len = RET_CHUNK
    lane = lax.broadcasted_iota(jnp.int32, (1, LANES), 1)
    head_masks = (lane < RET_DK, lane >= RET_DK)
    row = lax.broadcasted_iota(jnp.int32, (LANES, 2 * RET_DV), 0)
    col = lax.broadcasted_iota(jnp.int32, (LANES, 2 * RET_DV), 1)
    block_diag = (row < RET_DK) == (col < RET_DV)
    kf, kb, qf, qb, cd = kf_ref[0], kb_ref[0], qf_ref[0], qb_ref[0], cd_ref[0]
    gn = gn_ref[...]
    zero_state = jnp.zeros((LANES, 2 * RET_DV), F32)

    def kv_state(k, v16, dec):
        return jnp.where(block_diag, _dot_tn((k * dec).astype(BF16), v16), 0.0)

    def chunk_out(q, k, v16, g, sf16, sb16):
        k16 = k.astype(BF16)
        parts = []
        for e in range(2):
            q16 = jnp.where(head_masks[e], q, 0.0).astype(BF16)
            a = _dot_nt(q16, k16) * dm_ref[0, e]
            parts.append(_dot(a.astype(BF16), v16[:, e * RET_DV:(e + 1) * RET_DV]))
        o = jnp.concatenate(parts, axis=1)
        o = o + _dot((q * qf).astype(BF16), sf16) + _dot((q * qb).astype(BF16), sb16)
        normed = []
        for e in range(2):
            oe = o[:, e * RET_DV:(e + 1) * RET_DV]
            dlt = oe - jnp.mean(oe, axis=-1, keepdims=True)
            var = jnp.mean(dlt * dlt, axis=-1, keepdims=True)
            normed.append(dlt * lax.rsqrt(var + EPS))
        on = jnp.concatenate(normed, axis=1) * gn
        return ((g * _sigmoid(g)) * on).astype(BF16)

    cq = [qc_ref[c * c_len:(c + 1) * c_len, :] for c in range(n_ctx)]
    ck = [kc_ref[c * c_len:(c + 1) * c_len, :] for c in range(n_ctx)]
    cv = [vc_ref[c * c_len:(c + 1) * c_len, :].astype(BF16) for c in range(n_ctx)]
    kvf = [kv_state(ck[c], cv[c], kf) for c in range(n_ctx)]
    kvb = [kv_state(ck[c], cv[c], kb) for c in range(n_ctx)]
    sf_list = [zero_state]
    for c in range(n_ctx):
        sf_list.append(cd * sf_list[c] + kvf[c])
    sb_list = [zero_state] * (n_ctx + 1)
    for c in range(n_ctx - 1, -1, -1):
        sb_list[c] = cd * sb_list[c + 1] + kvb[c]
    for c in range(n_ctx):
        yc_ref[c * c_len:(c + 1) * c_len, :] = chunk_out(
            cq[c], ck[c], cv[c], gc_ref[c * c_len:(c + 1) * c_len, :],
            sf_list[c].astype(BF16), sb_list[c + 1].astype(BF16))
    s0f, s0b = sf_list[n_ctx], sb_list[0]

    def fwd(c, sf):
        r0 = pl.multiple_of(c * c_len, c_len)
        rows = pl.ds(r0, c_len)
        rc, rs1, rs2 = rc_ref[rows, :], rs1_ref[rows, :], rs2_ref[rows, :]
        q = _rope(ql_ref[rows, :], rc, rs1, rs2)
        k = _rope(kl_ref[rows, :], rc, rs1, rs2)
        qr_ref[rows, :] = q
        kr_ref[rows, :] = k
        sf_ref[c] = sf.astype(BF16)
        return cd * sf + kv_state(k, vl_ref[rows, :].astype(BF16), kf)

    lax.fori_loop(0, n_lat, fwd, s0f, unroll=RET_UNROLL)

    def bwd(i, sb):
        c = n_lat - 1 - i
        r0 = pl.multiple_of(c * c_len, c_len)
        rows = pl.ds(r0, c_len)
        q, k = qr_ref[rows, :], kr_ref[rows, :]
        v16 = vl_ref[rows, :].astype(BF16)
        yl_ref[rows, :] = chunk_out(q, k, v16, gl_ref[rows, :], sf_ref[c], sb.astype(BF16))
        return cd * sb + kv_state(k, v16, kb)

    lax.fori_loop(0, n_lat, bwd, s0b, unroll=RET_UNROLL)


def _retention(p_lat, p_ctx, tabs, gn, batch):
    t, lc = SEQ, CTX_LEN
    n_pairs = RET_HEADS // 2
    n_lat, n_ctx = t // RET_CHUNK, lc // RET_CHUNK
    qw, vw = 2 * RET_DK, 2 * RET_DV

    def pspec(rows, width, col0):
        return pl.BlockSpec((rows, width), lambda b, hp: (b, col0 // width + hp))

    tab3 = pl.BlockSpec((1, LANES, LANES), lambda b, hp: (hp, 0, 0))
    rope_spec = pl.BlockSpec((t, LANES), lambda b, hp: (0, 0))
    return pl.pallas_call(
        functools.partial(_ret_kernel, n_lat=n_lat, n_ctx=n_ctx),
        grid=(batch, n_pairs),
        in_specs=[
            pspec(t, qw, P_RQ), pspec(t, qw, P_RK), pspec(t, vw, P_RV), pspec(t, vw, P_RG),
            pspec(lc, qw, P_RQ), pspec(lc, qw, P_RK), pspec(lc, vw, P_RV), pspec(lc, vw, P_RG),
            rope_spec, rope_spec, rope_spec,
            pl.BlockSpec((1, 2, LANES, LANES), lambda b, hp: (hp, 0, 0, 0)),
            tab3, tab3, tab3, tab3,
            pl.BlockSpec((1, LANES, vw), lambda b, hp: (hp, 0, 0)),
            pl.BlockSpec((1, vw), lambda b, hp: (0, hp)),
        ],
        out_specs=[
            pl.BlockSpec((t, vw), lambda b, hp: (b, hp)),
            pl.BlockSpec((lc, vw), lambda b, hp: (b, hp)),
        ],
        out_shape=[
            jax.ShapeDtypeStruct((batch * t, RET_W), BF16),
            jax.ShapeDtypeStruct((batch * lc, RET_W), BF16),
        ],
        scratch_shapes=[
            pltpu.VMEM((t, LANES), F32),
            pltpu.VMEM((t, LANES), F32),
            pltpu.VMEM((n_lat, LANES, vw), BF16),
        ],
        compiler_params=_cparams(("parallel", "parallel"), 48),
        name="retention",
    )(p_lat, p_lat, p_lat, p_lat, p_ctx, p_ctx, p_ctx, p_ctx,
      tabs["ret_c"], tabs["ret_s1"], tabs["ret_s2"],
      tabs["dm"], tabs["kf"], tabs["kb"], tabs["qf"], tabs["qb"], tabs["cd"], gn)


def _conv_kernel(glu_ref, dw_ref, misc_ref, pw_ref, o_ref, pad_ref, *, t, rt):
    half = CONV_W // 2
    lo = 16
    zeros = jnp.zeros((lo, CONV_CH), F32)
    pad_ref[0:lo, :] = zeros
    pad_ref[lo + t:lo + t + lo, :] = zeros

    def glu(r, carry):
        rows = pl.ds(pl.multiple_of(r * rt, rt), rt)
        a = glu_ref[rows, 0:CONV_CH]
        b = glu_ref[rows, CONV_CH:2 * CONV_CH]
        pad_ref[pl.ds(pl.multiple_of(r * rt + lo, 8), rt), :] = a * _sigmoid(b)
        return carry

    lax.fori_loop(0, t // rt, glu, 0)

    def conv(r, carry):
        base = pl.multiple_of(r * rt, rt)
        win_rows = rt + 2 * lo
        cg_w = 2 * LANES
        accs = []
        for cg in range(CONV_CH // cg_w):
            cols = slice(cg * cg_w, (cg + 1) * cg_w)
            win = pad_ref[pl.ds(base, win_rows), cols]
            acc = jnp.zeros((rt, cg_w), F32) + misc_ref[0:1, cols]
            for phase in range(SUBLANES):
                shifted = win if phase == 0 else pltpu.roll(win, win_rows - phase, 0)
                for off in range(phase, lo + half + 1, SUBLANES):
                    k = off - (lo - half)
                    if 0 <= k < CONV_W:
                        a0 = off - phase
                        acc = acc + shifted[a0:a0 + rt, :] * dw_ref[k:k + 1, cols]
            accs.append(acc)
        acc = jnp.concatenate(accs, axis=1)
        dlt = acc - jnp.mean(acc, axis=-1, keepdims=True)
        var = jnp.mean(dlt * dlt, axis=-1, keepdims=True)
        y = dlt * lax.rsqrt(var + EPS) * misc_ref[1:2, :] + misc_ref[2:3, :]
        y = y * _sigmoid(y)
        o_ref[pl.ds(base, rt), :] = _dot(y.astype(BF16), pw_ref[...]).astype(BF16)
        return carry

    lax.fori_loop(0, t // rt, conv, 0)


def _conv(p, dw, misc, pw, batch, t):
    rt = 64
    return pl.pallas_call(
        functools.partial(_conv_kernel, t=t, rt=rt),
        grid=(batch,),
        in_specs=[
            pl.BlockSpec((t, 2 * CONV_CH), lambda b: (b, P_GLU // (2 * CONV_CH))),
            pl.BlockSpec((CONV_W, CONV_CH), lambda b: (0, 0)),
            pl.BlockSpec((3, CONV_CH), lambda b: (0, 0)),
            pl.BlockSpec((CONV_CH, CONV_CH), lambda b: (0, 0)),
        ],
        out_specs=pl.BlockSpec((t, CONV_CH), lambda b: (b, 0)),
        out_shape=jax.ShapeDtypeStruct((batch * t, CONV_CH), BF16),
        scratch_shapes=[pltpu.VMEM((t + 32, CONV_CH), F32)],
        compiler_params=_cparams(("parallel",), 40),
        name="conv",
    )(p, dw, misc, pw)


def _mla_proj_kernel(cq_ref, ckv_ref, kr_ref, nq_ref, nkv_ref, wq_ref, wkv_ref, *rest, rope):
    if rope:
        rc_ref, rs1_ref, rs2_ref, q_ref, k_ref, v_ref = rest
        rot = lambda z: _rope(z, rc_ref[...], rs1_ref[...], rs2_ref[...])
    else:
        q_ref, k_ref, v_ref = rest
        rot = lambda z: z
    qq = _dot(_rms(cq_ref[...], nq_ref[...]).astype(BF16), wq_ref[...])
    kv = _dot(_rms(ckv_ref[...], nkv_ref[...]).astype(BF16), wkv_ref[...])
    kr = rot(kr_ref[...]).astype(BF16)
    hw = MLA_NOPE + LANES
    for h in range(MLA_HEADS):
        q_ref[:, h * hw:h * hw + MLA_NOPE] = qq[:, h * MLA_NOPE:(h + 1) * MLA_NOPE].astype(BF16)
        qr = qq[:, MLA_HEADS * MLA_NOPE + h * LANES:MLA_HEADS * MLA_NOPE + (h + 1) * LANES]
        q_ref[:, h * hw + MLA_NOPE:(h + 1) * hw] = rot(qr).astype(BF16)
        k_ref[:, h * hw:h * hw + MLA_NOPE] = kv[:, 2 * h * LANES:(2 * h + 1) * LANES].astype(BF16)
        k_ref[:, h * hw + MLA_NOPE:(h + 1) * hw] = kr
        v_ref[:, h * MLA_DV:(h + 1) * MLA_DV] = kv[:, (2 * h + 1) * LANES:(2 * h + 2) * LANES].astype(BF16)


def _mla_proj(p, nq, nkv, wq, wkv, rope_tabs, tm):
    r = p.shape[0]
    hw = MLA_NOPE + LANES
    rope = rope_tabs is not None
    in_specs = [
        pl.BlockSpec((tm, MLA_Q_RANK), lambda i: (i, P_CQ // MLA_Q_RANK)),
        pl.BlockSpec((tm, MLA_KV_RANK), lambda i: (i, P_CKV // MLA_KV_RANK)),
        pl.BlockSpec((tm, LANES), lambda i: (i, P_KR // LANES)),
        pl.BlockSpec((1, MLA_Q_RANK), lambda i: (0, 0)),
        pl.BlockSpec((1, MLA_KV_RANK), lambda i: (0, 0)),
        pl.BlockSpec(wq.shape, lambda i: (0, 0)),
        pl.BlockSpec(wkv.shape, lambda i: (0, 0)),
    ]
    args = [p, p, p, nq, nkv, wq, wkv]
    if rope:
        n_pos = SEQ // tm
        in_specs += [pl.BlockSpec((tm, LANES), lambda i: (i % n_pos, 0))] * 3
        args += list(rope_tabs)
    return pl.pallas_call(
        functools.partial(_mla_proj_kernel, rope=rope),
        grid=(r // tm,),
        in_specs=in_specs,
        out_specs=[
            pl.BlockSpec((tm, MLA_HEADS * hw), lambda i: (i, 0)),
            pl.BlockSpec((tm, MLA_HEADS * hw), lambda i: (i, 0)),
            pl.BlockSpec((tm, MLA_W), lambda i: (i, 0)),
        ],
        out_shape=[
            jax.ShapeDtypeStruct((r, MLA_HEADS * hw), BF16),
            jax.ShapeDtypeStruct((r, MLA_HEADS * hw), BF16),
            jax.ShapeDtypeStruct((r, MLA_W), BF16),
        ],
        compiler_params=_cparams(("parallel",), 40),
        name="mla_proj",
    )(*args)


def _attn_kernel(q_ref, kl_ref, kc_ref, vl_ref, vc_ref, o_ref, *, n_sub):
    sub = q_ref.shape[0] // n_sub
    for i in range(n_sub):
        rows = slice(i * sub, (i + 1) * sub)
        q = q_ref[rows, :]
        sl = _dot_nt(q, kl_ref[...])
        sc = _dot_nt(q, kc_ref[...])
        m = jnp.maximum(jnp.max(sl, axis=-1, keepdims=True), jnp.max(sc, axis=-1, keepdims=True))
        el = jnp.exp2((sl - m) * (MLA_SCALE * LOG2_E))
        ec = jnp.exp2((sc - m) * (MLA_SCALE * LOG2_E))
        den = jnp.sum(el, axis=-1, keepdims=True) + jnp.sum(ec, axis=-1, keepdims=True)
        o = _dot(el.astype(BF16), vl_ref[...]) + _dot(ec.astype(BF16), vc_ref[...])
        o_ref[rows, :] = (o / den).astype(BF16)


def _attn(q, k_lat, k_ctx, v_lat, v_ctx, batch, tq, n_sub):
    hw = MLA_NOPE + LANES
    nq = SEQ // tq
    return pl.pallas_call(
        functools.partial(_attn_kernel, n_sub=n_sub),
        grid=(batch, MLA_HEADS, nq),
        in_specs=[
            pl.BlockSpec((tq, hw), lambda b, h, i: (b * nq + i, h)),
            pl.BlockSpec((SEQ, hw), lambda b, h, i: (b, h)),
            pl.BlockSpec((CTX_LEN, hw), lambda b, h, i: (b, h)),
            pl.BlockSpec((SEQ, MLA_DV), lambda b, h, i: (b, h)),
            pl.BlockSpec((CTX_LEN, MLA_DV), lambda b, h, i: (b, h)),
        ],
        out_specs=pl.BlockSpec((tq, MLA_DV), lambda b, h, i: (b * nq + i, h)),
        out_shape=jax.ShapeDtypeStruct((batch * SEQ, MLA_W), BF16),
        compiler_params=_cparams(("parallel", "parallel", "arbitrary"), 48),
        name="mla_attn",
    )(q, k_lat, k_ctx, v_lat, v_ctx)


def _attn_ctx_kernel(q_ref, k_ref, v_ref, o_ref):
    s = _dot_nt(q_ref[...], k_ref[...])
    e = jnp.exp((s - jnp.max(s, axis=-1, keepdims=True)) * MLA_SCALE)
    o = _dot(e.astype(BF16), v_ref[...])
    o_ref[...] = (o / jnp.sum(e, axis=-1, keepdims=True)).astype(BF16)


def _attn_ctx(q, k, v, batch):
    hw = MLA_NOPE + LANES
    return pl.pallas_call(
        _attn_ctx_kernel,
        grid=(batch, MLA_HEADS),
        in_specs=[
            pl.BlockSpec((CTX_LEN, hw), lambda b, h: (b, h)),
            pl.BlockSpec((CTX_LEN, hw), lambda b, h: (b, h)),
            pl.BlockSpec((CTX_LEN, MLA_DV), lambda b, h: (b, h)),
        ],
        out_specs=pl.BlockSpec((CTX_LEN, MLA_DV), lambda b, h: (b, h)),
        out_shape=jax.ShapeDtypeStruct((batch * CTX_LEN, MLA_W), BF16),
        compiler_params=_cparams(("parallel", "parallel"), 32),
        name="mla_attn_ctx",
    )(q, k, v)


def _tables():
    rows = SEQ // GRID_W
    quarter = ROPE_DIM // 4
    inv = ROPE_BASE ** (-jnp.arange(quarter, dtype=F32) / quarter)
    r_idx = jnp.repeat(jnp.arange(rows, dtype=F32), GRID_W)
    c_idx = jnp.tile(jnp.arange(GRID_W, dtype=F32), rows)
    ang = jnp.concatenate([r_idx[:, None] * inv, c_idx[:, None] * inv], axis=-1)
    cos, sin = jnp.cos(ang), jnp.sin(ang)
    z = jnp.zeros_like(cos)
    tabs = {
        "ret_c": jnp.concatenate([cos, cos, cos, cos], axis=1),
        "ret_s1": jnp.concatenate([-sin, z, -sin, z], axis=1),
        "ret_s2": jnp.concatenate([z, sin, z, sin], axis=1),
        "mla_c": jnp.concatenate([cos, cos, z, z], axis=1),
        "mla_s1": jnp.concatenate([-sin, z, z, z], axis=1),
        "mla_s2": jnp.concatenate([z, sin, z, z], axis=1),
    }
    c_len = RET_CHUNK
    log_g = jnp.log1p(-jnp.exp2(-5.0 - jnp.arange(RET_HEADS, dtype=F32)))
    i = jnp.arange(c_len, dtype=F32)
    k_scale = RET_DK ** -0.5
    lg = log_g[:, None, None]
    dm = jnp.exp(lg * jnp.abs(i[:, None] - i[None, :])) * k_scale
    tabs["dm"] = dm.reshape(RET_HEADS // 2, 2, c_len, c_len)

    def lanes_by_head(per_head_rows):
        t = jnp.repeat(per_head_rows[:, :, None], RET_DK, axis=2)
        t = t.reshape(RET_HEADS // 2, 2, c_len, RET_DK)
        return jnp.concatenate([t[:, 0], t[:, 1]], axis=-1)

    tabs["kf"] = lanes_by_head(jnp.exp(log_g[:, None] * (c_len - 1.0 - i)) * k_scale)
    tabs["kb"] = lanes_by_head(jnp.exp(log_g[:, None] * i) * k_scale)
    tabs["qf"] = lanes_by_head(jnp.exp(log_g[:, None] * (i + 1.0)))
    tabs["qb"] = lanes_by_head(jnp.exp(log_g[:, None] * (c_len - i)))
    cdec = jnp.exp(log_g * c_len).reshape(RET_HEADS // 2, 2)
    cd = jnp.repeat(cdec[:, :, None], RET_DK, axis=2).reshape(RET_HEADS // 2, 2 * RET_DK, 1)
    tabs["cd"] = jnp.broadcast_to(cd, (RET_HEADS // 2, 2 * RET_DK, 2 * RET_DV))
    return tabs


def _prep_w_in(w):
    rq, rk = w[..., 0:384], w[..., 384:768]
    rv, rg = w[..., 768:1536], w[..., 1536:2304]
    glu, cq = w[..., 2304:3328], w[..., 3328:3840]
    ckv, kr = w[..., 3840:4096], w[..., 4096:4160]
    pad = jnp.zeros(w.shape[:-1] + (P_W - 4160,), BF16)
    parts = [glu, cq, rv, rg, rq, rk, ckv, kr]
    return jnp.concatenate([p.astype(BF16) for p in parts] + [pad], axis=-1)


def _prep_w_uq(w):
    w = w.reshape(MLA_Q_RANK, MLA_HEADS, MLA_NOPE + MLA_ROPE)
    nope = w[:, :, :MLA_NOPE].reshape(MLA_Q_RANK, MLA_HEADS * MLA_NOPE)
    rope = jnp.pad(w[:, :, MLA_NOPE:], ((0, 0), (0, 0), (0, LANES - MLA_ROPE)))
    return jnp.concatenate([nope, rope.reshape(MLA_Q_RANK, MLA_HEADS * LANES)], axis=1).astype(BF16)


def _lat_mod_row(i, tm):
    return (i * tm) // SEQ


def _ctx_mod_row(i, tm):
    return CTX_MOD_ROW


def kernel(x, c, ctx, c_ctx, w_ada, b_ada, norm_g, w_ffa_in, w_ffa_out, w_ffb_in, w_ffb_out,
           w_in, w_out, ret_gn, cv_dw, cv_dw_b, cv_ln_g, cv_ln_b, cv_pw,
           mla_q_norm, w_uq, mla_kv_norm, w_ukv):
    batch, seq, d = x.shape
    assert (seq, d, ctx.shape[1]) == (SEQ, D_MODEL, CTX_LEN)
    xl = x.reshape(batch * seq, d)
    xc = ctx.reshape(batch * CTX_LEN, d)
    s_in = jnp.concatenate([c, c_ctx[None, :], jnp.zeros((ROWS_PAD - batch - 1, d), F32)], axis=0)
    mod_all = _ada(s_in, w_ada, b_ada).reshape(DEPTH, ROWS_PAD, N_MOD, d)
    tabs = _tables()
    mla_rope = (tabs["mla_c"], tabs["mla_s1"], tabs["mla_s2"])

    wa_in, wa_out = w_ffa_in.astype(BF16), w_ffa_out.astype(BF16)
    wb_in, wb_out = w_ffb_in.astype(BF16), w_ffb_out.astype(BF16)
    wi = _prep_w_in(w_in)
    wo = w_out.astype(BF16)

    ffn_lat = dict(mod_row=_lat_mod_row, tm=1024, tf=512, x_buffers=1)
    ffn_ctx = dict(mod_row=_ctx_mod_row, tm=1024, tf=512, x_buffers=1)
    for l in range(DEPTH):
        last = l == DEPTH - 1
        mod = mod_all[l]
        ng = norm_g[l]
        wq = _prep_w_uq(w_uq[l])
        wkv = w_ukv[l].astype(BF16)
        nq, nkv = mla_q_norm[l][None, :], mla_kv_norm[l][None, :]
        gn = ret_gn[l][None, :]
        conv_misc = jnp.stack([cv_dw_b[l], cv_ln_g[l], cv_ln_b[l]], axis=0)
        pw = cv_pw[l].astype(BF16)

        xl = _ffn(xl, mod, ng[0:2], wa_in, wa_out, l, 0, **ffn_lat)
        xc = _ffn(xc, mod, ng[0:2], wa_in, wa_out, l, 0, **ffn_ctx)

        p_lat = _inproj(xl, mod, ng[2:3], wi, l, _lat_mod_row, 1024, 1408)
        p_ctx = _inproj(xc, mod, ng[2:3], wi, l, _ctx_mod_row, 1024, 1408)

        y_ret_l, y_ret_c = _retention(p_lat, p_ctx, tabs, gn, batch)
        y_conv_l = _conv(p_lat, cv_dw[l], conv_misc, pw, batch, SEQ)
        q_l, k_l, v_l = _mla_proj(p_lat, nq, nkv, wq, wkv, mla_rope, 512)
        q_c, k_c, v_c = _mla_proj(p_ctx, nq, nkv, wq, wkv, None, 512)
        y_mla_l = _attn(q_l, k_l, k_c, v_l, v_c, batch, 2048, 4)

        xl = _outproj(xl, y_ret_l, y_conv_l, y_mla_l, mod, ng[3:4], wo, l, _lat_mod_row, 512)
        xl = _ffn(xl, mod, ng[4:6], wb_in, wb_out, l, 6, **ffn_lat)
        if not last:
            y_conv_c = _conv(p_ctx, cv_dw[l], conv_misc, pw, batch, CTX_LEN)
            y_mla_c = _attn_ctx(q_c, k_c, v_c, batch)
            xc = _outproj(xc, y_ret_c, y_conv_c, y_mla_c, mod, ng[3:4], wo, l, _ctx_mod_row, 512)
            xc = _ffn(xc, mod, ng[4:6], wb_in, wb_out, l, 6, **ffn_ctx)
    return xl.reshape(batch, seq, d)
```

```python
import functools

import jax
import jax.numpy as jnp
from jax import lax
from jax.experimental import pallas as pl
from jax.experimental.pallas import tpu as pltpu

F32 = jnp.float32
BF16 = jnp.bfloat16

D_MODEL = 2048
SEQ = 2048
DEPTH = 2
GRID_W = 64
CTX_LEN = 256
D_FF = 5632
FFN_RES = 0.5
N_MOD = 9
ROPE_DIM = 64
ROPE_BASE = 10000.0
EPS = 1e-6
RET_HEADS = 6
RET_DK = 64
RET_DV = 128
RET_CHUNK = 128
CONV_CH = 512
CONV_W = 31
MLA_HEADS = 6
MLA_Q_RANK = 512
MLA_KV_RANK = 256
MLA_NOPE = 128
MLA_ROPE = 64
MLA_DV = 128
MLA_SCALE = (MLA_NOPE + MLA_ROPE) ** -0.5
LOG2_E = 1.4426950408889634
RET_W = RET_HEADS * RET_DV
MLA_W = MLA_HEADS * MLA_DV

LANES = 128
MIB = 1024 * 1024
ROW_CHUNK = 128
RET_UNROLL = 8
SUBLANES = 8

P_GLU, P_CQ, P_RV, P_RG, P_RQ, P_RK, P_CKV, P_KR = 0, 1024, 1536, 2304, 3072, 3456, 3840, 4096
P_W = 4224
CTX_MOD_ROW = 4
ROWS_PAD = 8


def _cparams(sem, vmem_mib):
    return pltpu.CompilerParams(dimension_semantics=sem, vmem_limit_bytes=int(vmem_mib * MIB))


def _sigmoid(x):
    return 1.0 / (1.0 + jnp.exp(-x))


def _rms(x, g):
    ms = jnp.mean(x * x, axis=-1, keepdims=True)
    return x * lax.rsqrt(ms + EPS) * g


def _dot(a, b):
    return jnp.dot(a, b, preferred_element_type=F32)


def _prenorm_modulate(x_ref, h_ref, gain_row, shift_row):
    def body(i, carry):
        rows = pl.ds(pl.multiple_of(i * ROW_CHUNK, ROW_CHUNK), ROW_CHUNK)
        x = x_ref[rows, :]
        r = lax.rsqrt(jnp.mean(x * x, axis=-1, keepdims=True) + EPS)
        h_ref[rows, :] = (x * r * gain_row + shift_row).astype(BF16)
        return carry

    lax.fori_loop(0, x_ref.shape[0] // ROW_CHUNK, body, 0)


def _postnorm_residual(y_ref, x_ref, o_ref, gain_row):
    def body(i, carry):
        rows = pl.ds(pl.multiple_of(i * ROW_CHUNK, ROW_CHUNK), ROW_CHUNK)
        y = y_ref[rows, :]
        r = lax.rsqrt(jnp.mean(y * y, axis=-1, keepdims=True) + EPS)
        o_ref[rows, :] = x_ref[rows, :] + y * r * gain_row
        return carry

    lax.fori_loop(0, y_ref.shape[0] // ROW_CHUNK, body, 0)


def _dot_nt(a, b):
    return lax.dot_general(a, b, (((1,), (1,)), ((), ())), preferred_element_type=F32)


def _dot_tn(a, b):
    return lax.dot_general(a, b, (((0,), (0,)), ((), ())), preferred_element_type=F32)


def _rope(x, c, s1, s2):
    return x * c + pltpu.roll(x, 96, 1) * s1 + pltpu.roll(x, 32, 1) * s2


def _ada_kernel(s_ref, w_ref, b_ref, o_ref):
    s = s_ref[...]
    s = s * _sigmoid(s)
    o_ref[0] = _dot(s.astype(BF16), w_ref[0].astype(BF16)) + b_ref[0]


def _ada(s_in, w_ada, b_ada):
    depth, d, n = w_ada.shape
    tn = 1024
    return pl.pallas_call(
        _ada_kernel,
        grid=(depth, n // tn),
        in_specs=[
            pl.BlockSpec((ROWS_PAD, d), lambda l, j: (0, 0)),
            pl.BlockSpec((1, d, tn), lambda l, j: (l, 0, j)),
            pl.BlockSpec((1, 1, tn), lambda l, j: (l, 0, j)),
        ],
        out_specs=pl.BlockSpec((1, ROWS_PAD, tn), lambda l, j: (l, 0, j)),
        out_shape=jax.ShapeDtypeStruct((depth, ROWS_PAD, n), F32),
        compiler_params=_cparams(("parallel", "parallel"), 40),
        name="ada",
    )(s_in, w_ada, b_ada.reshape(depth, 1, n))


def _ffn_kernel(x_ref, mod_ref, g_ref, wa_ref, wu_ref, wo_ref, o_ref, h_ref, *, k0, nj):
    j = pl.program_id(1)

    @pl.when(j == 0)
    def _():
        gain = g_ref[0:1, :] * (1.0 + mod_ref[0, k0 + 1:k0 + 2, :])
        _prenorm_modulate(x_ref, h_ref, gain, mod_ref[0, k0:k0 + 1, :])
        o_ref[...] = jnp.zeros_like(o_ref)

    h = h_ref[...]
    a = _dot(h, wa_ref[...])
    u = _dot(h, wu_ref[...])
    act = (a * _sigmoid(a) * u).astype(BF16)
    o_ref[...] += _dot(act, wo_ref[...])

    @pl.when(j == nj - 1)
    def _():
        gain = g_ref[1:2, :] * (FFN_RES * mod_ref[0, k0 + 2:k0 + 3, :])
        _postnorm_residual(o_ref, x_ref, o_ref, gain)


def _ffn(x, mod, g2, w_in, w_out, layer, k0, mod_row, tm, tf, x_buffers):
    r, d = x.shape
    nj = D_FF // tf
    return pl.pallas_call(
        functools.partial(_ffn_kernel, k0=k0, nj=nj),
        grid=(r // tm, nj),
        in_specs=[
            pl.BlockSpec((tm, d), lambda i, j: (i, 0), pipeline_mode=pl.Buffered(x_buffers)),
            pl.BlockSpec((1, N_MOD, d), lambda i, j: (mod_row(i, tm), 0, 0)),
            pl.BlockSpec((2, d), lambda i, j: (0, 0)),
            pl.BlockSpec((None, d, tf), lambda i, j: (layer, 0, j)),
            pl.BlockSpec((None, d, tf), lambda i, j: (layer, 0, nj + j)),
            pl.BlockSpec((None, tf, d), lambda i, j: (layer, j, 0)),
        ],
        out_specs=pl.BlockSpec((tm, d), lambda i, j: (i, 0)),
        out_shape=jax.ShapeDtypeStruct((r, d), F32),
        scratch_shapes=[pltpu.VMEM((tm, d), BF16)],
        compiler_params=_cparams(("parallel", "arbitrary"), 60),
        name="ffn",
    )(x, mod, g2, w_in, w_in, w_out)


def _inproj_kernel(x_ref, mod_ref, g_ref, w_ref, o_ref, h_ref):
    @pl.when(pl.program_id(1) == 0)
    def _():
        gain = g_ref[...] * (1.0 + mod_ref[0, 4:5, :])
        _prenorm_modulate(x_ref, h_ref, gain, mod_ref[0, 3:4, :])

    o_ref[...] = _dot(h_ref[...], w_ref[...])


def _inproj(x, mod, g, w, layer, mod_row, tm, tn):
    r, d = x.shape
    return pl.pallas_call(
        _inproj_kernel,
        grid=(r // tm, P_W // tn),
        in_specs=[
            pl.BlockSpec((tm, d), lambda i, j: (i, 0)),
            pl.BlockSpec((1, N_MOD, d), lambda i, j: (mod_row(i, tm), 0, 0)),
            pl.BlockSpec((1, d), lambda i, j: (0, 0)),
            pl.BlockSpec((None, d, tn), lambda i, j: (layer, 0, j)),
        ],
        out_specs=pl.BlockSpec((tm, tn), lambda i, j: (i, j)),
        out_shape=jax.ShapeDtypeStruct((r, P_W), F32),
        scratch_shapes=[pltpu.VMEM((tm, d), BF16)],
        compiler_params=_cparams(("parallel", "arbitrary"), 56),
        name="inproj",
    )(x, mod, g, w)


def _outproj_kernel(x_ref, yr_ref, yc_ref, ym_ref, mod_ref, g_ref, w_ref, o_ref):
    y = _dot(yr_ref[...], w_ref[0:RET_W, :])
    y += _dot(yc_ref[...], w_ref[RET_W:RET_W + CONV_CH, :])
    y += _dot(ym_ref[...], w_ref[RET_W + CONV_CH:, :])
    o_ref[...] = y
    _postnorm_residual(o_ref, x_ref, o_ref, g_ref[...] * mod_ref[0, 5:6, :])


def _outproj(x, y_ret, y_conv, y_mla, mod, g, w, layer, mod_row, tm):
    r, d = x.shape
    return pl.pallas_call(
        _outproj_kernel,
        grid=(r // tm,),
        in_specs=[
            pl.BlockSpec((tm, d), lambda i: (i, 0)),
            pl.BlockSpec((tm, RET_W), lambda i: (i, 0)),
            pl.BlockSpec((tm, CONV_CH), lambda i: (i, 0)),
            pl.BlockSpec((tm, MLA_W), lambda i: (i, 0)),
            pl.BlockSpec((1, N_MOD, d), lambda i: (mod_row(i, tm), 0, 0)),
            pl.BlockSpec((1, d), lambda i: (0, 0)),
            pl.BlockSpec((None, d, d), lambda i: (layer, 0, 0), pipeline_mode=pl.Buffered(1)),
        ],
        out_specs=pl.BlockSpec((tm, d), lambda i: (i, 0)),
        out_shape=jax.ShapeDtypeStruct((r, d), F32),
        compiler_params=_cparams(("parallel",), 60),
        name="outproj",
    )(x, y_ret, y_conv, y_mla, mod, g, w)


def _ret_kernel(ql_ref, kl_ref, vl_ref, gl_ref, qc_ref, kc_ref, vc_ref, gc_ref,
                rc_ref, rs1_ref, rs2_ref, dm_ref, kf_ref, kb_ref, qf_ref, qb_ref, cd_ref, gn_ref,
                yl_ref, yc_ref, qr_ref, kr_ref, sf_ref, *, n_lat, n_ctx):
    c_len = RET_CHUNK
    lane = lax.broadcasted_iota(jnp.int32, (1, LANES), 1)
    head_masks = (lane < RET_DK, lane >= RET_DK)
    row = lax.broadcasted_iota(jnp.int32, (LANES, 2 * RET_DV), 0)
    col = lax.broadcasted_iota(jnp.int32, (LANES, 2 * RET_DV), 1)
    block_diag = (row < RET_DK) == (col < RET_DV)
    kf, kb, qf, qb, cd = kf_ref[0], kb_ref[0], qf_ref[0], qb_ref[0], cd_ref[0]
    gn = gn_ref[...]
    zero_state = jnp.zeros((LANES, 2 * RET_DV), F32)

    def kv_state(k, v16, dec):
        return jnp.where(block_diag, _dot_tn((k * dec).astype(BF16), v16), 0.0)

    def chunk_out(q, k, v16, g, sf16, sb16):
        k16 = k.astype(BF16)
        parts = []
        for e in range(2):
            q16 = jnp.where(head_masks[e], q, 0.0).astype(BF16)
            a = _dot_nt(q16, k16) * dm_ref[0, e]
            parts.append(_dot(a.astype(BF16), v16[:, e * RET_DV:(e + 1) * RET_DV]))
        o = jnp.concatenate(parts, axis=1)
        o = o + _dot((q * qf).astype(BF16), sf16) + _dot((q * qb).astype(BF16), sb16)
        normed = []
        for e in range(2):
            oe = o[:, e * RET_DV:(e + 1) * RET_DV]
            dlt = oe - jnp.mean(oe, axis=-1, keepdims=True)
            var = jnp.mean(dlt * dlt, axis=-1, keepdims=True)
            normed.append(dlt * lax.rsqrt(var + EPS))
        on = jnp.concatenate(normed, axis=1) * gn
        return ((g * _sigmoid(g)) * on).astype(BF16)

    cq = [qc_ref[c * c_len:(c + 1) * c_len, :] for c in range(n_ctx)]
    ck = [kc_ref[c * c_len:(c + 1) * c_len, :] for c in range(n_ctx)]
    cv = [vc_ref[c * c_len:(c + 1) * c_len, :].astype(BF16) for c in range(n_ctx)]
    kvf = [kv_state(ck[c], cv[c], kf) for c in range(n_ctx)]
    kvb = [kv_state(ck[c], cv[c], kb) for c in range(n_ctx)]
    sf_list = [zero_state]
    for c in range(n_ctx):
        sf_list.append(cd * sf_list[c] + kvf[c])
    sb_list = [zero_state] * (n_ctx + 1)
    for c in range(n_ctx - 1, -1, -1):
        sb_list[c] = cd * sb_list[c + 1] + kvb[c]
    for c in range(n_ctx):
        yc_ref[c * c_len:(c + 1) * c_len, :] = chunk_out(
            cq[c], ck[c], cv[c], gc_ref[c * c_len:(c + 1) * c_len, :],
            sf_list[c].astype(BF16), sb_list[c + 1].astype(BF16))
    s0f, s0b = sf_list[n_ctx], sb_list[0]

    def fwd(c, sf):
        r0 = pl.multiple_of(c * c_len, c_len)
        rows = pl.ds(r0, c_len)
        rc, rs1, rs2 = rc_ref[rows, :], rs1_ref[rows, :], rs2_ref[rows, :]
        q = _rope(ql_ref[rows, :], rc, rs1, rs2)
        k = _rope(kl_ref[rows, :], rc, rs1, rs2)
        qr_ref[rows, :] = q
        kr_ref[rows, :] = k
        sf_ref[c] = sf.astype(BF16)
        return cd * sf + kv_state(k, vl_ref[rows, :].astype(BF16), kf)

    lax.fori_loop(0, n_lat, fwd, s0f, unroll=RET_UNROLL)

    def bwd(i, sb):
        c = n_lat - 1 - i
        r0 = pl.multiple_of(c * c_len, c_len)
        rows = pl.ds(r0, c_len)
        q, k = qr_ref[rows, :], kr_ref[rows, :]
        v16 = vl_ref[rows, :].astype(BF16)
        yl_ref[rows, :] = chunk_out(q, k, v16, gl_ref[rows, :], sf_ref[c], sb.astype(BF16))
        return cd * sb + kv_state(k, v16, kb)

    lax.fori_loop(0, n_lat, bwd, s0b, unroll=RET_UNROLL)


def _retention(p_lat, p_ctx, tabs, gn, batch):
    t, lc = SEQ, CTX_LEN
    n_pairs = RET_HEADS // 2
    n_lat, n_ctx = t // RET_CHUNK, lc // RET_CHUNK
    qw, vw = 2 * RET_DK, 2 * RET_DV

    def pspec(rows, width, col0):
        return pl.BlockSpec((rows, width), lambda b, hp: (b, col0 // width + hp))

    tab3 = pl.BlockSpec((1, LANES, LANES), lambda b, hp: (hp, 0, 0))
    rope_spec = pl.BlockSpec((t, LANES), lambda b, hp: (0, 0))
    return pl.pallas_call(
        functools.partial(_ret_kernel, n_lat=n_lat, n_ctx=n_ctx),
        grid=(batch, n_pairs),
        in_specs=[
            pspec(t, qw, P_RQ), pspec(t, qw, P_RK), pspec(t, vw, P_RV), pspec(t, vw, P_RG),
            pspec(lc, qw, P_RQ), pspec(lc, qw, P_RK), pspec(lc, vw, P_RV), pspec(lc, vw, P_RG),
            rope_spec, rope_spec, rope_spec,
            pl.BlockSpec((1, 2, LANES, LANES), lambda b, hp: (hp, 0, 0, 0)),
            tab3, tab3, tab3, tab3,
            pl.BlockSpec((1, LANES, vw), lambda b, hp: (hp, 0, 0)),
            pl.BlockSpec((1, vw), lambda b, hp: (0, hp)),
        ],
        out_specs=[
            pl.BlockSpec((t, vw), lambda b, hp: (b, hp)),
            pl.BlockSpec((lc, vw), lambda b, hp: (b, hp)),
        ],
        out_shape=[
            jax.ShapeDtypeStruct((batch * t, RET_W), BF16),
            jax.ShapeDtypeStruct((batch * lc, RET_W), BF16),
        ],
        scratch_shapes=[
            pltpu.VMEM((t, LANES), F32),
            pltpu.VMEM((t, LANES), F32),
            pltpu.VMEM((n_lat, LANES, vw), BF16),
        ],
        compiler_params=_cparams(("parallel", "parallel"), 48),
        name="retention",
    )(p_lat, p_lat, p_lat, p_lat, p_ctx, p_ctx, p_ctx, p_ctx,
      tabs["ret_c"], tabs["ret_s1"], tabs["ret_s2"],
      tabs["dm"], tabs["kf"], tabs["kb"], tabs["qf"], tabs["qb"], tabs["cd"], gn)


def _conv_kernel(glu_ref, dw_ref, misc_ref, pw_ref, o_ref, pad_ref, *, t, rt):
    half = CONV_W // 2
    lo = 16
    zeros = jnp.zeros((lo, CONV_CH), F32)
    pad_ref[0:lo, :] = zeros
    pad_ref[lo + t:lo + t + lo, :] = zeros

    def glu(r, carry):
        rows = pl.ds(pl.multiple_of(r * rt, rt), rt)
        a = glu_ref[rows, 0:CONV_CH]
        b = glu_ref[rows, CONV_CH:2 * CONV_CH]
        pad_ref[pl.ds(pl.multiple_of(r * rt + lo, 8), rt), :] = a * _sigmoid(b)
        return carry

    lax.fori_loop(0, t // rt, glu, 0)

    def conv(r, carry):
        base = pl.multiple_of(r * rt, rt)
        win_rows = rt + 2 * lo
        cg_w = 2 * LANES
        accs = []
        for cg in range(CONV_CH // cg_w):
            cols = slice(cg * cg_w, (cg + 1) * cg_w)
            win = pad_ref[pl.ds(base, win_rows), cols]
            acc = jnp.zeros((rt, cg_w), F32) + misc_ref[0:1, cols]
            for phase in range(SUBLANES):
                shifted = win if phase == 0 else pltpu.roll(win, win_rows - phase, 0)
                for off in range(phase, lo + half + 1, SUBLANES):
                    k = off - (lo - half)
                    if 0 <= k < CONV_W:
                        a0 = off - phase
                        acc = acc + shifted[a0:a0 + rt, :] * dw_ref[k:k + 1, cols]
            accs.append(acc)
        acc = jnp.concatenate(accs, axis=1)
        dlt = acc - jnp.mean(acc, axis=-1, keepdims=True)
        var = jnp.mean(dlt * dlt, axis=-1, keepdims=True)
        y = dlt * lax.rsqrt(var + EPS) * misc_ref[1:2, :] + misc_ref[2:3, :]
        y = y * _sigmoid(y)
        o_ref[pl.ds(base, rt), :] = _dot(y.astype(BF16), pw_ref[...]).astype(BF16)
        return carry

    lax.fori_loop(0, t // rt, conv, 0)


def _conv(p, dw, misc, pw, batch, t):
    rt = 64
    return pl.pallas_call(
        functools.partial(_conv_kernel, t=t, rt=rt),
        grid=(batch,),
        in_specs=[
            pl.BlockSpec((t, 2 * CONV_CH), lambda b: (b, P_GLU // (2 * CONV_CH))),
            pl.BlockSpec((CONV_W, CONV_CH), lambda b: (0, 0)),
            pl.BlockSpec((3, CONV_CH), lambda b: (0, 0)),
            pl.BlockSpec((CONV_CH, CONV_CH), lambda b: (0, 0)),
        ],
        out_specs=pl.BlockSpec((t, CONV_CH), lambda b: (b, 0)),
        out_shape=jax.ShapeDtypeStruct((batch * t, CONV_CH), BF16),
        scratch_shapes=[pltpu.VMEM((t + 32, CONV_CH), F32)],
        compiler_params=_cparams(("parallel",), 40),
        name="conv",
    )(p, dw, misc, pw)


def _mla_proj_kernel(cq_ref, ckv_ref, kr_ref, nq_ref, nkv_ref, wq_ref, wkv_ref, *rest, rope):
    if rope:
        rc_ref, rs1_ref, rs2_ref, q_ref, k_ref, v_ref = rest
        rot = lambda z: _rope(z, rc_ref[...], rs1_ref[...], rs2_ref[...])
    else:
        q_ref, k_ref, v_ref = rest
        rot = lambda z: z
    qq = _dot(_rms(cq_ref[...], nq_ref[...]).astype(BF16), wq_ref[...]) * (MLA_SCALE * LOG2_E)
    kv = _dot(_rms(ckv_ref[...], nkv_ref[...]).astype(BF16), wkv_ref[...])
    kr = rot(kr_ref[...]).astype(BF16)
    hw = MLA_NOPE + LANES
    for h in range(MLA_HEADS):
        q_ref[:, h * hw:h * hw + MLA_NOPE] = qq[:, h * MLA_NOPE:(h + 1) * MLA_NOPE].astype(BF16)
        qr = qq[:, MLA_HEADS * MLA_NOPE + h * LANES:MLA_HEADS * MLA_NOPE + (h + 1) * LANES]
        q_ref[:, h * hw + MLA_NOPE:(h + 1) * hw] = rot(qr).astype(BF16)
        k_ref[:, h * hw:h * hw + MLA_NOPE] = kv[:, 2 * h * LANES:(2 * h + 1) * LANES].astype(BF16)
        k_ref[:, h * hw + MLA_NOPE:(h + 1) * hw] = kr
        v_ref[:, h * MLA_DV:(h + 1) * MLA_DV] = kv[:, (2 * h + 1) * LANES:(2 * h + 2) * LANES].astype(BF16)


def _mla_proj(p, nq, nkv, wq, wkv, rope_tabs, tm):
    r = p.shape[0]
    hw = MLA_NOPE + LANES
    rope = rope_tabs is not None
    in_specs = [
        pl.BlockSpec((tm, MLA_Q_RANK), lambda i: (i, P_CQ // MLA_Q_RANK)),
        pl.BlockSpec((tm, MLA_KV_RANK), lambda i: (i, P_CKV // MLA_KV_RANK)),
        pl.BlockSpec((tm, LANES), lambda i: (i, P_KR // LANES)),
        pl.BlockSpec((1, MLA_Q_RANK), lambda i: (0, 0)),
        pl.BlockSpec((1, MLA_KV_RANK), lambda i: (0, 0)),
        pl.BlockSpec(wq.shape, lambda i: (0, 0)),
        pl.BlockSpec(wkv.shape, lambda i: (0, 0)),
    ]
    args = [p, p, p, nq, nkv, wq, wkv]
    if rope:
        n_pos = SEQ // tm
        in_specs += [pl.BlockSpec((tm, LANES), lambda i: (i % n_pos, 0))] * 3
        args += list(rope_tabs)
    return pl.pallas_call(
        functools.partial(_mla_proj_kernel, rope=rope),
        grid=(r // tm,),
        in_specs=in_specs,
        out_specs=[
            pl.BlockSpec((tm, MLA_HEADS * hw), lambda i: (i, 0)),
            pl.BlockSpec((tm, MLA_HEADS * hw), lambda i: (i, 0)),
            pl.BlockSpec((tm, MLA_W), lambda i: (i, 0)),
        ],
        out_shape=[
            jax.ShapeDtypeStruct((r, MLA_HEADS * hw), BF16),
            jax.ShapeDtypeStruct((r, MLA_HEADS * hw), BF16),
            jax.ShapeDtypeStruct((r, MLA_W), BF16),
        ],
        compiler_params=_cparams(("parallel",), 40),
        name="mla_proj",
    )(*args)


def _attn_kernel(q_ref, kl_ref, kc_ref, vl_ref, vc_ref, o_ref, *, n_heads, sub):
    hw = MLA_NOPE + LANES
    for e in range(n_heads):
        kcols = slice(e * hw, (e + 1) * hw)
        vcols = slice(e * MLA_DV, (e + 1) * MLA_DV)
        for i in range(q_ref.shape[0] // sub):
            rows = slice(i * sub, (i + 1) * sub)
            q = q_ref[rows, kcols]
            sl = _dot_nt(q, kl_ref[:, kcols])
            sc = _dot_nt(q, kc_ref[:, kcols])
            m = jnp.maximum(jnp.max(sl, axis=-1, keepdims=True), jnp.max(sc, axis=-1, keepdims=True))
            el = jnp.exp2(sl - m)
            ec = jnp.exp2(sc - m)
            den = jnp.sum(el, axis=-1, keepdims=True) + jnp.sum(ec, axis=-1, keepdims=True)
            o = _dot(el.astype(BF16), vl_ref[:, vcols]) + _dot(ec.astype(BF16), vc_ref[:, vcols])
            o_ref[rows, vcols] = (o / den).astype(BF16)


def _attn(q, k_lat, k_ctx, v_lat, v_ctx, batch, n_heads, sub):
    hw = n_heads * (MLA_NOPE + LANES)
    vw = n_heads * MLA_DV
    return pl.pallas_call(
        functools.partial(_attn_kernel, n_heads=n_heads, sub=sub),
        grid=(batch, MLA_HEADS // n_heads),
        in_specs=[
            pl.BlockSpec((SEQ, hw), lambda b, h: (b, h)),
            pl.BlockSpec((SEQ, hw), lambda b, h: (b, h)),
            pl.BlockSpec((CTX_LEN, hw), lambda b, h: (b, h)),
            pl.BlockSpec((SEQ, vw), lambda b, h: (b, h)),
            pl.BlockSpec((CTX_LEN, vw), lambda b, h: (b, h)),
        ],
        out_specs=pl.BlockSpec((SEQ, vw), lambda b, h: (b, h)),
        out_shape=jax.ShapeDtypeStruct((batch * SEQ, MLA_W), BF16),
        compiler_params=_cparams(("parallel", "parallel"), 48),
        name="mla_attn",
    )(q, k_lat, k_ctx, v_lat, v_ctx)


def _attn_ctx_kernel(q_ref, k_ref, v_ref, o_ref):
    s = _dot_nt(q_ref[...], k_ref[...])
    e = jnp.exp2(s - jnp.max(s, axis=-1, keepdims=True))
    o = _dot(e.astype(BF16), v_ref[...])
    o_ref[...] = (o / jnp.sum(e, axis=-1, keepdims=True)).astype(BF16)


def _attn_ctx(q, k, v, batch):
    hw = MLA_NOPE + LANES
    return pl.pallas_call(
        _attn_ctx_kernel,
        grid=(batch, MLA_HEADS),
        in_specs=[
            pl.BlockSpec((CTX_LEN, hw), lambda b, h: (b, h)),
            pl.BlockSpec((CTX_LEN, hw), lambda b, h: (b, h)),
            pl.BlockSpec((CTX_LEN, MLA_DV), lambda b, h: (b, h)),
        ],
        out_specs=pl.BlockSpec((CTX_LEN, MLA_DV), lambda b, h: (b, h)),
        out_shape=jax.ShapeDtypeStruct((batch * CTX_LEN, MLA_W), BF16),
        compiler_params=_cparams(("parallel", "parallel"), 32),
        name="mla_attn_ctx",
    )(q, k, v)


def _tables():
    rows = SEQ // GRID_W
    quarter = ROPE_DIM // 4
    inv = ROPE_BASE ** (-jnp.arange(quarter, dtype=F32) / quarter)
    r_idx = jnp.repeat(jnp.arange(rows, dtype=F32), GRID_W)
    c_idx = jnp.tile(jnp.arange(GRID_W, dtype=F32), rows)
    ang = jnp.concatenate([r_idx[:, None] * inv, c_idx[:, None] * inv], axis=-1)
    cos, sin = jnp.cos(ang), jnp.sin(ang)
    z = jnp.zeros_like(cos)
    tabs = {
        "ret_c": jnp.concatenate([cos, cos, cos, cos], axis=1),
        "ret_s1": jnp.concatenate([-sin, z, -sin, z], axis=1),
        "ret_s2": jnp.concatenate([z, sin, z, sin], axis=1),
        "mla_c": jnp.concatenate([cos, cos, z, z], axis=1),
        "mla_s1": jnp.concatenate([-sin, z, z, z], axis=1),
        "mla_s2": jnp.concatenate([z, sin, z, z], axis=1),
    }
    c_len = RET_CHUNK
    log_g = jnp.log1p(-jnp.exp2(-5.0 - jnp.arange(RET_HEADS, dtype=F32)))
    i = jnp.arange(c_len, dtype=F32)
    k_scale = RET_DK ** -0.5
    lg = log_g[:, None, None]
    dm = jnp.exp(lg * jnp.abs(i[:, None] - i[None, :])) * k_scale
    tabs["dm"] = dm.reshape(RET_HEADS // 2, 2, c_len, c_len)

    def lanes_by_head(per_head_rows):
        t = jnp.repeat(per_head_rows[:, :, None], RET_DK, axis=2)
        t = t.reshape(RET_HEADS // 2, 2, c_len, RET_DK)
        return jnp.concatenate([t[:, 0], t[:, 1]], axis=-1)

    tabs["kf"] = lanes_by_head(jnp.exp(log_g[:, None] * (c_len - 1.0 - i)) * k_scale)
    tabs["kb"] = lanes_by_head(jnp.exp(log_g[:, None] * i) * k_scale)
    tabs["qf"] = lanes_by_head(jnp.exp(log_g[:, None] * (i + 1.0)))
    tabs["qb"] = lanes_by_head(jnp.exp(log_g[:, None] * (c_len - i)))
    cdec = jnp.exp(log_g * c_len).reshape(RET_HEADS // 2, 2)
    cd = jnp.repeat(cdec[:, :, None], RET_DK, axis=2).reshape(RET_HEADS // 2, 2 * RET_DK, 1)
    tabs["cd"] = jnp.broadcast_to(cd, (RET_HEADS // 2, 2 * RET_DK, 2 * RET_DV))
    return tabs


def _cast_kernel(x_ref, o_ref):
    o_ref[...] = x_ref[...].astype(BF16)


def _cast_bf16(w, tr):
    depth, r, c = w.shape
    return pl.pallas_call(
        _cast_kernel,
        grid=(depth, r // tr),
        in_specs=[pl.BlockSpec((1, tr, c), lambda l, i: (l, i, 0))],
        out_specs=pl.BlockSpec((1, tr, c), lambda l, i: (l, i, 0)),
        out_shape=jax.ShapeDtypeStruct(w.shape, BF16),
        compiler_params=_cparams(("parallel", "parallel"), 48),
        name="cast",
    )(w)


_W_IN_SRC = ((2304, 1024), (3328, 512), (768, 768), (1536, 768), (0, 384), (384, 384), (3840, 256))
_W_IN_KR = (4096, 64)


def _prep_w_in_kernel(w_ref, kr_ref, o_ref):
    dst = 0
    for src, width in _W_IN_SRC:
        o_ref[0, :, dst:dst + width] = w_ref[0, :, src:src + width].astype(BF16)
        dst += width
    o_ref[0, :, dst:dst + LANES] = kr_ref[0]


def _prep_w_in(w):
    depth, d, n = w.shape
    kr = w[..., _W_IN_KR[0]:_W_IN_KR[0] + _W_IN_KR[1]].astype(BF16)
    kr = jnp.pad(kr, ((0, 0), (0, 0), (0, LANES - _W_IN_KR[1])))
    tr = 512
    return pl.pallas_call(
        _prep_w_in_kernel,
        grid=(depth, d // tr),
        in_specs=[
            pl.BlockSpec((1, tr, n), lambda l, i: (l, i, 0)),
            pl.BlockSpec((1, tr, LANES), lambda l, i: (l, i, 0)),
        ],
        out_specs=pl.BlockSpec((1, tr, P_W), lambda l, i: (l, i, 0)),
        out_shape=jax.ShapeDtypeStruct((depth, d, P_W), BF16),
        compiler_params=_cparams(("parallel", "parallel"), 48),
        name="prep_w_in",
    )(w, kr)


def _prep_w_uq(w):
    w = w.reshape(MLA_Q_RANK, MLA_HEADS, MLA_NOPE + MLA_ROPE)
    nope = w[:, :, :MLA_NOPE].reshape(MLA_Q_RANK, MLA_HEADS * MLA_NOPE)
    rope = jnp.pad(w[:, :, MLA_NOPE:], ((0, 0), (0, 0), (0, LANES - MLA_ROPE)))
    return jnp.concatenate([nope, rope.reshape(MLA_Q_RANK, MLA_HEADS * LANES)], axis=1).astype(BF16)


def _lat_mod_row(i, tm):
    return (i * tm) // SEQ


def _ctx_mod_row(i, tm):
    return CTX_MOD_ROW


def kernel(x, c, ctx, c_ctx, w_ada, b_ada, norm_g, w_ffa_in, w_ffa_out, w_ffb_in, w_ffb_out,
           w_in, w_out, ret_gn, cv_dw, cv_dw_b, cv_ln_g, cv_ln_b, cv_pw,
           mla_q_norm, w_uq, mla_kv_norm, w_ukv):
    batch, seq, d = x.shape
    assert (seq, d, ctx.shape[1]) == (SEQ, D_MODEL, CTX_LEN)
    xl = x.reshape(batch * seq, d)
    xc = ctx.reshape(batch * CTX_LEN, d)
    s_in = jnp.concatenate([c, c_ctx[None, :], jnp.zeros((ROWS_PAD - batch - 1, d), F32)], axis=0)
    mod_all = _ada(s_in, w_ada, b_ada).reshape(DEPTH, ROWS_PAD, N_MOD, d)
    tabs = _tables()
    mla_rope = (tabs["mla_c"], tabs["mla_s1"], tabs["mla_s2"])

    wa_in, wa_out = _cast_bf16(w_ffa_in, 256), _cast_bf16(w_ffa_out, 704)
    wb_in, wb_out = _cast_bf16(w_ffb_in, 256), _cast_bf16(w_ffb_out, 704)
    wi = _prep_w_in(w_in)
    wo = _cast_bf16(w_out, 1024)

    ffn_lat = dict(mod_row=_lat_mod_row, tm=1024, tf=512, x_buffers=2)
    ffn_ctx = dict(mod_row=_ctx_mod_row, tm=1024, tf=512, x_buffers=2)
    for l in range(DEPTH):
        last = l == DEPTH - 1
        mod = mod_all[l]
        ng = norm_g[l]
        wq = _prep_w_uq(w_uq[l])
        wkv = w_ukv[l].astype(BF16)
        nq, nkv = mla_q_norm[l][None, :], mla_kv_norm[l][None, :]
        gn = ret_gn[l][None, :]
        conv_misc = jnp.stack([cv_dw_b[l], cv_ln_g[l], cv_ln_b[l]], axis=0)
        pw = cv_pw[l].astype(BF16)

        xl = _ffn(xl, mod, ng[0:2], wa_in, wa_out, l, 0, **ffn_lat)
        xc = _ffn(xc, mod, ng[0:2], wa_in, wa_out, l, 0, **ffn_ctx)

        p_lat = _inproj(xl, mod, ng[2:3], wi, l, _lat_mod_row, 1024, 1408)
        p_ctx = _inproj(xc, mod, ng[2:3], wi, l, _ctx_mod_row, 1024, 1408)

        y_ret_l, y_ret_c = _retention(p_lat, p_ctx, tabs, gn, batch)
        y_conv_l = _conv(p_lat, cv_dw[l], conv_misc, pw, batch, SEQ)
        q_l, k_l, v_l = _mla_proj(p_lat, nq, nkv, wq, wkv, mla_rope, 512)
        q_c, k_c, v_c = _mla_proj(p_ctx, nq, nkv, wq, wkv, None, 512)
        y_mla_l = _attn(q_l, k_l, k_c, v_l, v_c, batch, 2, 512)

        xl = _outproj(xl, y_ret_l, y_conv_l, y_mla_l, mod, ng[3:4], wo, l, _lat_mod_row, 1024)
        xl = _ffn(xl, mod, ng[4:6], wb_in, wb_out, l, 6, **ffn_lat)
        if not last:
            y_conv_c = _conv(p_ctx, cv_dw[l], conv_misc, pw, batch, CTX_LEN)
            y_mla_c = _attn_ctx(q_c, k_c, v_c, batch)
            xc = _outproj(xc, y_ret_c, y_conv_c, y_mla_c, mod, ng[3:4], wo, l, _ctx_mod_row, 1024)
            xc = _ffn(xc, mod, ng[4:6], wb_in, wb_out, l, 6, **ffn_ctx)
    return xl.reshape(batch, seq, d)
```

```python
import functools

import jax
import jax.numpy as jnp
from jax import lax
from jax.experimental import pallas as pl
from jax.experimental.pallas import tpu as pltpu

F32 = jnp.float32
BF16 = jnp.bfloat16

D_MODEL = 2048
SEQ = 2048
DEPTH = 2
GRID_W = 64
CTX_LEN = 256
D_FF = 5632
FFN_RES = 0.5
N_MOD = 9
ROPE_DIM = 64
ROPE_BASE = 10000.0
EPS = 1e-6
RET_HEADS = 6
RET_DK = 64
RET_DV = 128
RET_CHUNK = 128
CONV_CH = 512
CONV_W = 31
MLA_HEADS = 6
MLA_Q_RANK = 512
MLA_KV_RANK = 256
MLA_NOPE = 128
MLA_ROPE = 64
MLA_DV = 128
MLA_SCALE = (MLA_NOPE + MLA_ROPE) ** -0.5
LOG2_E = 1.4426950408889634
RET_W = RET_HEADS * RET_DV
MLA_W = MLA_HEADS * MLA_DV

LANES = 128
MIB = 1024 * 1024
ROW_CHUNK = 128
RET_UNROLL = 8
SUBLANES = 8

P_GLU, P_CQ, P_RV, P_RG, P_RQ, P_RK, P_CKV, P_KR = 0, 1024, 1536, 2304, 3072, 3456, 3840, 4096
P_W = 4224
CTX_MOD_ROW = 4
ROWS_PAD = 8


def _cparams(sem, vmem_mib):
    return pltpu.CompilerParams(dimension_semantics=sem, vmem_limit_bytes=int(vmem_mib * MIB))


def _sigmoid(x):
    return 1.0 / (1.0 + jnp.exp(-x))


def _rms(x, g):
    ms = jnp.mean(x * x, axis=-1, keepdims=True)
    return x * lax.rsqrt(ms + EPS) * g


def _dot(a, b):
    return jnp.dot(a, b, preferred_element_type=F32)


def _prenorm_modulate(x_ref, h_ref, gain_row, shift_row):
    def body(i, carry):
        rows = pl.ds(pl.multiple_of(i * ROW_CHUNK, ROW_CHUNK), ROW_CHUNK)
        x = x_ref[rows, :]
        r = lax.rsqrt(jnp.mean(x * x, axis=-1, keepdims=True) + EPS)
        h_ref[rows, :] = (x * r * gain_row + shift_row).astype(BF16)
        return carry

    lax.fori_loop(0, x_ref.shape[0] // ROW_CHUNK, body, 0)


def _postnorm_residual(y_ref, x_ref, o_ref, gain_row):
    def body(i, carry):
        rows = pl.ds(pl.multiple_of(i * ROW_CHUNK, ROW_CHUNK), ROW_CHUNK)
        y = y_ref[rows, :]
        r = lax.rsqrt(jnp.mean(y * y, axis=-1, keepdims=True) + EPS)
        o_ref[rows, :] = x_ref[rows, :] + y * r * gain_row
        return carry

    lax.fori_loop(0, y_ref.shape[0] // ROW_CHUNK, body, 0)


def _dot_nt(a, b):
    return lax.dot_general(a, b, (((1,), (1,)), ((), ())), preferred_element_type=F32)


def _dot_tn(a, b):
    return lax.dot_general(a, b, (((0,), (0,)), ((), ())), preferred_element_type=F32)


def _rope(x, c, s1, s2):
    return x * c + pltpu.roll(x, 96, 1) * s1 + pltpu.roll(x, 32, 1) * s2


def _ada_kernel(s_ref, w_ref, b_ref, o_ref):
    s = s_ref[...]
    s = s * _sigmoid(s)
    o_ref[0] = _dot(s.astype(BF16), w_ref[0].astype(BF16)) + b_ref[0]


def _ada(s_in, w_ada, b_ada):
    depth, d, n = w_ada.shape
    tn = 1024
    return pl.pallas_call(
        _ada_kernel,
        grid=(depth, n // tn),
        in_specs=[
            pl.BlockSpec((ROWS_PAD, d), lambda l, j: (0, 0)),
            pl.BlockSpec((1, d, tn), lambda l, j: (l, 0, j)),
            pl.BlockSpec((1, 1, tn), lambda l, j: (l, 0, j)),
        ],
        out_specs=pl.BlockSpec((1, ROWS_PAD, tn), lambda l, j: (l, 0, j)),
        out_shape=jax.ShapeDtypeStruct((depth, ROWS_PAD, n), F32),
        compiler_params=_cparams(("parallel", "parallel"), 40),
        name="ada",
    )(s_in, w_ada, b_ada.reshape(depth, 1, n))


def _ffn_kernel(x_ref, mod_ref, g_ref, wa_ref, wu_ref, wo_ref, o_ref, h_ref, *, k0, nj):
    j = pl.program_id(1)

    @pl.when(j == 0)
    def _():
        gain = g_ref[0:1, :] * (1.0 + mod_ref[0, k0 + 1:k0 + 2, :])
        _prenorm_modulate(x_ref, h_ref, gain, mod_ref[0, k0:k0 + 1, :])
        o_ref[...] = jnp.zeros_like(o_ref)

    h = h_ref[...]
    a = _dot(h, wa_ref[...].astype(BF16))
    u = _dot(h, wu_ref[...].astype(BF16))
    act = (a * _sigmoid(a) * u).astype(BF16)
    o_ref[...] += _dot(act, wo_ref[...].astype(BF16))

    @pl.when(j == nj - 1)
    def _():
        gain = g_ref[1:2, :] * (FFN_RES * mod_ref[0, k0 + 2:k0 + 3, :])
        _postnorm_residual(o_ref, x_ref, o_ref, gain)


def _ffn(x, mod, g2, w_in, w_out, layer, k0, mod_row, tm, tf, x_buffers):
    r, d = x.shape
    nj = D_FF // tf
    return pl.pallas_call(
        functools.partial(_ffn_kernel, k0=k0, nj=nj),
        grid=(r // tm, nj),
        in_specs=[
            pl.BlockSpec((tm, d), lambda i, j: (i, 0), pipeline_mode=pl.Buffered(x_buffers)),
            pl.BlockSpec((1, N_MOD, d), lambda i, j: (mod_row(i, tm), 0, 0)),
            pl.BlockSpec((2, d), lambda i, j: (0, 0)),
            pl.BlockSpec((None, d, tf), lambda i, j: (layer, 0, j)),
            pl.BlockSpec((None, d, tf), lambda i, j: (layer, 0, nj + j)),
            pl.BlockSpec((None, tf, d), lambda i, j: (layer, j, 0)),
        ],
        out_specs=pl.BlockSpec((tm, d), lambda i, j: (i, 0)),
        out_shape=jax.ShapeDtypeStruct((r, d), F32),
        scratch_shapes=[pltpu.VMEM((tm, d), BF16)],
        compiler_params=_cparams(("parallel", "arbitrary"), 60),
        name="ffn",
    )(x, mod, g2, w_in, w_in, w_out)


def _inproj_kernel(x_ref, mod_ref, g_ref, w_ref, o_ref, h_ref):
    @pl.when(pl.program_id(1) == 0)
    def _():
        gain = g_ref[...] * (1.0 + mod_ref[0, 4:5, :])
        _prenorm_modulate(x_ref, h_ref, gain, mod_ref[0, 3:4, :])

    o_ref[...] = _dot(h_ref[...], w_ref[...])


def _inproj(x, mod, g, w, layer, mod_row, tm, tn):
    r, d = x.shape
    return pl.pallas_call(
        _inproj_kernel,
        grid=(r // tm, P_W // tn),
        in_specs=[
            pl.BlockSpec((tm, d), lambda i, j: (i, 0)),
            pl.BlockSpec((1, N_MOD, d), lambda i, j: (mod_row(i, tm), 0, 0)),
            pl.BlockSpec((1, d), lambda i, j: (0, 0)),
            pl.BlockSpec((None, d, tn), lambda i, j: (layer, 0, j)),
        ],
        out_specs=pl.BlockSpec((tm, tn), lambda i, j: (i, j)),
        out_shape=jax.ShapeDtypeStruct((r, P_W), F32),
        scratch_shapes=[pltpu.VMEM((tm, d), BF16)],
        compiler_params=_cparams(("parallel", "arbitrary"), 56),
        name="inproj",
    )(x, mod, g, w)


def _outproj_kernel(x_ref, yr_ref, yc_ref, ym_ref, mod_ref, g_ref, w_ref, o_ref):
    y = _dot(yr_ref[...], w_ref[0:RET_W, :])
    y += _dot(yc_ref[...], w_ref[RET_W:RET_W + CONV_CH, :])
    y += _dot(ym_ref[...], w_ref[RET_W + CONV_CH:, :])
    o_ref[...] = y
    _postnorm_residual(o_ref, x_ref, o_ref, g_ref[...] * mod_ref[0, 5:6, :])


def _outproj(x, y_ret, y_conv, y_mla, mod, g, w, layer, mod_row, tm):
    r, d = x.shape
    return pl.pallas_call(
        _outproj_kernel,
        grid=(r // tm,),
        in_specs=[
            pl.BlockSpec((tm, d), lambda i: (i, 0)),
            pl.BlockSpec((tm, RET_W), lambda i: (i, 0)),
            pl.BlockSpec((tm, CONV_CH), lambda i: (i, 0)),
            pl.BlockSpec((tm, MLA_W), lambda i: (i, 0)),
            pl.BlockSpec((1, N_MOD, d), lambda i: (mod_row(i, tm), 0, 0)),
            pl.BlockSpec((1, d), lambda i: (0, 0)),
            pl.BlockSpec((None, d, d), lambda i: (layer, 0, 0), pipeline_mode=pl.Buffered(1)),
        ],
        out_specs=pl.BlockSpec((tm, d), lambda i: (i, 0)),
        out_shape=jax.ShapeDtypeStruct((r, d), F32),
        compiler_params=_cparams(("parallel",), 60),
        name="outproj",
    )(x, y_ret, y_conv, y_mla, mod, g, w)


def _ret_kernel(ql_ref, kl_ref, vl_ref, gl_ref, qc_ref, kc_ref, vc_ref, gc_ref,
                rc_ref, rs1_ref, rs2_ref, dm_ref, kf_ref, kb_ref, qf_ref, qb_ref, cd_ref, gn_ref,
                yl_ref, yc_ref, qr_ref, kr_ref, sf_ref, *, n_lat, n_ctx):
    c_len = RET_CHUNK
    lane = lax.broadcasted_iota(jnp.int32, (1, LANES), 1)
    head_masks = (lane < RET_DK, lane >= RET_DK)
    row = lax.broadcasted_iota(jnp.int32, (LANES, 2 * RET_DV), 0)
    col = lax.broadcasted_iota(jnp.int32, (LANES, 2 * RET_DV), 1)
    block_diag = (row < RET_DK) == (col < RET_DV)
    kf, kb, qf, qb, cd = kf_ref[0], kb_ref[0], qf_ref[0], qb_ref[0], cd_ref[0]
    gn = gn_ref[...]
    zero_state = jnp.zeros((LANES, 2 * RET_DV), F32)

    def kv_state(k, v16, dec):
        return jnp.where(block_diag, _dot_tn((k * dec).astype(BF16), v16), 0.0)

    def chunk_out(q, k, v16, g, sf16, sb16):
        k16 = k.astype(BF16)
        parts = []
        for e in range(2):
            q16 = jnp.where(head_masks[e], q, 0.0).astype(BF16)
            a = _dot_nt(q16, k16) * dm_ref[0, e]
            parts.append(_dot(a.astype(BF16), v16[:, e * RET_DV:(e + 1) * RET_DV]))
        o = jnp.concatenate(parts, axis=1)
        o = o + _dot((q * qf).astype(BF16), sf16) + _dot((q * qb).astype(BF16), sb16)
        normed = []
        for e in range(2):
            oe = o[:, e * RET_DV:(e + 1) * RET_DV]
            dlt = oe - jnp.mean(oe, axis=-1, keepdims=True)
            var = jnp.mean(dlt * dlt, axis=-1, keepdims=True)
            normed.append(dlt * lax.rsqrt(var + EPS))
        on = jnp.concatenate(normed, axis=1) * gn
        return ((g * _sigmoid(g)) * on).astype(BF16)

    cq = [qc_ref[c * c_len:(c + 1) * c_len, :] for c in range(n_ctx)]
    ck = [kc_ref[c * c_len:(c + 1) * c_len, :] for c in range(n_ctx)]
    cv = [vc_ref[c * c_len:(c + 1) * c_len, :].astype(BF16) for c in range(n_ctx)]
    kvf = [kv_state(ck[c], cv[c], kf) for c in range(n_ctx)]
    kvb = [kv_state(ck[c], cv[c], kb) for c in range(n_ctx)]
    sf_list = [zero_state]
    for c in range(n_ctx):
        sf_list.append(cd * sf_list[c] + kvf[c])
    sb_list = [zero_state] * (n_ctx + 1)
    for c in range(n_ctx - 1, -1, -1):
        sb_list[c] = cd * sb_list[c + 1] + kvb[c]
    for c in range(n_ctx):
        yc_ref[c * c_len:(c + 1) * c_len, :] = chunk_out(
            cq[c], ck[c], cv[c], gc_ref[c * c_len:(c + 1) * c_len, :],
            sf_list[c].astype(BF16), sb_list[c + 1].astype(BF16))
    s0f, s0b = sf_list[n_ctx], sb_list[0]

    def fwd(c, sf):
        r0 = pl.multiple_of(c * c_len, c_len)
        rows = pl.ds(r0, c_len)
        rc, rs1, rs2 = rc_ref[rows, :], rs1_ref[rows, :], rs2_ref[rows, :]
        q = _rope(ql_ref[rows, :], rc, rs1, rs2)
        k = _rope(kl_ref[rows, :], rc, rs1, rs2)
        qr_ref[rows, :] = q
        kr_ref[rows, :] = k
        sf_ref[c] = sf.astype(BF16)
        return cd * sf + kv_state(k, vl_ref[rows, :].astype(BF16), kf)

    lax.fori_loop(0, n_lat, fwd, s0f, unroll=RET_UNROLL)

    def bwd(i, sb):
        c = n_lat - 1 - i
        r0 = pl.multiple_of(c * c_len, c_len)
        rows = pl.ds(r0, c_len)
        q, k = qr_ref[rows, :], kr_ref[rows, :]
        v16 = vl_ref[rows, :].astype(BF16)
        yl_ref[rows, :] = chunk_out(q, k, v16, gl_ref[rows, :], sf_ref[c], sb.astype(BF16))
        return cd * sb + kv_state(k, v16, kb)

    lax.fori_loop(0, n_lat, bwd, s0b, unroll=RET_UNROLL)


def _retention(p_lat, p_ctx, tabs, gn, batch):
    t, lc = SEQ, CTX_LEN
    n_pairs = RET_HEADS // 2
    n_lat, n_ctx = t // RET_CHUNK, lc // RET_CHUNK
    qw, vw = 2 * RET_DK, 2 * RET_DV

    def pspec(rows, width, col0):
        return pl.BlockSpec((rows, width), lambda b, hp: (b, col0 // width + hp))

    tab3 = pl.BlockSpec((1, LANES, LANES), lambda b, hp: (hp, 0, 0))
    rope_spec = pl.BlockSpec((t, LANES), lambda b, hp: (0, 0))
    return pl.pallas_call(
        functools.partial(_ret_kernel, n_lat=n_lat, n_ctx=n_ctx),
        grid=(batch, n_pairs),
        in_specs=[
            pspec(t, qw, P_RQ), pspec(t, qw, P_RK), pspec(t, vw, P_RV), pspec(t, vw, P_RG),
            pspec(lc, qw, P_RQ), pspec(lc, qw, P_RK), pspec(lc, vw, P_RV), pspec(lc, vw, P_RG),
            rope_spec, rope_spec, rope_spec,
            pl.BlockSpec((1, 2, LANES, LANES), lambda b, hp: (hp, 0, 0, 0)),
            tab3, tab3, tab3, tab3,
            pl.BlockSpec((1, LANES, vw), lambda b, hp: (hp, 0, 0)),
            pl.BlockSpec((1, vw), lambda b, hp: (0, hp)),
        ],
        out_specs=[
            pl.BlockSpec((t, vw), lambda b, hp: (b, hp)),
            pl.BlockSpec((lc, vw), lambda b, hp: (b, hp)),
        ],
        out_shape=[
            jax.ShapeDtypeStruct((batch * t, RET_W), BF16),
            jax.ShapeDtypeStruct((batch * lc, RET_W), BF16),
        ],
        scratch_shapes=[
            pltpu.VMEM((t, LANES), F32),
            pltpu.VMEM((t, LANES), F32),
            pltpu.VMEM((n_lat, LANES, vw), BF16),
        ],
        compiler_params=_cparams(("parallel", "parallel"), 48),
        name="retention",
    )(p_lat, p_lat, p_lat, p_lat, p_ctx, p_ctx, p_ctx, p_ctx,
      tabs["ret_c"], tabs["ret_s1"], tabs["ret_s2"],
      tabs["dm"], tabs["kf"], tabs["kb"], tabs["qf"], tabs["qb"], tabs["cd"], gn)


def _conv_kernel(glu_ref, dw_ref, misc_ref, pw_ref, o_ref, pad_ref, *, t, rt):
    half = CONV_W // 2
    lo = 16
    zeros = jnp.zeros((lo, CONV_CH), F32)
    pad_ref[0:lo, :] = zeros
    pad_ref[lo + t:lo + t + lo, :] = zeros

    def glu(r, carry):
        rows = pl.ds(pl.multiple_of(r * rt, rt), rt)
        a = glu_ref[rows, 0:CONV_CH]
        b = glu_ref[rows, CONV_CH:2 * CONV_CH]
        pad_ref[pl.ds(pl.multiple_of(r * rt + lo, 8), rt), :] = a * _sigmoid(b)
        return carry

    lax.fori_loop(0, t // rt, glu, 0)

    def conv(r, carry):
        base = pl.multiple_of(r * rt, rt)
        win_rows = rt + 2 * lo
        cg_w = 2 * LANES
        accs = []
        for cg in range(CONV_CH // cg_w):
            cols = slice(cg * cg_w, (cg + 1) * cg_w)
            win = pad_ref[pl.ds(base, win_rows), cols]
            acc = jnp.zeros((rt, cg_w), F32) + misc_ref[0:1, cols]
            for phase in range(SUBLANES):
                shifted = win if phase == 0 else pltpu.roll(win, win_rows - phase, 0)
                for off in range(phase, lo + half + 1, SUBLANES):
                    k = off - (lo - half)
                    if 0 <= k < CONV_W:
                        a0 = off - phase
                        acc = acc + shifted[a0:a0 + rt, :] * dw_ref[k:k + 1, cols]
            accs.append(acc)
        acc = jnp.concatenate(accs, axis=1)
        dlt = acc - jnp.mean(acc, axis=-1, keepdims=True)
        var = jnp.mean(dlt * dlt, axis=-1, keepdims=True)
        y = dlt * lax.rsqrt(var + EPS) * misc_ref[1:2, :] + misc_ref[2:3, :]
        y = y * _sigmoid(y)
        o_ref[pl.ds(base, rt), :] = _dot(y.astype(BF16), pw_ref[...]).astype(BF16)
        return carry

    lax.fori_loop(0, t // rt, conv, 0)


def _conv(p, dw, misc, pw, batch, t):
    rt = 64
    return pl.pallas_call(
        functools.partial(_conv_kernel, t=t, rt=rt),
        grid=(batch,),
        in_specs=[
            pl.BlockSpec((t, 2 * CONV_CH), lambda b: (b, P_GLU // (2 * CONV_CH))),
            pl.BlockSpec((CONV_W, CONV_CH), lambda b: (0, 0)),
            pl.BlockSpec((3, CONV_CH), lambda b: (0, 0)),
            pl.BlockSpec((CONV_CH, CONV_CH), lambda b: (0, 0)),
        ],
        out_specs=pl.BlockSpec((t, CONV_CH), lambda b: (b, 0)),
        out_shape=jax.ShapeDtypeStruct((batch * t, CONV_CH), BF16),
        scratch_shapes=[pltpu.VMEM((t + 32, CONV_CH), F32)],
        compiler_params=_cparams(("parallel",), 40),
        name="conv",
    )(p, dw, misc, pw)


def _mla_proj_kernel(cq_ref, ckv_ref, kr_ref, nq_ref, nkv_ref, wq_ref, wkv_ref, *rest, rope):
    if rope:
        rc_ref, rs1_ref, rs2_ref, q_ref, k_ref, v_ref = rest
        rot = lambda z: _rope(z, rc_ref[...], rs1_ref[...], rs2_ref[...])
    else:
        q_ref, k_ref, v_ref = rest
        rot = lambda z: z
    qq = _dot(_rms(cq_ref[...], nq_ref[...]).astype(BF16), wq_ref[...]) * (MLA_SCALE * LOG2_E)
    kv = _dot(_rms(ckv_ref[...], nkv_ref[...]).astype(BF16), wkv_ref[...])
    kr = rot(kr_ref[...]).astype(BF16)
    hw = MLA_NOPE + LANES
    for h in range(MLA_HEADS):
        q_ref[:, h * hw:h * hw + MLA_NOPE] = qq[:, h * MLA_NOPE:(h + 1) * MLA_NOPE].astype(BF16)
        qr = qq[:, MLA_HEADS * MLA_NOPE + h * LANES:MLA_HEADS * MLA_NOPE + (h + 1) * LANES]
        q_ref[:, h * hw + MLA_NOPE:(h + 1) * hw] = rot(qr).astype(BF16)
        k_ref[:, h * hw:h * hw + MLA_NOPE] = kv[:, 2 * h * LANES:(2 * h + 1) * LANES].astype(BF16)
        k_ref[:, h * hw + MLA_NOPE:(h + 1) * hw] = kr
        v_ref[:, h * MLA_DV:(h + 1) * MLA_DV] = kv[:, (2 * h + 1) * LANES:(2 * h + 2) * LANES].astype(BF16)


def _mla_proj(p, nq, nkv, wq, wkv, rope_tabs, tm):
    r = p.shape[0]
    hw = MLA_NOPE + LANES
    rope = rope_tabs is not None
    in_specs = [
        pl.BlockSpec((tm, MLA_Q_RANK), lambda i: (i, P_CQ // MLA_Q_RANK)),
        pl.BlockSpec((tm, MLA_KV_RANK), lambda i: (i, P_CKV // MLA_KV_RANK)),
        pl.BlockSpec((tm, LANES), lambda i: (i, P_KR // LANES)),
        pl.BlockSpec((1, MLA_Q_RANK), lambda i: (0, 0)),
        pl.BlockSpec((1, MLA_KV_RANK), lambda i: (0, 0)),
        pl.BlockSpec(wq.shape, lambda i: (0, 0)),
        pl.BlockSpec(wkv.shape, lambda i: (0, 0)),
    ]
    args = [p, p, p, nq, nkv, wq, wkv]
    if rope:
        n_pos = SEQ // tm
        in_specs += [pl.BlockSpec((tm, LANES), lambda i: (i % n_pos, 0))] * 3
        args += list(rope_tabs)
    return pl.pallas_call(
        functools.partial(_mla_proj_kernel, rope=rope),
        grid=(r // tm,),
        in_specs=in_specs,
        out_specs=[
            pl.BlockSpec((tm, MLA_HEADS * hw), lambda i: (i, 0)),
            pl.BlockSpec((tm, MLA_HEADS * hw), lambda i: (i, 0)),
            pl.BlockSpec((tm, MLA_W), lambda i: (i, 0)),
        ],
        out_shape=[
            jax.ShapeDtypeStruct((r, MLA_HEADS * hw), BF16),
            jax.ShapeDtypeStruct((r, MLA_HEADS * hw), BF16),
            jax.ShapeDtypeStruct((r, MLA_W), BF16),
        ],
        compiler_params=_cparams(("parallel",), 40),
        name="mla_proj",
    )(*args)


def _attn_kernel(q_ref, kl_ref, kc_ref, vl_ref, vc_ref, o_ref, *, n_heads, sub):
    hw = MLA_NOPE + LANES
    for e in range(n_heads):
        kcols = slice(e * hw, (e + 1) * hw)
        vcols = slice(e * MLA_DV, (e + 1) * MLA_DV)
        for i in range(q_ref.shape[0] // sub):
            rows = slice(i * sub, (i + 1) * sub)
            q = q_ref[rows, kcols]
            sl = _dot_nt(q, kl_ref[:, kcols])
            sc = _dot_nt(q, kc_ref[:, kcols])
            m = jnp.maximum(jnp.max(sl, axis=-1, keepdims=True), jnp.max(sc, axis=-1, keepdims=True))
            el = jnp.exp2(sl - m)
            ec = jnp.exp2(sc - m)
            den = jnp.sum(el, axis=-1, keepdims=True) + jnp.sum(ec, axis=-1, keepdims=True)
            o = _dot(el.astype(BF16), vl_ref[:, vcols]) + _dot(ec.astype(BF16), vc_ref[:, vcols])
            o_ref[rows, vcols] = (o / den).astype(BF16)


def _attn(q, k_lat, k_ctx, v_lat, v_ctx, batch, n_heads, sub):
    hw = n_heads * (MLA_NOPE + LANES)
    vw = n_heads * MLA_DV
    return pl.pallas_call(
        functools.partial(_attn_kernel, n_heads=n_heads, sub=sub),
        grid=(batch, MLA_HEADS // n_heads),
        in_specs=[
            pl.BlockSpec((SEQ, hw), lambda b, h: (b, h)),
            pl.BlockSpec((SEQ, hw), lambda b, h: (b, h)),
            pl.BlockSpec((CTX_LEN, hw), lambda b, h: (b, h)),
            pl.BlockSpec((SEQ, vw), lambda b, h: (b, h)),
            pl.BlockSpec((CTX_LEN, vw), lambda b, h: (b, h)),
        ],
        out_specs=pl.BlockSpec((SEQ, vw), lambda b, h: (b, h)),
        out_shape=jax.ShapeDtypeStruct((batch * SEQ, MLA_W), BF16),
        compiler_params=_cparams(("parallel", "parallel"), 48),
        name="mla_attn",
    )(q, k_lat, k_ctx, v_lat, v_ctx)


def _attn_ctx_kernel(q_ref, k_ref, v_ref, o_ref):
    s = _dot_nt(q_ref[...], k_ref[...])
    e = jnp.exp2(s - jnp.max(s, axis=-1, keepdims=True))
    o = _dot(e.astype(BF16), v_ref[...])
    o_ref[...] = (o / jnp.sum(e, axis=-1, keepdims=True)).astype(BF16)


def _attn_ctx(q, k, v, batch):
    hw = MLA_NOPE + LANES
    return pl.pallas_call(
        _attn_ctx_kernel,
        grid=(batch, MLA_HEADS),
        in_specs=[
            pl.BlockSpec((CTX_LEN, hw), lambda b, h: (b, h)),
            pl.BlockSpec((CTX_LEN, hw), lambda b, h: (b, h)),
            pl.BlockSpec((CTX_LEN, MLA_DV), lambda b, h: (b, h)),
        ],
        out_specs=pl.BlockSpec((CTX_LEN, MLA_DV), lambda b, h: (b, h)),
        out_shape=jax.ShapeDtypeStruct((batch * CTX_LEN, MLA_W), BF16),
        compiler_params=_cparams(("parallel", "parallel"), 32),
        name="mla_attn_ctx",
    )(q, k, v)


def _tables():
    rows = SEQ // GRID_W
    quarter = ROPE_DIM // 4
    inv = ROPE_BASE ** (-jnp.arange(quarter, dtype=F32) / quarter)
    r_idx = jnp.repeat(jnp.arange(rows, dtype=F32), GRID_W)
    c_idx = jnp.tile(jnp.arange(GRID_W, dtype=F32), rows)
    ang = jnp.concatenate([r_idx[:, None] * inv, c_idx[:, None] * inv], axis=-1)
    cos, sin = jnp.cos(ang), jnp.sin(ang)
    z = jnp.zeros_like(cos)
    tabs = {
        "ret_c": jnp.concatenate([cos, cos, cos, cos], axis=1),
        "ret_s1": jnp.concatenate([-sin, z, -sin, z], axis=1),
        "ret_s2": jnp.concatenate([z, sin, z, sin], axis=1),
        "mla_c": jnp.concatenate([cos, cos, z, z], axis=1),
        "mla_s1": jnp.concatenate([-sin, z, z, z], axis=1),
        "mla_s2": jnp.concatenate([z, sin, z, z], axis=1),
    }
    c_len = RET_CHUNK
    log_g = jnp.log1p(-jnp.exp2(-5.0 - jnp.arange(RET_HEADS, dtype=F32)))
    i = jnp.arange(c_len, dtype=F32)
    k_scale = RET_DK ** -0.5
    lg = log_g[:, None, None]
    dm = jnp.exp(lg * jnp.abs(i[:, None] - i[None, :])) * k_scale
    tabs["dm"] = dm.reshape(RET_HEADS // 2, 2, c_len, c_len)

    def lanes_by_head(per_head_rows):
        t = jnp.repeat(per_head_rows[:, :, None], RET_DK, axis=2)
        t = t.reshape(RET_HEADS // 2, 2, c_len, RET_DK)
        return jnp.concatenate([t[:, 0], t[:, 1]], axis=-1)

    tabs["kf"] = lanes_by_head(jnp.exp(log_g[:, None] * (c_len - 1.0 - i)) * k_scale)
    tabs["kb"] = lanes_by_head(jnp.exp(log_g[:, None] * i) * k_scale)
    tabs["qf"] = lanes_by_head(jnp.exp(log_g[:, None] * (i + 1.0)))
    tabs["qb"] = lanes_by_head(jnp.exp(log_g[:, None] * (c_len - i)))
    cdec = jnp.exp(log_g * c_len).reshape(RET_HEADS // 2, 2)
    cd = jnp.repeat(cdec[:, :, None], RET_DK, axis=2).reshape(RET_HEADS // 2, 2 * RET_DK, 1)
    tabs["cd"] = jnp.broadcast_to(cd, (RET_HEADS // 2, 2 * RET_DK, 2 * RET_DV))
    return tabs


def _cast_kernel(x_ref, o_ref):
    o_ref[...] = x_ref[...].astype(BF16)


def _cast_bf16(w, tr):
    depth, r, c = w.shape
    return pl.pallas_call(
        _cast_kernel,
        grid=(depth, r // tr),
        in_specs=[pl.BlockSpec((1, tr, c), lambda l, i: (l, i, 0))],
        out_specs=pl.BlockSpec((1, tr, c), lambda l, i: (l, i, 0)),
        out_shape=jax.ShapeDtypeStruct(w.shape, BF16),
        compiler_params=_cparams(("parallel", "parallel"), 48),
        name="cast",
    )(w)


_W_IN_SRC = ((2304, 1024), (3328, 512), (768, 768), (1536, 768), (0, 384), (384, 384), (3840, 256))
_W_IN_KR = (4096, 64)


def _prep_w_in_kernel(w_ref, kr_ref, o_ref):
    dst = 0
    for src, width in _W_IN_SRC:
        o_ref[0, :, dst:dst + width] = w_ref[0, :, src:src + width].astype(BF16)
        dst += width
    o_ref[0, :, dst:dst + LANES] = kr_ref[0]


def _prep_w_in(w):
    depth, d, n = w.shape
    kr = w[..., _W_IN_KR[0]:_W_IN_KR[0] + _W_IN_KR[1]].astype(BF16)
    kr = jnp.pad(kr, ((0, 0), (0, 0), (0, LANES - _W_IN_KR[1])))
    tr = 512
    return pl.pallas_call(
        _prep_w_in_kernel,
        grid=(depth, d // tr),
        in_specs=[
            pl.BlockSpec((1, tr, n), lambda l, i: (l, i, 0)),
            pl.BlockSpec((1, tr, LANES), lambda l, i: (l, i, 0)),
        ],
        out_specs=pl.BlockSpec((1, tr, P_W), lambda l, i: (l, i, 0)),
        out_shape=jax.ShapeDtypeStruct((depth, d, P_W), BF16),
        compiler_params=_cparams(("parallel", "parallel"), 48),
        name="prep_w_in",
    )(w, kr)


def _prep_w_uq(w):
    w = w.reshape(MLA_Q_RANK, MLA_HEADS, MLA_NOPE + MLA_ROPE)
    nope = w[:, :, :MLA_NOPE].reshape(MLA_Q_RANK, MLA_HEADS * MLA_NOPE)
    rope = jnp.pad(w[:, :, MLA_NOPE:], ((0, 0), (0, 0), (0, LANES - MLA_ROPE)))
    return jnp.concatenate([nope, rope.reshape(MLA_Q_RANK, MLA_HEADS * LANES)], axis=1).astype(BF16)


def _lat_mod_row(i, tm):
    return (i * tm) // SEQ


def _ctx_mod_row(i, tm):
    return CTX_MOD_ROW


def kernel(x, c, ctx, c_ctx, w_ada, b_ada, norm_g, w_ffa_in, w_ffa_out, w_ffb_in, w_ffb_out,
           w_in, w_out, ret_gn, cv_dw, cv_dw_b, cv_ln_g, cv_ln_b, cv_pw,
           mla_q_norm, w_uq, mla_kv_norm, w_ukv):
    batch, seq, d = x.shape
    assert (seq, d, ctx.shape[1]) == (SEQ, D_MODEL, CTX_LEN)
    xl = x.reshape(batch * seq, d)
    xc = ctx.reshape(batch * CTX_LEN, d)
    s_in = jnp.concatenate([c, c_ctx[None, :], jnp.zeros((ROWS_PAD - batch - 1, d), F32)], axis=0)
    mod_all = _ada(s_in, w_ada, b_ada).reshape(DEPTH, ROWS_PAD, N_MOD, d)
    tabs = _tables()
    mla_rope = (tabs["mla_c"], tabs["mla_s1"], tabs["mla_s2"])

    wa_in, wa_out, wb_in, wb_out = w_ffa_in, w_ffa_out, w_ffb_in, w_ffb_out
    wi = _prep_w_in(w_in)
    wo = _cast_bf16(w_out, 1024)

    ffn_lat = dict(mod_row=_lat_mod_row, tm=1024, tf=256, x_buffers=2)
    ffn_ctx = dict(mod_row=_ctx_mod_row, tm=1024, tf=256, x_buffers=2)
    for l in range(DEPTH):
        last = l == DEPTH - 1
        mod = mod_all[l]
        ng = norm_g[l]
        wq = _prep_w_uq(w_uq[l])
        wkv = w_ukv[l].astype(BF16)
        nq, nkv = mla_q_norm[l][None, :], mla_kv_norm[l][None, :]
        gn = ret_gn[l][None, :]
        conv_misc = jnp.stack([cv_dw_b[l], cv_ln_g[l], cv_ln_b[l]], axis=0)
        pw = cv_pw[l].astype(BF16)

        xl = _ffn(xl, mod, ng[0:2], wa_in, wa_out, l, 0, **ffn_lat)
        xc = _ffn(xc, mod, ng[0:2], wa_in, wa_out, l, 0, **ffn_ctx)

        p_lat = _inproj(xl, mod, ng[2:3], wi, l, _lat_mod_row, 1024, 1408)
        p_ctx = _inproj(xc, mod, ng[2:3], wi, l, _ctx_mod_row, 1024, 1408)

        y_ret_l, y_ret_c = _retention(p_lat, p_ctx, tabs, gn, batch)
        y_conv_l = _conv(p_lat, cv_dw[l], conv_misc, pw, batch, SEQ)
        q_l, k_l, v_l = _mla_proj(p_lat, nq, nkv, wq, wkv, mla_rope, 512)
        q_c, k_c, v_c = _mla_proj(p_ctx, nq, nkv, wq, wkv, None, 512)
        y_mla_l = _attn(q_l, k_l, k_c, v_l, v_c, batch, 2, 512)

        xl = _outproj(xl, y_ret_l, y_conv_l, y_mla_l, mod, ng[3:4], wo, l, _lat_mod_row, 1024)
        xl = _ffn(xl, mod, ng[4:6], wb_in, wb_out, l, 6, **ffn_lat)
        if not last:
            y_conv_c = _conv(p_ctx, cv_dw[l], conv_misc, pw, batch, CTX_LEN)
            y_mla_c = _attn_ctx(q_c, k_c, v_c, batch)
            xc = _outproj(xc, y_ret_c, y_conv_c, y_mla_c, mod, ng[3:4], wo, l, _ctx_mod_row, 1024)
            xc = _ffn(xc, mod, ng[4:6], wb_in, wb_out, l, 6, **ffn_ctx)
    return xl.reshape(batch, seq, d)
```

```python
import functools

import jax
import jax.numpy as jnp
from jax import lax
from jax.experimental import pallas as pl
from jax.experimental.pallas import tpu as pltpu

F32 = jnp.float32
BF16 = jnp.bfloat16

D_MODEL = 2048
SEQ = 2048
DEPTH = 2
GRID_W = 64
CTX_LEN = 256
D_FF = 5632
FFN_RES = 0.5
N_MOD = 9
ROPE_DIM = 64
ROPE_BASE = 10000.0
EPS = 1e-6
RET_HEADS = 6
RET_DK = 64
RET_DV = 128
RET_CHUNK = 128
CONV_CH = 512
CONV_W = 31
MLA_HEADS = 6
MLA_Q_RANK = 512
MLA_KV_RANK = 256
MLA_NOPE = 128
MLA_ROPE = 64
MLA_DV = 128
MLA_SCALE = (MLA_NOPE + MLA_ROPE) ** -0.5
LOG2_E = 1.4426950408889634
RET_W = RET_HEADS * RET_DV
MLA_W = MLA_HEADS * MLA_DV

LANES = 128
MIB = 1024 * 1024
ROW_CHUNK = 128
RET_UNROLL = 8
SUBLANES = 8

P_GLU, P_CQ, P_RV, P_RG, P_RQ, P_RK, P_CKV, P_KR = 0, 1024, 1536, 2304, 3072, 3456, 3840, 4096
P_W = 4224
CTX_MOD_ROW = 4
ROWS_PAD = 8


def _cparams(sem, vmem_mib):
    return pltpu.CompilerParams(dimension_semantics=sem, vmem_limit_bytes=int(vmem_mib * MIB))


def _sigmoid(x):
    return 1.0 / (1.0 + jnp.exp(-x))


def _rms(x, g):
    ms = jnp.mean(x * x, axis=-1, keepdims=True)
    return x * lax.rsqrt(ms + EPS) * g


def _dot(a, b):
    return jnp.dot(a, b, preferred_element_type=F32)


def _prenorm_modulate(x_ref, h_ref, gain_row, shift_row):
    def body(i, carry):
        rows = pl.ds(pl.multiple_of(i * ROW_CHUNK, ROW_CHUNK), ROW_CHUNK)
        x = x_ref[rows, :]
        r = lax.rsqrt(jnp.mean(x * x, axis=-1, keepdims=True) + EPS)
        h_ref[rows, :] = (x * r * gain_row + shift_row).astype(BF16)
        return carry

    lax.fori_loop(0, x_ref.shape[0] // ROW_CHUNK, body, 0)


def _postnorm_residual(y_ref, x_ref, o_ref, gain_row):
    def body(i, carry):
        rows = pl.ds(pl.multiple_of(i * ROW_CHUNK, ROW_CHUNK), ROW_CHUNK)
        y = y_ref[rows, :]
        r = lax.rsqrt(jnp.mean(y * y, axis=-1, keepdims=True) + EPS)
        o_ref[rows, :] = x_ref[rows, :] + y * r * gain_row
        return carry

    lax.fori_loop(0, y_ref.shape[0] // ROW_CHUNK, body, 0)


def _dot_nt(a, b):
    return lax.dot_general(a, b, (((1,), (1,)), ((), ())), preferred_element_type=F32)


def _dot_tn(a, b):
    return lax.dot_general(a, b, (((0,), (0,)), ((), ())), preferred_element_type=F32)


def _rope(x, c, s1, s2):
    return x * c + pltpu.roll(x, 96, 1) * s1 + pltpu.roll(x, 32, 1) * s2


def _ada_kernel(s_ref, w_ref, b_ref, o_ref):
    s = s_ref[...]
    s = s * _sigmoid(s)
    o_ref[0] = _dot(s.astype(BF16), w_ref[0].astype(BF16)) + b_ref[0]


def _ada(s_in, w_ada, b_ada):
    depth, d, n = w_ada.shape
    tn = 1024
    return pl.pallas_call(
        _ada_kernel,
        grid=(depth, n // tn),
        in_specs=[
            pl.BlockSpec((ROWS_PAD, d), lambda l, j: (0, 0)),
            pl.BlockSpec((1, d, tn), lambda l, j: (l, 0, j)),
            pl.BlockSpec((1, 1, tn), lambda l, j: (l, 0, j)),
        ],
        out_specs=pl.BlockSpec((1, ROWS_PAD, tn), lambda l, j: (l, 0, j)),
        out_shape=jax.ShapeDtypeStruct((depth, ROWS_PAD, n), F32),
        compiler_params=_cparams(("parallel", "parallel"), 40),
        name="ada",
    )(s_in, w_ada, b_ada.reshape(depth, 1, n))


def _ffn_kernel(x_ref, mod_ref, g_ref, wa_ref, wu_ref, wo_ref, o_ref, h_ref, *, k0, nj):
    j = pl.program_id(1)

    @pl.when(j == 0)
    def _():
        gain = g_ref[0:1, :] * (1.0 + mod_ref[0, k0 + 1:k0 + 2, :])
        _prenorm_modulate(x_ref, h_ref, gain, mod_ref[0, k0:k0 + 1, :])
        o_ref[...] = jnp.zeros_like(o_ref)

    h = h_ref[...]
    a = _dot(h, wa_ref[...].astype(BF16))
    u = _dot(h, wu_ref[...].astype(BF16))
    act = (a * _sigmoid(a) * u).astype(BF16)
    o_ref[...] += _dot(act, wo_ref[...].astype(BF16))

    @pl.when(j == nj - 1)
    def _():
        gain = g_ref[1:2, :] * (FFN_RES * mod_ref[0, k0 + 2:k0 + 3, :])
        _postnorm_residual(o_ref, x_ref, o_ref, gain)


def _ffn(x, mod, g2, w_in, w_out, layer, k0, mod_row, tm, tf, x_buffers):
    r, d = x.shape
    nj = D_FF // tf
    return pl.pallas_call(
        functools.partial(_ffn_kernel, k0=k0, nj=nj),
        grid=(r // tm, nj),
        in_specs=[
            pl.BlockSpec((tm, d), lambda i, j: (i, 0), pipeline_mode=pl.Buffered(x_buffers)),
            pl.BlockSpec((1, N_MOD, d), lambda i, j: (mod_row(i, tm), 0, 0)),
            pl.BlockSpec((2, d), lambda i, j: (0, 0)),
            pl.BlockSpec((None, d, tf), lambda i, j: (layer, 0, j)),
            pl.BlockSpec((None, d, tf), lambda i, j: (layer, 0, nj + j)),
            pl.BlockSpec((None, tf, d), lambda i, j: (layer, j, 0)),
        ],
        out_specs=pl.BlockSpec((tm, d), lambda i, j: (i, 0)),
        out_shape=jax.ShapeDtypeStruct((r, d), F32),
        scratch_shapes=[pltpu.VMEM((tm, d), BF16)],
        compiler_params=_cparams(("parallel", "arbitrary"), 60),
        name="ffn",
    )(x, mod, g2, w_in, w_in, w_out)


def _inproj_kernel(x_ref, mod_ref, g_ref, w_ref, o_ref, h_ref):
    @pl.when(pl.program_id(1) == 0)
    def _():
        gain = g_ref[...] * (1.0 + mod_ref[0, 4:5, :])
        _prenorm_modulate(x_ref, h_ref, gain, mod_ref[0, 3:4, :])

    o_ref[...] = _dot(h_ref[...], w_ref[...])


def _inproj(x, mod, g, w, layer, mod_row, tm, tn):
    r, d = x.shape
    return pl.pallas_call(
        _inproj_kernel,
        grid=(r // tm, P_W // tn),
        in_specs=[
            pl.BlockSpec((tm, d), lambda i, j: (i, 0)),
            pl.BlockSpec((1, N_MOD, d), lambda i, j: (mod_row(i, tm), 0, 0)),
            pl.BlockSpec((1, d), lambda i, j: (0, 0)),
            pl.BlockSpec((None, d, tn), lambda i, j: (layer, 0, j)),
        ],
        out_specs=pl.BlockSpec((tm, tn), lambda i, j: (i, j)),
        out_shape=jax.ShapeDtypeStruct((r, P_W), F32),
        scratch_shapes=[pltpu.VMEM((tm, d), BF16)],
        compiler_params=_cparams(("parallel", "arbitrary"), 56),
        name="inproj",
    )(x, mod, g, w)


def _outproj_kernel(x_ref, yr_ref, yc_ref, ym_ref, mod_ref, g_ref, w_ref, o_ref):
    y = _dot(yr_ref[...], w_ref[0:RET_W, :])
    y += _dot(yc_ref[...], w_ref[RET_W:RET_W + CONV_CH, :])
    y += _dot(ym_ref[...], w_ref[RET_W + CONV_CH:, :])
    o_ref[...] = y
    _postnorm_residual(o_ref, x_ref, o_ref, g_ref[...] * mod_ref[0, 5:6, :])


def _outproj(x, y_ret, y_conv, y_mla, mod, g, w, layer, mod_row, tm):
    r, d = x.shape
    return pl.pallas_call(
        _outproj_kernel,
        grid=(r // tm,),
        in_specs=[
            pl.BlockSpec((tm, d), lambda i: (i, 0)),
            pl.BlockSpec((tm, RET_W), lambda i: (i, 0)),
            pl.BlockSpec((tm, CONV_CH), lambda i: (i, 0)),
            pl.BlockSpec((tm, MLA_W), lambda i: (i, 0)),
            pl.BlockSpec((1, N_MOD, d), lambda i: (mod_row(i, tm), 0, 0)),
            pl.BlockSpec((1, d), lambda i: (0, 0)),
            pl.BlockSpec((None, d, d), lambda i: (layer, 0, 0), pipeline_mode=pl.Buffered(1)),
        ],
        out_specs=pl.BlockSpec((tm, d), lambda i: (i, 0)),
        out_shape=jax.ShapeDtypeStruct((r, d), F32),
        compiler_params=_cparams(("parallel",), 60),
        name="outproj",
    )(x, y_ret, y_conv, y_mla, mod, g, w)


def _ret_kernel(ql_ref, kl_ref, vl_ref, gl_ref, qc_ref, kc_ref, vc_ref, gc_ref,
                rc_ref, rs1_ref, rs2_ref, dm_ref, kf_ref, kb_ref, qf_ref, qb_ref, cd_ref, gn_ref,
                yl_ref, yc_ref, qr_ref, kr_ref, sf_ref, *, n_lat, n_ctx):
    c_len = RET_CHUNK
    lane = lax.broadcasted_iota(jnp.int32, (1, LANES), 1)
    head_masks = (lane < RET_DK, lane >= RET_DK)
    row = lax.broadcasted_iota(jnp.int32, (LANES, 2 * RET_DV), 0)
    col = lax.broadcasted_iota(jnp.int32, (LANES, 2 * RET_DV), 1)
    block_diag = (row < RET_DK) == (col < RET_DV)
    kf, kb, qf, qb, cd = kf_ref[0], kb_ref[0], qf_ref[0], qb_ref[0], cd_ref[0]
    gn = gn_ref[...]
    zero_state = jnp.zeros((LANES, 2 * RET_DV), F32)

    def kv_state(k, v16, dec):
        return jnp.where(block_diag, _dot_tn((k * dec).astype(BF16), v16), 0.0)

    def chunk_out(q, k, v16, g, sf16, sb16):
        k16 = k.astype(BF16)
        parts = []
        for e in range(2):
            q16 = jnp.where(head_masks[e], q, 0.0).astype(BF16)
            a = _dot_nt(q16, k16) * dm_ref[0, e]
            parts.append(_dot(a.astype(BF16), v16[:, e * RET_DV:(e + 1) * RET_DV]))
        o = jnp.concatenate(parts, axis=1)
        o = o + _dot((q * qf).astype(BF16), sf16) + _dot((q * qb).astype(BF16), sb16)
        normed = []
        for e in range(2):
            oe = o[:, e * RET_DV:(e + 1) * RET_DV]
            dlt = oe - jnp.mean(oe, axis=-1, keepdims=True)
            var = jnp.mean(dlt * dlt, axis=-1, keepdims=True)
            normed.append(dlt * lax.rsqrt(var + EPS))
        on = jnp.concatenate(normed, axis=1) * gn
        return ((g * _sigmoid(g)) * on).astype(BF16)

    cq = [qc_ref[c * c_len:(c + 1) * c_len, :] for c in range(n_ctx)]
    ck = [kc_ref[c * c_len:(c + 1) * c_len, :] for c in range(n_ctx)]
    cv = [vc_ref[c * c_len:(c + 1) * c_len, :].astype(BF16) for c in range(n_ctx)]
    kvf = [kv_state(ck[c], cv[c], kf) for c in range(n_ctx)]
    kvb = [kv_state(ck[c], cv[c], kb) for c in range(n_ctx)]
    sf_list = [zero_state]
    for c in range(n_ctx):
        sf_list.append(cd * sf_list[c] + kvf[c])
    sb_list = [zero_state] * (n_ctx + 1)
    for c in range(n_ctx - 1, -1, -1):
        sb_list[c] = cd * sb_list[c + 1] + kvb[c]
    for c in range(n_ctx):
        yc_ref[c * c_len:(c + 1) * c_len, :] = chunk_out(
            cq[c], ck[c], cv[c], gc_ref[c * c_len:(c + 1) * c_len, :],
            sf_list[c].astype(BF16), sb_list[c + 1].astype(BF16))
    s0f, s0b = sf_list[n_ctx], sb_list[0]

    def fwd(c, sf):
        r0 = pl.multiple_of(c * c_len, c_len)
        rows = pl.ds(r0, c_len)
        rc, rs1, rs2 = rc_ref[rows, :], rs1_ref[rows, :], rs2_ref[rows, :]
        q = _rope(ql_ref[rows, :], rc, rs1, rs2)
        k = _rope(kl_ref[rows, :], rc, rs1, rs2)
        qr_ref[rows, :] = q
        kr_ref[rows, :] = k
        sf_ref[c] = sf.astype(BF16)
        return cd * sf + kv_state(k, vl_ref[rows, :].astype(BF16), kf)

    lax.fori_loop(0, n_lat, fwd, s0f, unroll=RET_UNROLL)

    def bwd(i, sb):
        c = n_lat - 1 - i
        r0 = pl.multiple_of(c * c_len, c_len)
        rows = pl.ds(r0, c_len)
        q, k = qr_ref[rows, :], kr_ref[rows, :]
        v16 = vl_ref[rows, :].astype(BF16)
        yl_ref[rows, :] = chunk_out(q, k, v16, gl_ref[rows, :], sf_ref[c], sb.astype(BF16))
        return cd * sb + kv_state(k, v16, kb)

    lax.fori_loop(0, n_lat, bwd, s0b, unroll=RET_UNROLL)


def _retention(p_lat, p_ctx, tabs, gn, batch):
    t, lc = SEQ, CTX_LEN
    n_pairs = RET_HEADS // 2
    n_lat, n_ctx = t // RET_CHUNK, lc // RET_CHUNK
    qw, vw = 2 * RET_DK, 2 * RET_DV

    def pspec(rows, width, col0):
        return pl.BlockSpec((rows, width), lambda b, hp: (b, col0 // width + hp))

    tab3 = pl.BlockSpec((1, LANES, LANES), lambda b, hp: (hp, 0, 0))
    rope_spec = pl.BlockSpec((t, LANES), lambda b, hp: (0, 0))
    return pl.pallas_call(
        functools.partial(_ret_kernel, n_lat=n_lat, n_ctx=n_ctx),
        grid=(batch, n_pairs),
        in_specs=[
            pspec(t, qw, P_RQ), pspec(t, qw, P_RK), pspec(t, vw, P_RV), pspec(t, vw, P_RG),
            pspec(lc, qw, P_RQ), pspec(lc, qw, P_RK), pspec(lc, vw, P_RV), pspec(lc, vw, P_RG),
            rope_spec, rope_spec, rope_spec,
            pl.BlockSpec((1, 2, LANES, LANES), lambda b, hp: (hp, 0, 0, 0)),
            tab3, tab3, tab3, tab3,
            pl.BlockSpec((1, LANES, vw), lambda b, hp: (hp, 0, 0)),
            pl.BlockSpec((1, vw), lambda b, hp: (0, hp)),
        ],
        out_specs=[
            pl.BlockSpec((t, vw), lambda b, hp: (b, hp)),
            pl.BlockSpec((lc, vw), lambda b, hp: (b, hp)),
        ],
        out_shape=[
            jax.ShapeDtypeStruct((batch * t, RET_W), BF16),
            jax.ShapeDtypeStruct((batch * lc, RET_W), BF16),
        ],
        scratch_shapes=[
            pltpu.VMEM((t, LANES), F32),
            pltpu.VMEM((t, LANES), F32),
            pltpu.VMEM((n_lat, LANES, vw), BF16),
        ],
        compiler_params=_cparams(("parallel", "parallel"), 48),
        name="retention",
    )(p_lat, p_lat, p_lat, p_lat, p_ctx, p_ctx, p_ctx, p_ctx,
      tabs["ret_c"], tabs["ret_s1"], tabs["ret_s2"],
      tabs["dm"], tabs["kf"], tabs["kb"], tabs["qf"], tabs["qb"], tabs["cd"], gn)


def _conv_kernel(glu_ref, dw_ref, misc_ref, pw_ref, o_ref, pad_ref, *, t, rt):
    half = CONV_W // 2
    lo = 16
    zeros = jnp.zeros((lo, CONV_CH), F32)
    pad_ref[0:lo, :] = zeros
    pad_ref[lo + t:lo + t + lo, :] = zeros

    def glu(r, carry):
        rows = pl.ds(pl.multiple_of(r * rt, rt), rt)
        a = glu_ref[rows, 0:CONV_CH]
        b = glu_ref[rows, CONV_CH:2 * CONV_CH]
        pad_ref[pl.ds(pl.multiple_of(r * rt + lo, 8), rt), :] = a * _sigmoid(b)
        return carry

    lax.fori_loop(0, t // rt, glu, 0)

    def conv(r, carry):
        base = pl.multiple_of(r * rt, rt)
        win_rows = rt + 2 * lo
        cg_w = LANES
        accs = []
        for cg in range(CONV_CH // cg_w):
            cols = slice(cg * cg_w, (cg + 1) * cg_w)
            win = pad_ref[pl.ds(base, win_rows), cols]
            acc = jnp.zeros((rt, cg_w), F32) + misc_ref[0:1, cols]
            for phase in range(SUBLANES):
                shifted = win if phase == 0 else pltpu.roll(win, win_rows - phase, 0)
                for off in range(phase, lo + half + 1, SUBLANES):
                    k = off - (lo - half)
                    if 0 <= k < CONV_W:
                        a0 = off - phase
                        acc = acc + shifted[a0:a0 + rt, :] * dw_ref[k:k + 1, cols]
            accs.append(acc)
        acc = jnp.concatenate(accs, axis=1)
        dlt = acc - jnp.mean(acc, axis=-1, keepdims=True)
        var = jnp.mean(dlt * dlt, axis=-1, keepdims=True)
        y = dlt * lax.rsqrt(var + EPS) * misc_ref[1:2, :] + misc_ref[2:3, :]
        y = y * _sigmoid(y)
        o_ref[pl.ds(base, rt), :] = _dot(y.astype(BF16), pw_ref[...]).astype(BF16)
        return carry

    lax.fori_loop(0, t // rt, conv, 0, unroll=2)


def _conv(p, dw, misc, pw, batch, t):
    rt = 64
    return pl.pallas_call(
        functools.partial(_conv_kernel, t=t, rt=rt),
        grid=(batch,),
        in_specs=[
            pl.BlockSpec((t, 2 * CONV_CH), lambda b: (b, P_GLU // (2 * CONV_CH))),
            pl.BlockSpec((CONV_W, CONV_CH), lambda b: (0, 0)),
            pl.BlockSpec((3, CONV_CH), lambda b: (0, 0)),
            pl.BlockSpec((CONV_CH, CONV_CH), lambda b: (0, 0)),
        ],
        out_specs=pl.BlockSpec((t, CONV_CH), lambda b: (b, 0)),
        out_shape=jax.ShapeDtypeStruct((batch * t, CONV_CH), BF16),
        scratch_shapes=[pltpu.VMEM((t + 32, CONV_CH), F32)],
        compiler_params=_cparams(("parallel",), 40),
        name="conv",
    )(p, dw, misc, pw)


def _mla_proj_kernel(cq_ref, ckv_ref, kr_ref, nq_ref, nkv_ref, wq_ref, wkv_ref, *rest, rope):
    if rope:
        rc_ref, rs1_ref, rs2_ref, q_ref, k_ref, v_ref = rest
        rot = lambda z: _rope(z, rc_ref[...], rs1_ref[...], rs2_ref[...])
    else:
        q_ref, k_ref, v_ref = rest
        rot = lambda z: z
    qq = _dot(_rms(cq_ref[...], nq_ref[...]).astype(BF16), wq_ref[...]) * (MLA_SCALE * LOG2_E)
    kv = _dot(_rms(ckv_ref[...], nkv_ref[...]).astype(BF16), wkv_ref[...])
    kr = rot(kr_ref[...]).astype(BF16)
    hw = MLA_NOPE + LANES
    for h in range(MLA_HEADS):
        q_ref[:, h * hw:h * hw + MLA_NOPE] = qq[:, h * MLA_NOPE:(h + 1) * MLA_NOPE].astype(BF16)
        qr = qq[:, MLA_HEADS * MLA_NOPE + h * LANES:MLA_HEADS * MLA_NOPE + (h + 1) * LANES]
        q_ref[:, h * hw + MLA_NOPE:(h + 1) * hw] = rot(qr).astype(BF16)
        k_ref[:, h * hw:h * hw + MLA_NOPE] = kv[:, 2 * h * LANES:(2 * h + 1) * LANES].astype(BF16)
        k_ref[:, h * hw + MLA_NOPE:(h + 1) * hw] = kr
        v_ref[:, h * MLA_DV:(h + 1) * MLA_DV] = kv[:, (2 * h + 1) * LANES:(2 * h + 2) * LANES].astype(BF16)


def _mla_proj(p, nq, nkv, wq, wkv, rope_tabs, tm):
    r = p.shape[0]
    hw = MLA_NOPE + LANES
    rope = rope_tabs is not None
    in_specs = [
        pl.BlockSpec((tm, MLA_Q_RANK), lambda i: (i, P_CQ // MLA_Q_RANK)),
        pl.BlockSpec((tm, MLA_KV_RANK), lambda i: (i, P_CKV // MLA_KV_RANK)),
        pl.BlockSpec((tm, LANES), lambda i: (i, P_KR // LANES)),
        pl.BlockSpec((1, MLA_Q_RANK), lambda i: (0, 0)),
        pl.BlockSpec((1, MLA_KV_RANK), lambda i: (0, 0)),
        pl.BlockSpec(wq.shape, lambda i: (0, 0)),
        pl.BlockSpec(wkv.shape, lambda i: (0, 0)),
    ]
    args = [p, p, p, nq, nkv, wq, wkv]
    if rope:
        n_pos = SEQ // tm
        in_specs += [pl.BlockSpec((tm, LANES), lambda i: (i % n_pos, 0))] * 3
        args += list(rope_tabs)
    return pl.pallas_call(
        functools.partial(_mla_proj_kernel, rope=rope),
        grid=(r // tm,),
        in_specs=in_specs,
        out_specs=[
            pl.BlockSpec((tm, MLA_HEADS * hw), lambda i: (i, 0)),
            pl.BlockSpec((tm, MLA_HEADS * hw), lambda i: (i, 0)),
            pl.BlockSpec((tm, MLA_W), lambda i: (i, 0)),
        ],
        out_shape=[
            jax.ShapeDtypeStruct((r, MLA_HEADS * hw), BF16),
            jax.ShapeDtypeStruct((r, MLA_HEADS * hw), BF16),
            jax.ShapeDtypeStruct((r, MLA_W), BF16),
        ],
        compiler_params=_cparams(("parallel",), 40),
        name="mla_proj",
    )(*args)


def _attn_kernel(q_ref, kl_ref, kc_ref, vl_ref, vc_ref, o_ref, *, n_heads, sub):
    hw = MLA_NOPE + LANES
    for e in range(n_heads):
        kcols = slice(e * hw, (e + 1) * hw)
        vcols = slice(e * MLA_DV, (e + 1) * MLA_DV)
        for i in range(q_ref.shape[0] // sub):
            rows = slice(i * sub, (i + 1) * sub)
            q = q_ref[rows, kcols]
            sl = _dot_nt(q, kl_ref[:, kcols])
            sc = _dot_nt(q, kc_ref[:, kcols])
            m = jnp.maximum(jnp.max(sl, axis=-1, keepdims=True), jnp.max(sc, axis=-1, keepdims=True))
            el = jnp.exp2(sl - m)
            ec = jnp.exp2(sc - m)
            den = jnp.sum(el, axis=-1, keepdims=True) + jnp.sum(ec, axis=-1, keepdims=True)
            o = _dot(el.astype(BF16), vl_ref[:, vcols]) + _dot(ec.astype(BF16), vc_ref[:, vcols])
            o_ref[rows, vcols] = (o / den).astype(BF16)


def _attn(q, k_lat, k_ctx, v_lat, v_ctx, batch, n_heads, sub):
    hw = n_heads * (MLA_NOPE + LANES)
    vw = n_heads * MLA_DV
    return pl.pallas_call(
        functools.partial(_attn_kernel, n_heads=n_heads, sub=sub),
        grid=(batch, MLA_HEADS // n_heads),
        in_specs=[
            pl.BlockSpec((SEQ, hw), lambda b, h: (b, h)),
            pl.BlockSpec((SEQ, hw), lambda b, h: (b, h)),
            pl.BlockSpec((CTX_LEN, hw), lambda b, h: (b, h)),
            pl.BlockSpec((SEQ, vw), lambda b, h: (b, h)),
            pl.BlockSpec((CTX_LEN, vw), lambda b, h: (b, h)),
        ],
        out_specs=pl.BlockSpec((SEQ, vw), lambda b, h: (b, h)),
        out_shape=jax.ShapeDtypeStruct((batch * SEQ, MLA_W), BF16),
        compiler_params=_cparams(("parallel", "parallel"), 48),
        name="mla_attn",
    )(q, k_lat, k_ctx, v_lat, v_ctx)


def _attn_ctx_kernel(q_ref, k_ref, v_ref, o_ref):
    s = _dot_nt(q_ref[...], k_ref[...])
    e = jnp.exp2(s - jnp.max(s, axis=-1, keepdims=True))
    o = _dot(e.astype(BF16), v_ref[...])
    o_ref[...] = (o / jnp.sum(e, axis=-1, keepdims=True)).astype(BF16)


def _attn_ctx(q, k, v, batch):
    hw = MLA_NOPE + LANES
    return pl.pallas_call(
        _attn_ctx_kernel,
        grid=(batch, MLA_HEADS),
        in_specs=[
            pl.BlockSpec((CTX_LEN, hw), lambda b, h: (b, h)),
            pl.BlockSpec((CTX_LEN, hw), lambda b, h: (b, h)),
            pl.BlockSpec((CTX_LEN, MLA_DV), lambda b, h: (b, h)),
        ],
        out_specs=pl.BlockSpec((CTX_LEN, MLA_DV), lambda b, h: (b, h)),
        out_shape=jax.ShapeDtypeStruct((batch * CTX_LEN, MLA_W), BF16),
        compiler_params=_cparams(("parallel", "parallel"), 32),
        name="mla_attn_ctx",
    )(q, k, v)


def _tables():
    rows = SEQ // GRID_W
    quarter = ROPE_DIM // 4
    inv = ROPE_BASE ** (-jnp.arange(quarter, dtype=F32) / quarter)
    r_idx = jnp.repeat(jnp.arange(rows, dtype=F32), GRID_W)
    c_idx = jnp.tile(jnp.arange(GRID_W, dtype=F32), rows)
    ang = jnp.concatenate([r_idx[:, None] * inv, c_idx[:, None] * inv], axis=-1)
    cos, sin = jnp.cos(ang), jnp.sin(ang)
    z = jnp.zeros_like(cos)
    tabs = {
        "ret_c": jnp.concatenate([cos, cos, cos, cos], axis=1),
        "ret_s1": jnp.concatenate([-sin, z, -sin, z], axis=1),
        "ret_s2": jnp.concatenate([z, sin, z, sin], axis=1),
        "mla_c": jnp.concatenate([cos, cos, z, z], axis=1),
        "mla_s1": jnp.concatenate([-sin, z, z, z], axis=1),
        "mla_s2": jnp.concatenate([z, sin, z, z], axis=1),
    }
    c_len = RET_CHUNK
    log_g = jnp.log1p(-jnp.exp2(-5.0 - jnp.arange(RET_HEADS, dtype=F32)))
    i = jnp.arange(c_len, dtype=F32)
    k_scale = RET_DK ** -0.5
    lg = log_g[:, None, None]
    dm = jnp.exp(lg * jnp.abs(i[:, None] - i[None, :])) * k_scale
    tabs["dm"] = dm.reshape(RET_HEADS // 2, 2, c_len, c_len)

    def lanes_by_head(per_head_rows):
        t = jnp.repeat(per_head_rows[:, :, None], RET_DK, axis=2)
        t = t.reshape(RET_HEADS // 2, 2, c_len, RET_DK)
        return jnp.concatenate([t[:, 0], t[:, 1]], axis=-1)

    tabs["kf"] = lanes_by_head(jnp.exp(log_g[:, None] * (c_len - 1.0 - i)) * k_scale)
    tabs["kb"] = lanes_by_head(jnp.exp(log_g[:, None] * i) * k_scale)
    tabs["qf"] = lanes_by_head(jnp.exp(log_g[:, None] * (i + 1.0)))
    tabs["qb"] = lanes_by_head(jnp.exp(log_g[:, None] * (c_len - i)))
    cdec = jnp.exp(log_g * c_len).reshape(RET_HEADS // 2, 2)
    cd = jnp.repeat(cdec[:, :, None], RET_DK, axis=2).reshape(RET_HEADS // 2, 2 * RET_DK, 1)
    tabs["cd"] = jnp.broadcast_to(cd, (RET_HEADS // 2, 2 * RET_DK, 2 * RET_DV))
    return tabs


def _cast_kernel(x_ref, o_ref):
    o_ref[...] = x_ref[...].astype(BF16)


def _cast_bf16(w, tr):
    depth, r, c = w.shape
    return pl.pallas_call(
        _cast_kernel,
        grid=(depth, r // tr),
        in_specs=[pl.BlockSpec((1, tr, c), lambda l, i: (l, i, 0))],
        out_specs=pl.BlockSpec((1, tr, c), lambda l, i: (l, i, 0)),
        out_shape=jax.ShapeDtypeStruct(w.shape, BF16),
        compiler_params=_cparams(("parallel", "parallel"), 48),
        name="cast",
    )(w)


_W_IN_SRC = ((2304, 1024), (3328, 512), (768, 768), (1536, 768), (0, 384), (384, 384), (3840, 256))
_W_IN_KR = (4096, 64)


def _prep_w_in_kernel(w_ref, kr_ref, o_ref):
    dst = 0
    for src, width in _W_IN_SRC:
        o_ref[0, :, dst:dst + width] = w_ref[0, :, src:src + width].astype(BF16)
        dst += width
    o_ref[0, :, dst:dst + LANES] = kr_ref[0]


def _prep_w_in(w):
    depth, d, n = w.shape
    kr = w[..., _W_IN_KR[0]:_W_IN_KR[0] + _W_IN_KR[1]].astype(BF16)
    kr = jnp.pad(kr, ((0, 0), (0, 0), (0, LANES - _W_IN_KR[1])))
    tr = 512
    return pl.pallas_call(
        _prep_w_in_kernel,
        grid=(depth, d // tr),
        in_specs=[
            pl.BlockSpec((1, tr, n), lambda l, i: (l, i, 0)),
            pl.BlockSpec((1, tr, LANES), lambda l, i: (l, i, 0)),
        ],
        out_specs=pl.BlockSpec((1, tr, P_W), lambda l, i: (l, i, 0)),
        out_shape=jax.ShapeDtypeStruct((depth, d, P_W), BF16),
        compiler_params=_cparams(("parallel", "parallel"), 48),
        name="prep_w_in",
    )(w, kr)


def _prep_w_uq(w):
    w = w.reshape(MLA_Q_RANK, MLA_HEADS, MLA_NOPE + MLA_ROPE)
    nope = w[:, :, :MLA_NOPE].reshape(MLA_Q_RANK, MLA_HEADS * MLA_NOPE)
    rope = jnp.pad(w[:, :, MLA_NOPE:], ((0, 0), (0, 0), (0, LANES - MLA_ROPE)))
    return jnp.concatenate([nope, rope.reshape(MLA_Q_RANK, MLA_HEADS * LANES)], axis=1).astype(BF16)


def _lat_mod_row(i, tm):
    return (i * tm) // SEQ


def _ctx_mod_row(i, tm):
    return CTX_MOD_ROW


def kernel(x, c, ctx, c_ctx, w_ada, b_ada, norm_g, w_ffa_in, w_ffa_out, w_ffb_in, w_ffb_out,
           w_in, w_out, ret_gn, cv_dw, cv_dw_b, cv_ln_g, cv_ln_b, cv_pw,
           mla_q_norm, w_uq, mla_kv_norm, w_ukv):
    batch, seq, d = x.shape
    assert (seq, d, ctx.shape[1]) == (SEQ, D_MODEL, CTX_LEN)
    xl = x.reshape(batch * seq, d)
    xc = ctx.reshape(batch * CTX_LEN, d)
    s_in = jnp.concatenate([c, c_ctx[None, :], jnp.zeros((ROWS_PAD - batch - 1, d), F32)], axis=0)
    mod_all = _ada(s_in, w_ada, b_ada).reshape(DEPTH, ROWS_PAD, N_MOD, d)
    tabs = _tables()
    mla_rope = (tabs["mla_c"], tabs["mla_s1"], tabs["mla_s2"])

    wa_in, wa_out, wb_in, wb_out = w_ffa_in, w_ffa_out, w_ffb_in, w_ffb_out
    wi = _prep_w_in(w_in)
    wo = _cast_bf16(w_out, 1024)

    ffn_lat = dict(mod_row=_lat_mod_row, tm=1024, tf=256, x_buffers=2)
    ffn_ctx = dict(mod_row=_ctx_mod_row, tm=1024, tf=256, x_buffers=2)
    for l in range(DEPTH):
        last = l == DEPTH - 1
        mod = mod_all[l]
        ng = norm_g[l]
        wq = _prep_w_uq(w_uq[l])
        wkv = w_ukv[l].astype(BF16)
        nq, nkv = mla_q_norm[l][None, :], mla_kv_norm[l][None, :]
        gn = ret_gn[l][None, :]
        conv_misc = jnp.stack([cv_dw_b[l], cv_ln_g[l], cv_ln_b[l]], axis=0)
        pw = cv_pw[l].astype(BF16)

        xl = _ffn(xl, mod, ng[0:2], wa_in, wa_out, l, 0, **ffn_lat)
        xc = _ffn(xc, mod, ng[0:2], wa_in, wa_out, l, 0, **ffn_ctx)

        p_lat = _inproj(xl, mod, ng[2:3], wi, l, _lat_mod_row, 1024, 1408)
        p_ctx = _inproj(xc, mod, ng[2:3], wi, l, _ctx_mod_row, 1024, 1408)

        y_ret_l, y_ret_c = _retention(p_lat, p_ctx, tabs, gn, batch)
        y_conv_l = _conv(p_lat, cv_dw[l], conv_misc, pw, batch, SEQ)
        q_l, k_l, v_l = _mla_proj(p_lat, nq, nkv, wq, wkv, mla_rope, 512)
        q_c, k_c, v_c = _mla_proj(p_ctx, nq, nkv, wq, wkv, None, 512)
        y_mla_l = _attn(q_l, k_l, k_c, v_l, v_c, batch, 2, 512)

        xl = _outproj(xl, y_ret_l, y_conv_l, y_mla_l, mod, ng[3:4], wo, l, _lat_mod_row, 1024)
        xl = _ffn(xl, mod, ng[4:6], wb_in, wb_out, l, 6, **ffn_lat)
        if not last:
            y_conv_c = _conv(p_ctx, cv_dw[l], conv_misc, pw, batch, CTX_LEN)
            y_mla_c = _attn_ctx(q_c, k_c, v_c, batch)
            xc = _outproj(xc, y_ret_c, y_conv_c, y_mla_c, mod, ng[3:4], wo, l, _ctx_mod_row, 1024)
            xc = _ffn(xc, mod, ng[4:6], wb_in, wb_out, l, 6, **ffn_ctx)
    return xl.reshape(batch, seq, d)
```

```python
import functools

import jax
import jax.numpy as jnp
from jax import lax
from jax.experimental import pallas as pl
from jax.experimental.pallas import tpu as pltpu

F32 = jnp.float32
BF16 = jnp.bfloat16

D_MODEL = 2048
SEQ = 2048
DEPTH = 2
GRID_W = 64
CTX_LEN = 256
D_FF = 5632
FFN_RES = 0.5
N_MOD = 9
ROPE_DIM = 64
ROPE_BASE = 10000.0
EPS = 1e-6
RET_HEADS = 6
RET_DK = 64
RET_DV = 128
RET_CHUNK = 128
CONV_CH = 512
CONV_W = 31
MLA_HEADS = 6
MLA_Q_RANK = 512
MLA_KV_RANK = 256
MLA_NOPE = 128
MLA_ROPE = 64
MLA_DV = 128
MLA_SCALE = (MLA_NOPE + MLA_ROPE) ** -0.5
LOG2_E = 1.4426950408889634
RET_W = RET_HEADS * RET_DV
MLA_W = MLA_HEADS * MLA_DV

LANES = 128
MIB = 1024 * 1024
ROW_CHUNK = 128
RET_UNROLL = 8
SUBLANES = 8

P_GLU, P_CQ, P_RV, P_RG, P_RQ, P_RK, P_CKV, P_KR = 0, 1024, 1536, 2304, 3072, 3456, 3840, 4096
P_W = 4224
CTX_MOD_ROW = 4
ROWS_PAD = 8


def _cparams(sem, vmem_mib):
    return pltpu.CompilerParams(dimension_semantics=sem, vmem_limit_bytes=int(vmem_mib * MIB))


def _sigmoid(x):
    return 1.0 / (1.0 + jnp.exp(-x))


def _rms(x, g):
    ms = jnp.mean(x * x, axis=-1, keepdims=True)
    return x * lax.rsqrt(ms + EPS) * g


def _dot(a, b):
    return jnp.dot(a, b, preferred_element_type=F32)


def _prenorm_modulate(x_ref, h_ref, gain_row, shift_row):
    def body(i, carry):
        rows = pl.ds(pl.multiple_of(i * ROW_CHUNK, ROW_CHUNK), ROW_CHUNK)
        x = x_ref[rows, :]
        r = lax.rsqrt(jnp.mean(x * x, axis=-1, keepdims=True) + EPS)
        h_ref[rows, :] = (x * r * gain_row + shift_row).astype(BF16)
        return carry

    lax.fori_loop(0, x_ref.shape[0] // ROW_CHUNK, body, 0)


def _postnorm_residual(y_ref, x_ref, o_ref, gain_row):
    def body(i, carry):
        rows = pl.ds(pl.multiple_of(i * ROW_CHUNK, ROW_CHUNK), ROW_CHUNK)
        y = y_ref[rows, :]
        r = lax.rsqrt(jnp.mean(y * y, axis=-1, keepdims=True) + EPS)
        o_ref[rows, :] = x_ref[rows, :] + y * r * gain_row
        return carry

    lax.fori_loop(0, y_ref.shape[0] // ROW_CHUNK, body, 0)


def _dot_nt(a, b):
    return lax.dot_general(a, b, (((1,), (1,)), ((), ())), preferred_element_type=F32)


def _dot_tn(a, b):
    return lax.dot_general(a, b, (((0,), (0,)), ((), ())), preferred_element_type=F32)


def _rope(x, c, s1, s2):
    return x * c + pltpu.roll(x, 96, 1) * s1 + pltpu.roll(x, 32, 1) * s2


def _ada_kernel(s_ref, w_ref, b_ref, o_ref):
    s = s_ref[...]
    s = s * _sigmoid(s)
    o_ref[0] = _dot(s.astype(BF16), w_ref[0].astype(BF16)) + b_ref[0]


def _ada(s_in, w_ada, b_ada):
    depth, d, n = w_ada.shape
    tn = 1024
    return pl.pallas_call(
        _ada_kernel,
        grid=(depth, n // tn),
        in_specs=[
            pl.BlockSpec((ROWS_PAD, d), lambda l, j: (0, 0)),
            pl.BlockSpec((1, d, tn), lambda l, j: (l, 0, j)),
            pl.BlockSpec((1, 1, tn), lambda l, j: (l, 0, j)),
        ],
        out_specs=pl.BlockSpec((1, ROWS_PAD, tn), lambda l, j: (l, 0, j)),
        out_shape=jax.ShapeDtypeStruct((depth, ROWS_PAD, n), F32),
        compiler_params=_cparams(("parallel", "parallel"), 40),
        name="ada",
    )(s_in, w_ada, b_ada.reshape(depth, 1, n))


def _ffn_kernel(x_ref, mod_ref, g_ref, wa_ref, wu_ref, wo_ref, o_ref, h_ref, *, k0, nj):
    j = pl.program_id(1)

    @pl.when(j == 0)
    def _():
        gain = g_ref[0:1, :] * (1.0 + mod_ref[0, k0 + 1:k0 + 2, :])
        _prenorm_modulate(x_ref, h_ref, gain, mod_ref[0, k0:k0 + 1, :])
        o_ref[...] = jnp.zeros_like(o_ref)

    h = h_ref[...]
    a = _dot(h, wa_ref[...].astype(BF16))
    u = _dot(h, wu_ref[...].astype(BF16))
    act = (a * _sigmoid(a) * u).astype(BF16)
    o_ref[...] += _dot(act, wo_ref[...].astype(BF16))

    @pl.when(j == nj - 1)
    def _():
        gain = g_ref[1:2, :] * (FFN_RES * mod_ref[0, k0 + 2:k0 + 3, :])
        _postnorm_residual(o_ref, x_ref, o_ref, gain)


def _ffn(x, mod, g2, w_in, w_out, layer, k0, mod_row, tm, tf, x_buffers):
    r, d = x.shape
    nj = D_FF // tf
    return pl.pallas_call(
        functools.partial(_ffn_kernel, k0=k0, nj=nj),
        grid=(r // tm, nj),
        in_specs=[
            pl.BlockSpec((tm, d), lambda i, j: (i, 0), pipeline_mode=pl.Buffered(x_buffers)),
            pl.BlockSpec((1, N_MOD, d), lambda i, j: (mod_row(i, tm), 0, 0)),
            pl.BlockSpec((2, d), lambda i, j: (0, 0)),
            pl.BlockSpec((None, d, tf), lambda i, j: (layer, 0, j)),
            pl.BlockSpec((None, d, tf), lambda i, j: (layer, 0, nj + j)),
            pl.BlockSpec((None, tf, d), lambda i, j: (layer, j, 0)),
        ],
        out_specs=pl.BlockSpec((tm, d), lambda i, j: (i, 0)),
        out_shape=jax.ShapeDtypeStruct((r, d), F32),
        scratch_shapes=[pltpu.VMEM((tm, d), BF16)],
        compiler_params=_cparams(("parallel", "arbitrary"), 60),
        name="ffn",
    )(x, mod, g2, w_in, w_in, w_out)


def _inproj_kernel(x_ref, mod_ref, g_ref, w_ref, o_ref, h_ref):
    @pl.when(pl.program_id(1) == 0)
    def _():
        gain = g_ref[...] * (1.0 + mod_ref[0, 4:5, :])
        _prenorm_modulate(x_ref, h_ref, gain, mod_ref[0, 3:4, :])

    o_ref[...] = _dot_nt(h_ref[...], w_ref[...])


def _inproj(x, mod, g, w, layer, mod_row, tm, tn):
    r, d = x.shape
    return pl.pallas_call(
        _inproj_kernel,
        grid=(r // tm, P_W // tn),
        in_specs=[
            pl.BlockSpec((tm, d), lambda i, j: (i, 0)),
            pl.BlockSpec((1, N_MOD, d), lambda i, j: (mod_row(i, tm), 0, 0)),
            pl.BlockSpec((1, d), lambda i, j: (0, 0)),
            pl.BlockSpec((None, tn, d), lambda i, j: (layer, j, 0)),
        ],
        out_specs=pl.BlockSpec((tm, tn), lambda i, j: (i, j)),
        out_shape=jax.ShapeDtypeStruct((r, P_W), F32),
        scratch_shapes=[pltpu.VMEM((tm, d), BF16)],
        compiler_params=_cparams(("parallel", "arbitrary"), 56),
        name="inproj",
    )(x, mod, g, w)


def _outproj_kernel(x_ref, yr_ref, yc_ref, ym_ref, mod_ref, g_ref, w_ref, o_ref):
    y = _dot(yr_ref[...], w_ref[0:RET_W, :])
    y += _dot(yc_ref[...], w_ref[RET_W:RET_W + CONV_CH, :])
    y += _dot(ym_ref[...], w_ref[RET_W + CONV_CH:, :])
    o_ref[...] = y
    _postnorm_residual(o_ref, x_ref, o_ref, g_ref[...] * mod_ref[0, 5:6, :])


def _outproj(x, y_ret, y_conv, y_mla, mod, g, w, layer, mod_row, tm):
    r, d = x.shape
    return pl.pallas_call(
        _outproj_kernel,
        grid=(r // tm,),
        in_specs=[
            pl.BlockSpec((tm, d), lambda i: (i, 0)),
            pl.BlockSpec((tm, RET_W), lambda i: (i, 0)),
            pl.BlockSpec((tm, CONV_CH), lambda i: (i, 0)),
            pl.BlockSpec((tm, MLA_W), lambda i: (i, 0)),
            pl.BlockSpec((1, N_MOD, d), lambda i: (mod_row(i, tm), 0, 0)),
            pl.BlockSpec((1, d), lambda i: (0, 0)),
            pl.BlockSpec((None, d, d), lambda i: (layer, 0, 0), pipeline_mode=pl.Buffered(1)),
        ],
        out_specs=pl.BlockSpec((tm, d), lambda i: (i, 0)),
        out_shape=jax.ShapeDtypeStruct((r, d), F32),
        compiler_params=_cparams(("parallel",), 60),
        name="outproj",
    )(x, y_ret, y_conv, y_mla, mod, g, w)


def _ret_kernel(ql_ref, kl_ref, vl_ref, gl_ref, qc_ref, kc_ref, vc_ref, gc_ref,
                rc_ref, rs1_ref, rs2_ref, dm_ref, kf_ref, kb_ref, qf_ref, qb_ref, cd_ref, gn_ref,
                yl_ref, yc_ref, qr_ref, kr_ref, sf_ref, *, n_lat, n_ctx):
    c_len = RET_CHUNK
    lane = lax.broadcasted_iota(jnp.int32, (1, LANES), 1)
    head_masks = (lane < RET_DK, lane >= RET_DK)
    row = lax.broadcasted_iota(jnp.int32, (LANES, 2 * RET_DV), 0)
    col = lax.broadcasted_iota(jnp.int32, (LANES, 2 * RET_DV), 1)
    block_diag = (row < RET_DK) == (col < RET_DV)
    kf, kb, qf, qb, cd = kf_ref[0], kb_ref[0], qf_ref[0], qb_ref[0], cd_ref[0]
    gn = gn_ref[...]
    zero_state = jnp.zeros((LANES, 2 * RET_DV), F32)

    def kv_state(k, v16, dec):
        return jnp.where(block_diag, _dot_tn((k * dec).astype(BF16), v16), 0.0)

    def chunk_out(q, k, v16, g, sf16, sb16):
        k16 = k.astype(BF16)
        parts = []
        for e in range(2):
            q16 = jnp.where(head_masks[e], q, 0.0).astype(BF16)
            a = _dot_nt(q16, k16) * dm_ref[0, e]
            parts.append(_dot(a.astype(BF16), v16[:, e * RET_DV:(e + 1) * RET_DV]))
        o = jnp.concatenate(parts, axis=1)
        o = o + _dot((q * qf).astype(BF16), sf16) + _dot((q * qb).astype(BF16), sb16)
        normed = []
        for e in range(2):
            oe = o[:, e * RET_DV:(e + 1) * RET_DV]
            dlt = oe - jnp.mean(oe, axis=-1, keepdims=True)
            var = jnp.mean(dlt * dlt, axis=-1, keepdims=True)
            normed.append(dlt * lax.rsqrt(var + EPS))
        on = jnp.concatenate(normed, axis=1) * gn
        return ((g * _sigmoid(g)) * on).astype(BF16)

    cq = [qc_ref[c * c_len:(c + 1) * c_len, :] for c in range(n_ctx)]
    ck = [kc_ref[c * c_len:(c + 1) * c_len, :] for c in range(n_ctx)]
    cv = [vc_ref[c * c_len:(c + 1) * c_len, :].astype(BF16) for c in range(n_ctx)]
    kvf = [kv_state(ck[c], cv[c], kf) for c in range(n_ctx)]
    kvb = [kv_state(ck[c], cv[c], kb) for c in range(n_ctx)]
    sf_list = [zero_state]
    for c in range(n_ctx):
        sf_list.append(cd * sf_list[c] + kvf[c])
    sb_list = [zero_state] * (n_ctx + 1)
    for c in range(n_ctx - 1, -1, -1):
        sb_list[c] = cd * sb_list[c + 1] + kvb[c]
    for c in range(n_ctx):
        yc_ref[c * c_len:(c + 1) * c_len, :] = chunk_out(
            cq[c], ck[c], cv[c], gc_ref[c * c_len:(c + 1) * c_len, :],
            sf_list[c].astype(BF16), sb_list[c + 1].astype(BF16))
    s0f, s0b = sf_list[n_ctx], sb_list[0]

    def fwd(c, sf):
        r0 = pl.multiple_of(c * c_len, c_len)
        rows = pl.ds(r0, c_len)
        rc, rs1, rs2 = rc_ref[rows, :], rs1_ref[rows, :], rs2_ref[rows, :]
        q = _rope(ql_ref[rows, :], rc, rs1, rs2)
        k = _rope(kl_ref[rows, :], rc, rs1, rs2)
        qr_ref[rows, :] = q
        kr_ref[rows, :] = k
        sf_ref[c] = sf.astype(BF16)
        return cd * sf + kv_state(k, vl_ref[rows, :].astype(BF16), kf)

    lax.fori_loop(0, n_lat, fwd, s0f, unroll=RET_UNROLL)

    def bwd(i, sb):
        c = n_lat - 1 - i
        r0 = pl.multiple_of(c * c_len, c_len)
        rows = pl.ds(r0, c_len)
        q, k = qr_ref[rows, :], kr_ref[rows, :]
        v16 = vl_ref[rows, :].astype(BF16)
        yl_ref[rows, :] = chunk_out(q, k, v16, gl_ref[rows, :], sf_ref[c], sb.astype(BF16))
        return cd * sb + kv_state(k, v16, kb)

    lax.fori_loop(0, n_lat, bwd, s0b, unroll=RET_UNROLL)


def _retention(p_lat, p_ctx, tabs, gn, batch):
    t, lc = SEQ, CTX_LEN
    n_pairs = RET_HEADS // 2
    n_lat, n_ctx = t // RET_CHUNK, lc // RET_CHUNK
    qw, vw = 2 * RET_DK, 2 * RET_DV

    def pspec(rows, width, col0):
        return pl.BlockSpec((rows, width), lambda b, hp: (b, col0 // width + hp))

    tab3 = pl.BlockSpec((1, LANES, LANES), lambda b, hp: (hp, 0, 0))
    rope_spec = pl.BlockSpec((t, LANES), lambda b, hp: (0, 0))
    return pl.pallas_call(
        functools.partial(_ret_kernel, n_lat=n_lat, n_ctx=n_ctx),
        grid=(batch, n_pairs),
        in_specs=[
            pspec(t, qw, P_RQ), pspec(t, qw, P_RK), pspec(t, vw, P_RV), pspec(t, vw, P_RG),
            pspec(lc, qw, P_RQ), pspec(lc, qw, P_RK), pspec(lc, vw, P_RV), pspec(lc, vw, P_RG),
            rope_spec, rope_spec, rope_spec,
            pl.BlockSpec((1, 2, LANES, LANES), lambda b, hp: (hp, 0, 0, 0)),
            tab3, tab3, tab3, tab3,
            pl.BlockSpec((1, LANES, vw), lambda b, hp: (hp, 0, 0)),
            pl.BlockSpec((1, vw), lambda b, hp: (0, hp)),
        ],
        out_specs=[
            pl.BlockSpec((t, vw), lambda b, hp: (b, hp)),
            pl.BlockSpec((lc, vw), lambda b, hp: (b, hp)),
        ],
        out_shape=[
            jax.ShapeDtypeStruct((batch * t, RET_W), BF16),
            jax.ShapeDtypeStruct((batch * lc, RET_W), BF16),
        ],
        scratch_shapes=[
            pltpu.VMEM((t, LANES), F32),
            pltpu.VMEM((t, LANES), F32),
            pltpu.VMEM((n_lat, LANES, vw), BF16),
        ],
        compiler_params=_cparams(("parallel", "parallel"), 48),
        name="retention",
    )(p_lat, p_lat, p_lat, p_lat, p_ctx, p_ctx, p_ctx, p_ctx,
      tabs["ret_c"], tabs["ret_s1"], tabs["ret_s2"],
      tabs["dm"], tabs["kf"], tabs["kb"], tabs["qf"], tabs["qb"], tabs["cd"], gn)


def _conv_kernel(glu_ref, dw_ref, misc_ref, pw_ref, o_ref, pad_ref, *, t, rt):
    half = CONV_W // 2
    lo = 16
    zeros = jnp.zeros((lo, CONV_CH), F32)
    pad_ref[0:lo, :] = zeros
    pad_ref[lo + t:lo + t + lo, :] = zeros

    def glu(r, carry):
        rows = pl.ds(pl.multiple_of(r * rt, rt), rt)
        a = glu_ref[rows, 0:CONV_CH]
        b = glu_ref[rows, CONV_CH:2 * CONV_CH]
        pad_ref[pl.ds(pl.multiple_of(r * rt + lo, 8), rt), :] = a * _sigmoid(b)
        return carry

    lax.fori_loop(0, t // rt, glu, 0)

    def conv(r, carry):
        base = pl.multiple_of(r * rt, rt)
        win_rows = rt + 2 * lo
        cg_w = LANES
        accs = []
        for cg in range(CONV_CH // cg_w):
            cols = slice(cg * cg_w, (cg + 1) * cg_w)
            win = pad_ref[pl.ds(base, win_rows), cols]
            acc = jnp.zeros((rt, cg_w), F32) + misc_ref[0:1, cols]
            for phase in range(SUBLANES):
                shifted = win if phase == 0 else pltpu.roll(win, win_rows - phase, 0)
                for off in range(phase, lo + half + 1, SUBLANES):
                    k = off - (lo - half)
                    if 0 <= k < CONV_W:
                        a0 = off - phase
                        acc = acc + shifted[a0:a0 + rt, :] * dw_ref[k:k + 1, cols]
            accs.append(acc)
        acc = jnp.concatenate(accs, axis=1)
        dlt = acc - jnp.mean(acc, axis=-1, keepdims=True)
        var = jnp.mean(dlt * dlt, axis=-1, keepdims=True)
        y = dlt * lax.rsqrt(var + EPS) * misc_ref[1:2, :] + misc_ref[2:3, :]
        y = y * _sigmoid(y)
        o_ref[pl.ds(base, rt), :] = _dot(y.astype(BF16), pw_ref[...]).astype(BF16)
        return carry

    lax.fori_loop(0, t // rt, conv, 0, unroll=2)


def _conv(p, dw, misc, pw, batch, t):
    rt = 64
    return pl.pallas_call(
        functools.partial(_conv_kernel, t=t, rt=rt),
        grid=(batch,),
        in_specs=[
            pl.BlockSpec((t, 2 * CONV_CH), lambda b: (b, P_GLU // (2 * CONV_CH))),
            pl.BlockSpec((CONV_W, CONV_CH), lambda b: (0, 0)),
            pl.BlockSpec((3, CONV_CH), lambda b: (0, 0)),
            pl.BlockSpec((CONV_CH, CONV_CH), lambda b: (0, 0)),
        ],
        out_specs=pl.BlockSpec((t, CONV_CH), lambda b: (b, 0)),
        out_shape=jax.ShapeDtypeStruct((batch * t, CONV_CH), BF16),
        scratch_shapes=[pltpu.VMEM((t + 32, CONV_CH), F32)],
        compiler_params=_cparams(("parallel",), 40),
        name="conv",
    )(p, dw, misc, pw)


def _mla_proj_kernel(cq_ref, ckv_ref, kr_ref, nq_ref, nkv_ref, wq_ref, wkv_ref, *rest, rope):
    if rope:
        rc_ref, rs1_ref, rs2_ref, q_ref, k_ref, v_ref = rest
        rot = lambda z: _rope(z, rc_ref[...], rs1_ref[...], rs2_ref[...])
    else:
        q_ref, k_ref, v_ref = rest
        rot = lambda z: z
    qq = _dot(_rms(cq_ref[...], nq_ref[...]).astype(BF16), wq_ref[...]) * (MLA_SCALE * LOG2_E)
    kv = _dot(_rms(ckv_ref[...], nkv_ref[...]).astype(BF16), wkv_ref[...])
    kr = rot(kr_ref[...]).astype(BF16)
    hw = MLA_NOPE + LANES
    for h in range(MLA_HEADS):
        q_ref[:, h * hw:h * hw + MLA_NOPE] = qq[:, h * MLA_NOPE:(h + 1) * MLA_NOPE].astype(BF16)
        qr = qq[:, MLA_HEADS * MLA_NOPE + h * LANES:MLA_HEADS * MLA_NOPE + (h + 1) * LANES]
        q_ref[:, h * hw + MLA_NOPE:(h + 1) * hw] = rot(qr).astype(BF16)
        k_ref[:, h * hw:h * hw + MLA_NOPE] = kv[:, 2 * h * LANES:(2 * h + 1) * LANES].astype(BF16)
        k_ref[:, h * hw + MLA_NOPE:(h + 1) * hw] = kr
        v_ref[:, h * MLA_DV:(h + 1) * MLA_DV] = kv[:, (2 * h + 1) * LANES:(2 * h + 2) * LANES].astype(BF16)


def _mla_proj(p, nq, nkv, wq, wkv, rope_tabs, tm):
    r = p.shape[0]
    hw = MLA_NOPE + LANES
    rope = rope_tabs is not None
    in_specs = [
        pl.BlockSpec((tm, MLA_Q_RANK), lambda i: (i, P_CQ // MLA_Q_RANK)),
        pl.BlockSpec((tm, MLA_KV_RANK), lambda i: (i, P_CKV // MLA_KV_RANK)),
        pl.BlockSpec((tm, LANES), lambda i: (i, P_KR // LANES)),
        pl.BlockSpec((1, MLA_Q_RANK), lambda i: (0, 0)),
        pl.BlockSpec((1, MLA_KV_RANK), lambda i: (0, 0)),
        pl.BlockSpec(wq.shape, lambda i: (0, 0)),
        pl.BlockSpec(wkv.shape, lambda i: (0, 0)),
    ]
    args = [p, p, p, nq, nkv, wq, wkv]
    if rope:
        n_pos = SEQ // tm
        in_specs += [pl.BlockSpec((tm, LANES), lambda i: (i % n_pos, 0))] * 3
        args += list(rope_tabs)
    return pl.pallas_call(
        functools.partial(_mla_proj_kernel, rope=rope),
        grid=(r // tm,),
        in_specs=in_specs,
        out_specs=[
            pl.BlockSpec((tm, MLA_HEADS * hw), lambda i: (i, 0)),
            pl.BlockSpec((tm, MLA_HEADS * hw), lambda i: (i, 0)),
            pl.BlockSpec((tm, MLA_W), lambda i: (i, 0)),
        ],
        out_shape=[
            jax.ShapeDtypeStruct((r, MLA_HEADS * hw), BF16),
            jax.ShapeDtypeStruct((r, MLA_HEADS * hw), BF16),
            jax.ShapeDtypeStruct((r, MLA_W), BF16),
        ],
        compiler_params=_cparams(("parallel",), 40),
        name="mla_proj",
    )(*args)


def _attn_kernel(q_ref, kl_ref, kc_ref, vl_ref, vc_ref, o_ref, *, n_heads, sub):
    hw = MLA_NOPE + LANES
    for e in range(n_heads):
        kcols = slice(e * hw, (e + 1) * hw)
        vcols = slice(e * MLA_DV, (e + 1) * MLA_DV)
        for i in range(q_ref.shape[0] // sub):
            rows = slice(i * sub, (i + 1) * sub)
            q = q_ref[rows, kcols]
            sl = _dot_nt(q, kl_ref[:, kcols])
            sc = _dot_nt(q, kc_ref[:, kcols])
            m = jnp.maximum(jnp.max(sl, axis=-1, keepdims=True), jnp.max(sc, axis=-1, keepdims=True))
            el = jnp.exp2(sl - m)
            ec = jnp.exp2(sc - m)
            den = jnp.sum(el, axis=-1, keepdims=True) + jnp.sum(ec, axis=-1, keepdims=True)
            o = _dot(el.astype(BF16), vl_ref[:, vcols]) + _dot(ec.astype(BF16), vc_ref[:, vcols])
            o_ref[rows, vcols] = (o / den).astype(BF16)


def _attn(q, k_lat, k_ctx, v_lat, v_ctx, batch, n_heads, sub):
    hw = n_heads * (MLA_NOPE + LANES)
    vw = n_heads * MLA_DV
    return pl.pallas_call(
        functools.partial(_attn_kernel, n_heads=n_heads, sub=sub),
        grid=(batch, MLA_HEADS // n_heads),
        in_specs=[
            pl.BlockSpec((SEQ, hw), lambda b, h: (b, h)),
            pl.BlockSpec((SEQ, hw), lambda b, h: (b, h)),
            pl.BlockSpec((CTX_LEN, hw), lambda b, h: (b, h)),
            pl.BlockSpec((SEQ, vw), lambda b, h: (b, h)),
            pl.BlockSpec((CTX_LEN, vw), lambda b, h: (b, h)),
        ],
        out_specs=pl.BlockSpec((SEQ, vw), lambda b, h: (b, h)),
        out_shape=jax.ShapeDtypeStruct((batch * SEQ, MLA_W), BF16),
        compiler_params=_cparams(("parallel", "parallel"), 48),
        name="mla_attn",
    )(q, k_lat, k_ctx, v_lat, v_ctx)


def _attn_ctx_kernel(q_ref, k_ref, v_ref, o_ref):
    s = _dot_nt(q_ref[...], k_ref[...])
    e = jnp.exp2(s - jnp.max(s, axis=-1, keepdims=True))
    o = _dot(e.astype(BF16), v_ref[...])
    o_ref[...] = (o / jnp.sum(e, axis=-1, keepdims=True)).astype(BF16)


def _attn_ctx(q, k, v, batch):
    hw = MLA_NOPE + LANES
    return pl.pallas_call(
        _attn_ctx_kernel,
        grid=(batch, MLA_HEADS),
        in_specs=[
            pl.BlockSpec((CTX_LEN, hw), lambda b, h: (b, h)),
            pl.BlockSpec((CTX_LEN, hw), lambda b, h: (b, h)),
            pl.BlockSpec((CTX_LEN, MLA_DV), lambda b, h: (b, h)),
        ],
        out_specs=pl.BlockSpec((CTX_LEN, MLA_DV), lambda b, h: (b, h)),
        out_shape=jax.ShapeDtypeStruct((batch * CTX_LEN, MLA_W), BF16),
        compiler_params=_cparams(("parallel", "parallel"), 32),
        name="mla_attn_ctx",
    )(q, k, v)


def _tables():
    rows = SEQ // GRID_W
    quarter = ROPE_DIM // 4
    inv = ROPE_BASE ** (-jnp.arange(quarter, dtype=F32) / quarter)
    r_idx = jnp.repeat(jnp.arange(rows, dtype=F32), GRID_W)
    c_idx = jnp.tile(jnp.arange(GRID_W, dtype=F32), rows)
    ang = jnp.concatenate([r_idx[:, None] * inv, c_idx[:, None] * inv], axis=-1)
    cos, sin = jnp.cos(ang), jnp.sin(ang)
    z = jnp.zeros_like(cos)
    tabs = {
        "ret_c": jnp.concatenate([cos, cos, cos, cos], axis=1),
        "ret_s1": jnp.concatenate([-sin, z, -sin, z], axis=1),
        "ret_s2": jnp.concatenate([z, sin, z, sin], axis=1),
        "mla_c": jnp.concatenate([cos, cos, z, z], axis=1),
        "mla_s1": jnp.concatenate([-sin, z, z, z], axis=1),
        "mla_s2": jnp.concatenate([z, sin, z, z], axis=1),
    }
    c_len = RET_CHUNK
    log_g = jnp.log1p(-jnp.exp2(-5.0 - jnp.arange(RET_HEADS, dtype=F32)))
    i = jnp.arange(c_len, dtype=F32)
    k_scale = RET_DK ** -0.5
    lg = log_g[:, None, None]
    dm = jnp.exp(lg * jnp.abs(i[:, None] - i[None, :])) * k_scale
    tabs["dm"] = dm.reshape(RET_HEADS // 2, 2, c_len, c_len)

    def lanes_by_head(per_head_rows):
        t = jnp.repeat(per_head_rows[:, :, None], RET_DK, axis=2)
        t = t.reshape(RET_HEADS // 2, 2, c_len, RET_DK)
        return jnp.concatenate([t[:, 0], t[:, 1]], axis=-1)

    tabs["kf"] = lanes_by_head(jnp.exp(log_g[:, None] * (c_len - 1.0 - i)) * k_scale)
    tabs["kb"] = lanes_by_head(jnp.exp(log_g[:, None] * i) * k_scale)
    tabs["qf"] = lanes_by_head(jnp.exp(log_g[:, None] * (i + 1.0)))
    tabs["qb"] = lanes_by_head(jnp.exp(log_g[:, None] * (c_len - i)))
    cdec = jnp.exp(log_g * c_len).reshape(RET_HEADS // 2, 2)
    cd = jnp.repeat(cdec[:, :, None], RET_DK, axis=2).reshape(RET_HEADS // 2, 2 * RET_DK, 1)
    tabs["cd"] = jnp.broadcast_to(cd, (RET_HEADS // 2, 2 * RET_DK, 2 * RET_DV))
    return tabs


def _cast_kernel(x_ref, o_ref):
    o_ref[...] = x_ref[...].astype(BF16)


def _cast_bf16(w, tr):
    depth, r, c = w.shape
    return pl.pallas_call(
        _cast_kernel,
        grid=(depth, r // tr),
        in_specs=[pl.BlockSpec((1, tr, c), lambda l, i: (l, i, 0))],
        out_specs=pl.BlockSpec((1, tr, c), lambda l, i: (l, i, 0)),
        out_shape=jax.ShapeDtypeStruct(w.shape, BF16),
        compiler_params=_cparams(("parallel", "parallel"), 48),
        name="cast",
    )(w)


_W_IN_SRC = ((2304, 1024), (3328, 512), (768, 768), (1536, 768), (0, 384), (384, 384), (3840, 256),
             (4096, 64))


def _prep_w_in_kernel(w_ref, o_ref):
    dst = 0
    for src, width in _W_IN_SRC:
        o_ref[0, dst:dst + width, :] = w_ref[0, src:src + width, :].astype(BF16)
        dst += width
    o_ref[0, dst:, :] = jnp.zeros((P_W - dst, o_ref.shape[2]), BF16)


def _prep_w_in(w):
    depth, d, n = w.shape
    wt = jnp.swapaxes(w, 1, 2)
    tk = 512
    return pl.pallas_call(
        _prep_w_in_kernel,
        grid=(depth, d // tk),
        in_specs=[pl.BlockSpec((1, n, tk), lambda l, i: (l, 0, i))],
        out_specs=pl.BlockSpec((1, P_W, tk), lambda l, i: (l, 0, i)),
        out_shape=jax.ShapeDtypeStruct((depth, P_W, d), BF16),
        compiler_params=_cparams(("parallel", "parallel"), 48),
        name="prep_w_in",
    )(wt)


def _prep_w_uq(w):
    w = w.reshape(MLA_Q_RANK, MLA_HEADS, MLA_NOPE + MLA_ROPE)
    nope = w[:, :, :MLA_NOPE].reshape(MLA_Q_RANK, MLA_HEADS * MLA_NOPE)
    rope = jnp.pad(w[:, :, MLA_NOPE:], ((0, 0), (0, 0), (0, LANES - MLA_ROPE)))
    return jnp.concatenate([nope, rope.reshape(MLA_Q_RANK, MLA_HEADS * LANES)], axis=1).astype(BF16)


def _lat_mod_row(i, tm):
    return (i * tm) // SEQ


def _ctx_mod_row(i, tm):
    return CTX_MOD_ROW


def kernel(x, c, ctx, c_ctx, w_ada, b_ada, norm_g, w_ffa_in, w_ffa_out, w_ffb_in, w_ffb_out,
           w_in, w_out, ret_gn, cv_dw, cv_dw_b, cv_ln_g, cv_ln_b, cv_pw,
           mla_q_norm, w_uq, mla_kv_norm, w_ukv):
    batch, seq, d = x.shape
    assert (seq, d, ctx.shape[1]) == (SEQ, D_MODEL, CTX_LEN)
    xl = x.reshape(batch * seq, d)
    xc = ctx.reshape(batch * CTX_LEN, d)
    s_in = jnp.concatenate([c, c_ctx[None, :], jnp.zeros((ROWS_PAD - batch - 1, d), F32)], axis=0)
    mod_all = _ada(s_in, w_ada, b_ada).reshape(DEPTH, ROWS_PAD, N_MOD, d)
    tabs = _tables()
    mla_rope = (tabs["mla_c"], tabs["mla_s1"], tabs["mla_s2"])

    wa_in, wa_out, wb_in, wb_out = w_ffa_in, w_ffa_out, w_ffb_in, w_ffb_out
    wi = _prep_w_in(w_in)
    wo = _cast_bf16(w_out, 1024)

    ffn_lat = dict(mod_row=_lat_mod_row, tm=1024, tf=256, x_buffers=2)
    ffn_ctx = dict(mod_row=_ctx_mod_row, tm=1024, tf=256, x_buffers=2)
    for l in range(DEPTH):
        last = l == DEPTH - 1
        mod = mod_all[l]
        ng = norm_g[l]
        wq = _prep_w_uq(w_uq[l])
        wkv = w_ukv[l].astype(BF16)
        nq, nkv = mla_q_norm[l][None, :], mla_kv_norm[l][None, :]
        gn = ret_gn[l][None, :]
        conv_misc = jnp.stack([cv_dw_b[l], cv_ln_g[l], cv_ln_b[l]], axis=0)
        pw = cv_pw[l].astype(BF16)

        xl = _ffn(xl, mod, ng[0:2], wa_in, wa_out, l, 0, **ffn_lat)
        xc = _ffn(xc, mod, ng[0:2], wa_in, wa_out, l, 0, **ffn_ctx)

        p_lat = _inproj(xl, mod, ng[2:3], wi, l, _lat_mod_row, 1024, 1408)
        p_ctx = _inproj(xc, mod, ng[2:3], wi, l, _ctx_mod_row, 1024, 1408)

        y_ret_l, y_ret_c = _retention(p_lat, p_ctx, tabs, gn, batch)
        y_conv_l = _conv(p_lat, cv_dw[l], conv_misc, pw, batch, SEQ)
        q_l, k_l, v_l = _mla_proj(p_lat, nq, nkv, wq, wkv, mla_rope, 512)
        q_c, k_c, v_c = _mla_proj(p_ctx, nq, nkv, wq, wkv, None, 512)
        y_mla_l = _attn(q_l, k_l, k_c, v_l, v_c, batch, 2, 512)

        xl = _outproj(xl, y_ret_l, y_conv_l, y_mla_l, mod, ng[3:4], wo, l, _lat_mod_row, 1024)
        xl = _ffn(xl, mod, ng[4:6], wb_in, wb_out, l, 6, **ffn_lat)
        if not last:
            y_conv_c = _conv(p_ctx, cv_dw[l], conv_misc, pw, batch, CTX_LEN)
            y_mla_c = _attn_ctx(q_c, k_c, v_c, batch)
            xc = _outproj(xc, y_ret_c, y_conv_c, y_mla_c, mod, ng[3:4], wo, l, _ctx_mod_row, 1024)
            xc = _ffn(xc, mod, ng[4:6], wb_in, wb_out, l, 6, **ffn_ctx)
    return xl.reshape(batch, seq, d)
```

```python
import functools

import jax
import jax.numpy as jnp
from jax import lax
from jax.experimental import pallas as pl
from jax.experimental.pallas import tpu as pltpu

F32 = jnp.float32
BF16 = jnp.bfloat16

D_MODEL = 2048
SEQ = 2048
DEPTH = 2
GRID_W = 64
CTX_LEN = 256
D_FF = 5632
FFN_RES = 0.5
N_MOD = 9
ROPE_DIM = 64
ROPE_BASE = 10000.0
EPS = 1e-6
RET_HEADS = 6
RET_DK = 64
RET_DV = 128
RET_CHUNK = 128
CONV_CH = 512
CONV_W = 31
MLA_HEADS = 6
MLA_Q_RANK = 512
MLA_KV_RANK = 256
MLA_NOPE = 128
MLA_ROPE = 64
MLA_DV = 128
MLA_SCALE = (MLA_NOPE + MLA_ROPE) ** -0.5
LOG2_E = 1.4426950408889634
RET_W = RET_HEADS * RET_DV
MLA_W = MLA_HEADS * MLA_DV

LANES = 128
MIB = 1024 * 1024
ROW_CHUNK = 128
RET_UNROLL = 8
SUBLANES = 8

P_GLU, P_CQ, P_RV, P_RG, P_RQ, P_RK, P_CKV, P_KR = 0, 1024, 1536, 2304, 3072, 3456, 3840, 4096
P_W = 4224
CTX_MOD_ROW = 4
ROWS_PAD = 8


def _cparams(sem, vmem_mib):
    return pltpu.CompilerParams(dimension_semantics=sem, vmem_limit_bytes=int(vmem_mib * MIB))


def _sigmoid(x):
    return 1.0 / (1.0 + jnp.exp(-x))


def _rms(x, g):
    ms = jnp.mean(x * x, axis=-1, keepdims=True)
    return x * lax.rsqrt(ms + EPS) * g


def _dot(a, b):
    return jnp.dot(a, b, preferred_element_type=F32)


def _prenorm_modulate(x_ref, h_ref, gain_row, shift_row):
    def body(i, carry):
        rows = pl.ds(pl.multiple_of(i * ROW_CHUNK, ROW_CHUNK), ROW_CHUNK)
        x = x_ref[rows, :]
        r = lax.rsqrt(jnp.mean(x * x, axis=-1, keepdims=True) + EPS)
        h_ref[rows, :] = (x * r * gain_row + shift_row).astype(BF16)
        return carry

    lax.fori_loop(0, x_ref.shape[0] // ROW_CHUNK, body, 0)


def _postnorm_residual(y_ref, x_ref, o_ref, gain_row):
    def body(i, carry):
        rows = pl.ds(pl.multiple_of(i * ROW_CHUNK, ROW_CHUNK), ROW_CHUNK)
        y = y_ref[rows, :]
        r = lax.rsqrt(jnp.mean(y * y, axis=-1, keepdims=True) + EPS)
        o_ref[rows, :] = x_ref[rows, :] + y * r * gain_row
        return carry

    lax.fori_loop(0, y_ref.shape[0] // ROW_CHUNK, body, 0)


def _dot_nt(a, b):
    return lax.dot_general(a, b, (((1,), (1,)), ((), ())), preferred_element_type=F32)


def _dot_tn(a, b):
    return lax.dot_general(a, b, (((0,), (0,)), ((), ())), preferred_element_type=F32)


def _rope(x, c, s1, s2):
    return x * c + pltpu.roll(x, 96, 1) * s1 + pltpu.roll(x, 32, 1) * s2


def _ada_kernel(s_ref, w_ref, b_ref, o_ref):
    s = s_ref[...]
    s = s * _sigmoid(s)
    o_ref[0] = _dot(s.astype(BF16), w_ref[0].astype(BF16)) + b_ref[0]


def _ada(s_in, w_ada, b_ada):
    depth, d, n = w_ada.shape
    tn = 1024
    return pl.pallas_call(
        _ada_kernel,
        grid=(depth, n // tn),
        in_specs=[
            pl.BlockSpec((ROWS_PAD, d), lambda l, j: (0, 0)),
            pl.BlockSpec((1, d, tn), lambda l, j: (l, 0, j)),
            pl.BlockSpec((1, 1, tn), lambda l, j: (l, 0, j)),
        ],
        out_specs=pl.BlockSpec((1, ROWS_PAD, tn), lambda l, j: (l, 0, j)),
        out_shape=jax.ShapeDtypeStruct((depth, ROWS_PAD, n), F32),
        compiler_params=_cparams(("parallel", "parallel"), 40),
        name="ada",
    )(s_in, w_ada, b_ada.reshape(depth, 1, n))


def _ffn_kernel(x_ref, mod_ref, g_ref, wa_ref, wu_ref, wo_ref, o_hbm, acc_ref, h_ref, sem, *, k0, nj):
    i = pl.program_id(0)
    j = pl.program_id(1)
    tm = acc_ref.shape[0]
    n_chunks = tm // ROW_CHUNK

    def out_copy(c):
        src = acc_ref.at[pl.ds(pl.multiple_of(c * ROW_CHUNK, ROW_CHUNK), ROW_CHUNK), :]
        dst = o_hbm.at[pl.ds(pl.multiple_of(i * tm + c * ROW_CHUNK, ROW_CHUNK), ROW_CHUNK), :]
        return pltpu.make_async_copy(src, dst, sem.at[c])

    @pl.when(j == 0)
    def _():
        gain = g_ref[0:1, :] * (1.0 + mod_ref[0, k0 + 1:k0 + 2, :])
        _prenorm_modulate(x_ref, h_ref, gain, mod_ref[0, k0:k0 + 1, :])
        acc_ref[...] = jnp.zeros_like(acc_ref)

    h = h_ref[...]
    a = _dot(h, wa_ref[...].astype(BF16))
    u = _dot(h, wu_ref[...].astype(BF16))
    act = (a * _sigmoid(a) * u).astype(BF16)
    acc_ref[...] += _dot(act, wo_ref[...].astype(BF16))

    @pl.when(j == nj - 1)
    def _():
        gain = g_ref[1:2, :] * (FFN_RES * mod_ref[0, k0 + 2:k0 + 3, :])

        def finish(c, carry):
            rows = pl.ds(pl.multiple_of(c * ROW_CHUNK, ROW_CHUNK), ROW_CHUNK)
            y = acc_ref[rows, :]
            r = lax.rsqrt(jnp.mean(y * y, axis=-1, keepdims=True) + EPS)
            acc_ref[rows, :] = x_ref[rows, :] + y * r * gain
            out_copy(c).start()
            return carry

        lax.fori_loop(0, n_chunks, finish, 0)

        def drain(c, carry):
            out_copy(c).wait()
            return carry

        lax.fori_loop(0, n_chunks, drain, 0)


def _ffn(x, mod, g2, w_in, w_out, layer, k0, mod_row, tm, tf):
    r, d = x.shape
    nj = D_FF // tf
    return pl.pallas_call(
        functools.partial(_ffn_kernel, k0=k0, nj=nj),
        grid=(r // tm, nj),
        in_specs=[
            pl.BlockSpec((tm, d), lambda i, j: (i, 0)),
            pl.BlockSpec((1, N_MOD, d), lambda i, j: (mod_row(i, tm), 0, 0)),
            pl.BlockSpec((2, d), lambda i, j: (0, 0)),
            pl.BlockSpec((None, d, tf), lambda i, j: (layer, 0, j)),
            pl.BlockSpec((None, d, tf), lambda i, j: (layer, 0, nj + j)),
            pl.BlockSpec((None, tf, d), lambda i, j: (layer, j, 0)),
        ],
        out_specs=pl.BlockSpec(memory_space=pl.ANY),
        out_shape=jax.ShapeDtypeStruct((r, d), F32),
        scratch_shapes=[
            pltpu.VMEM((tm, d), F32),
            pltpu.VMEM((tm, d), BF16),
            pltpu.SemaphoreType.DMA((tm // ROW_CHUNK,)),
        ],
        compiler_params=_cparams(("arbitrary", "arbitrary"), 60),
        name="ffn",
    )(x, mod, g2, w_in, w_in, w_out)


def _inproj_kernel(x_ref, mod_ref, g_ref, w_ref, o_ref, h_ref):
    @pl.when(pl.program_id(1) == 0)
    def _():
        gain = g_ref[...] * (1.0 + mod_ref[0, 4:5, :])
        _prenorm_modulate(x_ref, h_ref, gain, mod_ref[0, 3:4, :])

    o_ref[...] = _dot_nt(h_ref[...], w_ref[...])


def _inproj(x, mod, g, w, layer, mod_row, tm, tn):
    r, d = x.shape
    return pl.pallas_call(
        _inproj_kernel,
        grid=(r // tm, P_W // tn),
        in_specs=[
            pl.BlockSpec((tm, d), lambda i, j: (i, 0)),
            pl.BlockSpec((1, N_MOD, d), lambda i, j: (mod_row(i, tm), 0, 0)),
            pl.BlockSpec((1, d), lambda i, j: (0, 0)),
            pl.BlockSpec((None, tn, d), lambda i, j: (layer, j, 0)),
        ],
        out_specs=pl.BlockSpec((tm, tn), lambda i, j: (i, j)),
        out_shape=jax.ShapeDtypeStruct((r, P_W), F32),
        scratch_shapes=[pltpu.VMEM((tm, d), BF16)],
        compiler_params=_cparams(("parallel", "arbitrary"), 56),
        name="inproj",
    )(x, mod, g, w)


def _outproj_kernel(x_ref, yr_ref, yc_ref, ym_ref, mod_ref, g_ref, w_ref, o_ref):
    y = _dot(yr_ref[...], w_ref[0:RET_W, :])
    y += _dot(yc_ref[...], w_ref[RET_W:RET_W + CONV_CH, :])
    y += _dot(ym_ref[...], w_ref[RET_W + CONV_CH:, :])
    o_ref[...] = y
    _postnorm_residual(o_ref, x_ref, o_ref, g_ref[...] * mod_ref[0, 5:6, :])


def _outproj(x, y_ret, y_conv, y_mla, mod, g, w, layer, mod_row, tm):
    r, d = x.shape
    return pl.pallas_call(
        _outproj_kernel,
        grid=(r // tm,),
        in_specs=[
            pl.BlockSpec((tm, d), lambda i: (i, 0)),
            pl.BlockSpec((tm, RET_W), lambda i: (i, 0)),
            pl.BlockSpec((tm, CONV_CH), lambda i: (i, 0)),
            pl.BlockSpec((tm, MLA_W), lambda i: (i, 0)),
            pl.BlockSpec((1, N_MOD, d), lambda i: (mod_row(i, tm), 0, 0)),
            pl.BlockSpec((1, d), lambda i: (0, 0)),
            pl.BlockSpec((None, d, d), lambda i: (layer, 0, 0), pipeline_mode=pl.Buffered(1)),
        ],
        out_specs=pl.BlockSpec((tm, d), lambda i: (i, 0)),
        out_shape=jax.ShapeDtypeStruct((r, d), F32),
        compiler_params=_cparams(("parallel",), 60),
        name="outproj",
    )(x, y_ret, y_conv, y_mla, mod, g, w)


def _ret_kernel(ql_ref, kl_ref, vl_ref, gl_ref, qc_ref, kc_ref, vc_ref, gc_ref,
                rc_ref, rs1_ref, rs2_ref, dm_ref, kf_ref, kb_ref, qf_ref, qb_ref, cd_ref, gn_ref,
                yl_ref, yc_ref, qr_ref, kr_ref, sf_ref, *, n_lat, n_ctx):
    c_len = RET_CHUNK
    lane = lax.broadcasted_iota(jnp.int32, (1, LANES), 1)
    head_masks = (lane < RET_DK, lane >= RET_DK)
    row = lax.broadcasted_iota(jnp.int32, (LANES, 2 * RET_DV), 0)
    col = lax.broadcasted_iota(jnp.int32, (LANES, 2 * RET_DV), 1)
    block_diag = (row < RET_DK) == (col < RET_DV)
    kf, kb, qf, qb, cd = kf_ref[0], kb_ref[0], qf_ref[0], qb_ref[0], cd_ref[0]
    gn = gn_ref[...]
    zero_state = jnp.zeros((LANES, 2 * RET_DV), F32)

    def kv_state(k, v16, dec):
        return jnp.where(block_diag, _dot_tn((k * dec).astype(BF16), v16), 0.0)

    def chunk_out(q, k, v16, g, sf16, sb16):
        k16 = k.astype(BF16)
        parts = []
        for e in range(2):
            q16 = jnp.where(head_masks[e], q, 0.0).astype(BF16)
            a = _dot_nt(q16, k16) * dm_ref[0, e]
            parts.append(_dot(a.astype(BF16), v16[:, e * RET_DV:(e + 1) * RET_DV]))
        o = jnp.concatenate(parts, axis=1)
        o = o + _dot((q * qf).astype(BF16), sf16) + _dot((q * qb).astype(BF16), sb16)
        normed = []
        for e in range(2):
            oe = o[:, e * RET_DV:(e + 1) * RET_DV]
            dlt = oe - jnp.mean(oe, axis=-1, keepdims=True)
            var = jnp.mean(dlt * dlt, axis=-1, keepdims=True)
            normed.append(dlt * lax.rsqrt(var + EPS))
        on = jnp.concatenate(normed, axis=1) * gn
        return ((g * _sigmoid(g)) * on).astype(BF16)

    cq = [qc_ref[c * c_len:(c + 1) * c_len, :] for c in range(n_ctx)]
    ck = [kc_ref[c * c_len:(c + 1) * c_len, :] for c in range(n_ctx)]
    cv = [vc_ref[c * c_len:(c + 1) * c_len, :].astype(BF16) for c in range(n_ctx)]
    kvf = [kv_state(ck[c], cv[c], kf) for c in range(n_ctx)]
    kvb = [kv_state(ck[c], cv[c], kb) for c in range(n_ctx)]
    sf_list = [zero_state]
    for c in range(n_ctx):
        sf_list.append(cd * sf_list[c] + kvf[c])
    sb_list = [zero_state] * (n_ctx + 1)
    for c in range(n_ctx - 1, -1, -1):
        sb_list[c] = cd * sb_list[c + 1] + kvb[c]
    for c in range(n_ctx):
        yc_ref[c * c_len:(c + 1) * c_len, :] = chunk_out(
            cq[c], ck[c], cv[c], gc_ref[c * c_len:(c + 1) * c_len, :],
            sf_list[c].astype(BF16), sb_list[c + 1].astype(BF16))
    s0f, s0b = sf_list[n_ctx], sb_list[0]

    def fwd(c, sf):
        r0 = pl.multiple_of(c * c_len, c_len)
        rows = pl.ds(r0, c_len)
        rc, rs1, rs2 = rc_ref[rows, :], rs1_ref[rows, :], rs2_ref[rows, :]
        q = _rope(ql_ref[rows, :], rc, rs1, rs2)
        k = _rope(kl_ref[rows, :], rc, rs1, rs2)
        qr_ref[rows, :] = q
        kr_ref[rows, :] = k
        sf_ref[c] = sf.astype(BF16)
        return cd * sf + kv_state(k, vl_ref[rows, :].astype(BF16), kf)

    lax.fori_loop(0, n_lat, fwd, s0f, unroll=RET_UNROLL)

    def bwd(i, sb):
        c = n_lat - 1 - i
        r0 = pl.multiple_of(c * c_len, c_len)
        rows = pl.ds(r0, c_len)
        q, k = qr_ref[rows, :], kr_ref[rows, :]
        v16 = vl_ref[rows, :].astype(BF16)
        yl_ref[rows, :] = chunk_out(q, k, v16, gl_ref[rows, :], sf_ref[c], sb.astype(BF16))
        return cd * sb + kv_state(k, v16, kb)

    lax.fori_loop(0, n_lat, bwd, s0b, unroll=RET_UNROLL)


def _retention(p_lat, p_ctx, tabs, gn, batch):
    t, lc = SEQ, CTX_LEN
    n_pairs = RET_HEADS // 2
    n_lat, n_ctx = t // RET_CHUNK, lc // RET_CHUNK
    qw, vw = 2 * RET_DK, 2 * RET_DV

    def pspec(rows, width, col0):
        return pl.BlockSpec((rows, width), lambda b, hp: (b, col0 // width + hp))

    tab3 = pl.BlockSpec((1, LANES, LANES), lambda b, hp: (hp, 0, 0))
    rope_spec = pl.BlockSpec((t, LANES), lambda b, hp: (0, 0))
    return pl.pallas_call(
        functools.partial(_ret_kernel, n_lat=n_lat, n_ctx=n_ctx),
        grid=(batch, n_pairs),
        in_specs=[
            pspec(t, qw, P_RQ), pspec(t, qw, P_RK), pspec(t, vw, P_RV), pspec(t, vw, P_RG),
            pspec(lc, qw, P_RQ), pspec(lc, qw, P_RK), pspec(lc, vw, P_RV), pspec(lc, vw, P_RG),
            rope_spec, rope_spec, rope_spec,
            pl.BlockSpec((1, 2, LANES, LANES), lambda b, hp: (hp, 0, 0, 0)),
            tab3, tab3, tab3, tab3,
            pl.BlockSpec((1, LANES, vw), lambda b, hp: (hp, 0, 0)),
            pl.BlockSpec((1, vw), lambda b, hp: (0, hp)),
        ],
        out_specs=[
            pl.BlockSpec((t, vw), lambda b, hp: (b, hp)),
            pl.BlockSpec((lc, vw), lambda b, hp: (b, hp)),
        ],
        out_shape=[
            jax.ShapeDtypeStruct((batch * t, RET_W), BF16),
            jax.ShapeDtypeStruct((batch * lc, RET_W), BF16),
        ],
        scratch_shapes=[
            pltpu.VMEM((t, LANES), F32),
            pltpu.VMEM((t, LANES), F32),
            pltpu.VMEM((n_lat, LANES, vw), BF16),
        ],
        compiler_params=_cparams(("parallel", "parallel"), 48),
        name="retention",
    )(p_lat, p_lat, p_lat, p_lat, p_ctx, p_ctx, p_ctx, p_ctx,
      tabs["ret_c"], tabs["ret_s1"], tabs["ret_s2"],
      tabs["dm"], tabs["kf"], tabs["kb"], tabs["qf"], tabs["qb"], tabs["cd"], gn)


def _conv_kernel(glu_ref, dw_ref, misc_ref, pw_ref, o_ref, pad_ref, *, t, rt):
    half = CONV_W // 2
    lo = 16
    zeros = jnp.zeros((lo, CONV_CH), F32)
    pad_ref[0:lo, :] = zeros
    pad_ref[lo + t:lo + t + lo, :] = zeros

    def glu(r, carry):
        rows = pl.ds(pl.multiple_of(r * rt, rt), rt)
        a = glu_ref[rows, 0:CONV_CH]
        b = glu_ref[rows, CONV_CH:2 * CONV_CH]
        pad_ref[pl.ds(pl.multiple_of(r * rt + lo, 8), rt), :] = a * _sigmoid(b)
        return carry

    lax.fori_loop(0, t // rt, glu, 0)

    def conv(r, carry):
        base = pl.multiple_of(r * rt, rt)
        win_rows = rt + 2 * lo
        cg_w = LANES
        accs = []
        for cg in range(CONV_CH // cg_w):
            cols = slice(cg * cg_w, (cg + 1) * cg_w)
            win = pad_ref[pl.ds(base, win_rows), cols]
            acc = jnp.zeros((rt, cg_w), F32) + misc_ref[0:1, cols]
            for phase in range(SUBLANES):
                shifted = win if phase == 0 else pltpu.roll(win, win_rows - phase, 0)
                for off in range(phase, lo + half + 1, SUBLANES):
                    k = off - (lo - half)
                    if 0 <= k < CONV_W:
                        a0 = off - phase
                        acc = acc + shifted[a0:a0 + rt, :] * dw_ref[k:k + 1, cols]
            accs.append(acc)
        acc = jnp.concatenate(accs, axis=1)
        dlt = acc - jnp.mean(acc, axis=-1, keepdims=True)
        var = jnp.mean(dlt * dlt, axis=-1, keepdims=True)
        y = dlt * lax.rsqrt(var + EPS) * misc_ref[1:2, :] + misc_ref[2:3, :]
        y = y * _sigmoid(y)
        o_ref[pl.ds(base, rt), :] = _dot(y.astype(BF16), pw_ref[...]).astype(BF16)
        return carry

    lax.fori_loop(0, t // rt, conv, 0, unroll=2)


def _conv(p, dw, misc, pw, batch, t):
    rt = 64
    return pl.pallas_call(
        functools.partial(_conv_kernel, t=t, rt=rt),
        grid=(batch,),
        in_specs=[
            pl.BlockSpec((t, 2 * CONV_CH), lambda b: (b, P_GLU // (2 * CONV_CH))),
            pl.BlockSpec((CONV_W, CONV_CH), lambda b: (0, 0)),
            pl.BlockSpec((3, CONV_CH), lambda b: (0, 0)),
            pl.BlockSpec((CONV_CH, CONV_CH), lambda b: (0, 0)),
        ],
        out_specs=pl.BlockSpec((t, CONV_CH), lambda b: (b, 0)),
        out_shape=jax.ShapeDtypeStruct((batch * t, CONV_CH), BF16),
        scratch_shapes=[pltpu.VMEM((t + 32, CONV_CH), F32)],
        compiler_params=_cparams(("parallel",), 40),
        name="conv",
    )(p, dw, misc, pw)


def _mla_proj_kernel(cq_ref, ckv_ref, kr_ref, nq_ref, nkv_ref, wq_ref, wkv_ref, *rest, rope):
    if rope:
        rc_ref, rs1_ref, rs2_ref, q_ref, k_ref, v_ref = rest
        rot = lambda z: _rope(z, rc_ref[...], rs1_ref[...], rs2_ref[...])
    else:
        q_ref, k_ref, v_ref = rest
        rot = lambda z: z
    qq = _dot(_rms(cq_ref[...], nq_ref[...]).astype(BF16), wq_ref[...]) * (MLA_SCALE * LOG2_E)
    kv = _dot(_rms(ckv_ref[...], nkv_ref[...]).astype(BF16), wkv_ref[...])
    kr = rot(kr_ref[...]).astype(BF16)
    hw = MLA_NOPE + LANES
    for h in range(MLA_HEADS):
        q_ref[:, h * hw:h * hw + MLA_NOPE] = qq[:, h * MLA_NOPE:(h + 1) * MLA_NOPE].astype(BF16)
        qr = qq[:, MLA_HEADS * MLA_NOPE + h * LANES:MLA_HEADS * MLA_NOPE + (h + 1) * LANES]
        q_ref[:, h * hw + MLA_NOPE:(h + 1) * hw] = rot(qr).astype(BF16)
        k_ref[:, h * hw:h * hw + MLA_NOPE] = kv[:, 2 * h * LANES:(2 * h + 1) * LANES].astype(BF16)
        k_ref[:, h * hw + MLA_NOPE:(h + 1) * hw] = kr
        v_ref[:, h * MLA_DV:(h + 1) * MLA_DV] = kv[:, (2 * h + 1) * LANES:(2 * h + 2) * LANES].astype(BF16)


def _mla_proj(p, nq, nkv, wq, wkv, rope_tabs, tm):
    r = p.shape[0]
    hw = MLA_NOPE + LANES
    rope = rope_tabs is not None
    in_specs = [
        pl.BlockSpec((tm, MLA_Q_RANK), lambda i: (i, P_CQ // MLA_Q_RANK)),
        pl.BlockSpec((tm, MLA_KV_RANK), lambda i: (i, P_CKV // MLA_KV_RANK)),
        pl.BlockSpec((tm, LANES), lambda i: (i, P_KR // LANES)),
        pl.BlockSpec((1, MLA_Q_RANK), lambda i: (0, 0)),
        pl.BlockSpec((1, MLA_KV_RANK), lambda i: (0, 0)),
        pl.BlockSpec(wq.shape, lambda i: (0, 0)),
        pl.BlockSpec(wkv.shape, lambda i: (0, 0)),
    ]
    args = [p, p, p, nq, nkv, wq, wkv]
    if rope:
        n_pos = SEQ // tm
        in_specs += [pl.BlockSpec((tm, LANES), lambda i: (i % n_pos, 0))] * 3
        args += list(rope_tabs)
    return pl.pallas_call(
        functools.partial(_mla_proj_kernel, rope=rope),
        grid=(r // tm,),
        in_specs=in_specs,
        out_specs=[
            pl.BlockSpec((tm, MLA_HEADS * hw), lambda i: (i, 0)),
            pl.BlockSpec((tm, MLA_HEADS * hw), lambda i: (i, 0)),
            pl.BlockSpec((tm, MLA_W), lambda i: (i, 0)),
        ],
        out_shape=[
            jax.ShapeDtypeStruct((r, MLA_HEADS * hw), BF16),
            jax.ShapeDtypeStruct((r, MLA_HEADS * hw), BF16),
            jax.ShapeDtypeStruct((r, MLA_W), BF16),
        ],
        compiler_params=_cparams(("parallel",), 40),
        name="mla_proj",
    )(*args)


def _attn_kernel(q_ref, kl_ref, kc_ref, vl_ref, vc_ref, o_ref, *, n_heads, sub):
    hw = MLA_NOPE + LANES
    for e in range(n_heads):
        kcols = slice(e * hw, (e + 1) * hw)
        vcols = slice(e * MLA_DV, (e + 1) * MLA_DV)
        for i in range(q_ref.shape[0] // sub):
            rows = slice(i * sub, (i + 1) * sub)
            q = q_ref[rows, kcols]
            sl = _dot_nt(q, kl_ref[:, kcols])
            sc = _dot_nt(q, kc_ref[:, kcols])
            m = jnp.maximum(jnp.max(sl, axis=-1, keepdims=True), jnp.max(sc, axis=-1, keepdims=True))
            el = jnp.exp2(sl - m)
            ec = jnp.exp2(sc - m)
            den = jnp.sum(el, axis=-1, keepdims=True) + jnp.sum(ec, axis=-1, keepdims=True)
            o = _dot(el.astype(BF16), vl_ref[:, vcols]) + _dot(ec.astype(BF16), vc_ref[:, vcols])
            o_ref[rows, vcols] = (o / den).astype(BF16)


def _attn(q, k_lat, k_ctx, v_lat, v_ctx, batch, n_heads, sub):
    hw = n_heads * (MLA_NOPE + LANES)
    vw = n_heads * MLA_DV
    return pl.pallas_call(
        functools.partial(_attn_kernel, n_heads=n_heads, sub=sub),
        grid=(batch, MLA_HEADS // n_heads),
        in_specs=[
            pl.BlockSpec((SEQ, hw), lambda b, h: (b, h)),
            pl.BlockSpec((SEQ, hw), lambda b, h: (b, h)),
            pl.BlockSpec((CTX_LEN, hw), lambda b, h: (b, h)),
            pl.BlockSpec((SEQ, vw), lambda b, h: (b, h)),
            pl.BlockSpec((CTX_LEN, vw), lambda b, h: (b, h)),
        ],
        out_specs=pl.BlockSpec((SEQ, vw), lambda b, h: (b, h)),
        out_shape=jax.ShapeDtypeStruct((batch * SEQ, MLA_W), BF16),
        compiler_params=_cparams(("parallel", "parallel"), 48),
        name="mla_attn",
    )(q, k_lat, k_ctx, v_lat, v_ctx)


def _attn_ctx_kernel(q_ref, k_ref, v_ref, o_ref):
    s = _dot_nt(q_ref[...], k_ref[...])
    e = jnp.exp2(s - jnp.max(s, axis=-1, keepdims=True))
    o = _dot(e.astype(BF16), v_ref[...])
    o_ref[...] = (o / jnp.sum(e, axis=-1, keepdims=True)).astype(BF16)


def _attn_ctx(q, k, v, batch):
    hw = MLA_NOPE + LANES
    return pl.pallas_call(
        _attn_ctx_kernel,
        grid=(batch, MLA_HEADS),
        in_specs=[
            pl.BlockSpec((CTX_LEN, hw), lambda b, h: (b, h)),
            pl.BlockSpec((CTX_LEN, hw), lambda b, h: (b, h)),
            pl.BlockSpec((CTX_LEN, MLA_DV), lambda b, h: (b, h)),
        ],
        out_specs=pl.BlockSpec((CTX_LEN, MLA_DV), lambda b, h: (b, h)),
        out_shape=jax.ShapeDtypeStruct((batch * CTX_LEN, MLA_W), BF16),
        compiler_params=_cparams(("parallel", "parallel"), 32),
        name="mla_attn_ctx",
    )(q, k, v)


def _tables():
    rows = SEQ // GRID_W
    quarter = ROPE_DIM // 4
    inv = ROPE_BASE ** (-jnp.arange(quarter, dtype=F32) / quarter)
    r_idx = jnp.repeat(jnp.arange(rows, dtype=F32), GRID_W)
    c_idx = jnp.tile(jnp.arange(GRID_W, dtype=F32), rows)
    ang = jnp.concatenate([r_idx[:, None] * inv, c_idx[:, None] * inv], axis=-1)
    cos, sin = jnp.cos(ang), jnp.sin(ang)
    z = jnp.zeros_like(cos)
    tabs = {
        "ret_c": jnp.concatenate([cos, cos, cos, cos], axis=1),
        "ret_s1": jnp.concatenate([-sin, z, -sin, z], axis=1),
        "ret_s2": jnp.concatenate([z, sin, z, sin], axis=1),
        "mla_c": jnp.concatenate([cos, cos, z, z], axis=1),
        "mla_s1": jnp.concatenate([-sin, z, z, z], axis=1),
        "mla_s2": jnp.concatenate([z, sin, z, z], axis=1),
    }
    c_len = RET_CHUNK
    log_g = jnp.log1p(-jnp.exp2(-5.0 - jnp.arange(RET_HEADS, dtype=F32)))
    i = jnp.arange(c_len, dtype=F32)
    k_scale = RET_DK ** -0.5
    lg = log_g[:, None, None]
    dm = jnp.exp(lg * jnp.abs(i[:, None] - i[None, :])) * k_scale
    tabs["dm"] = dm.reshape(RET_HEADS // 2, 2, c_len, c_len)

    def lanes_by_head(per_head_rows):
        t = jnp.repeat(per_head_rows[:, :, None], RET_DK, axis=2)
        t = t.reshape(RET_HEADS // 2, 2, c_len, RET_DK)
        return jnp.concatenate([t[:, 0], t[:, 1]], axis=-1)

    tabs["kf"] = lanes_by_head(jnp.exp(log_g[:, None] * (c_len - 1.0 - i)) * k_scale)
    tabs["kb"] = lanes_by_head(jnp.exp(log_g[:, None] * i) * k_scale)
    tabs["qf"] = lanes_by_head(jnp.exp(log_g[:, None] * (i + 1.0)))
    tabs["qb"] = lanes_by_head(jnp.exp(log_g[:, None] * (c_len - i)))
    cdec = jnp.exp(log_g * c_len).reshape(RET_HEADS // 2, 2)
    cd = jnp.repeat(cdec[:, :, None], RET_DK, axis=2).reshape(RET_HEADS // 2, 2 * RET_DK, 1)
    tabs["cd"] = jnp.broadcast_to(cd, (RET_HEADS // 2, 2 * RET_DK, 2 * RET_DV))
    return tabs


def _cast_kernel(x_ref, o_ref):
    o_ref[...] = x_ref[...].astype(BF16)


def _cast_bf16(w, tr):
    depth, r, c = w.shape
    return pl.pallas_call(
        _cast_kernel,
        grid=(depth, r // tr),
        in_specs=[pl.BlockSpec((1, tr, c), lambda l, i: (l, i, 0))],
        out_specs=pl.BlockSpec((1, tr, c), lambda l, i: (l, i, 0)),
        out_shape=jax.ShapeDtypeStruct(w.shape, BF16),
        compiler_params=_cparams(("parallel", "parallel"), 48),
        name="cast",
    )(w)


_W_IN_SRC = ((2304, 1024), (3328, 512), (768, 768), (1536, 768), (0, 384), (384, 384), (3840, 256),
             (4096, 64))


def _prep_w_in_kernel(w_ref, o_ref):
    dst = 0
    for src, width in _W_IN_SRC:
        o_ref[0, dst:dst + width, :] = w_ref[0, src:src + width, :].astype(BF16)
        dst += width
    o_ref[0, dst:, :] = jnp.zeros((P_W - dst, o_ref.shape[2]), BF16)


def _prep_w_in(w):
    depth, d, n = w.shape
    wt = jnp.swapaxes(w, 1, 2)
    tk = 512
    return pl.pallas_call(
        _prep_w_in_kernel,
        grid=(depth, d // tk),
        in_specs=[pl.BlockSpec((1, n, tk), lambda l, i: (l, 0, i))],
        out_specs=pl.BlockSpec((1, P_W, tk), lambda l, i: (l, 0, i)),
        out_shape=jax.ShapeDtypeStruct((depth, P_W, d), BF16),
        compiler_params=_cparams(("parallel", "parallel"), 48),
        name="prep_w_in",
    )(wt)


def _prep_w_uq(w):
    w = w.reshape(MLA_Q_RANK, MLA_HEADS, MLA_NOPE + MLA_ROPE)
    nope = w[:, :, :MLA_NOPE].reshape(MLA_Q_RANK, MLA_HEADS * MLA_NOPE)
    rope = jnp.pad(w[:, :, MLA_NOPE:], ((0, 0), (0, 0), (0, LANES - MLA_ROPE)))
    return jnp.concatenate([nope, rope.reshape(MLA_Q_RANK, MLA_HEADS * LANES)], axis=1).astype(BF16)


def _lat_mod_row(i, tm):
    return (i * tm) // SEQ


def _ctx_mod_row(i, tm):
    return CTX_MOD_ROW


def kernel(x, c, ctx, c_ctx, w_ada, b_ada, norm_g, w_ffa_in, w_ffa_out, w_ffb_in, w_ffb_out,
           w_in, w_out, ret_gn, cv_dw, cv_dw_b, cv_ln_g, cv_ln_b, cv_pw,
           mla_q_norm, w_uq, mla_kv_norm, w_ukv):
    batch, seq, d = x.shape
    assert (seq, d, ctx.shape[1]) == (SEQ, D_MODEL, CTX_LEN)
    xl = x.reshape(batch * seq, d)
    xc = ctx.reshape(batch * CTX_LEN, d)
    s_in = jnp.concatenate([c, c_ctx[None, :], jnp.zeros((ROWS_PAD - batch - 1, d), F32)], axis=0)
    mod_all = _ada(s_in, w_ada, b_ada).reshape(DEPTH, ROWS_PAD, N_MOD, d)
    tabs = _tables()
    mla_rope = (tabs["mla_c"], tabs["mla_s1"], tabs["mla_s2"])

    wa_in, wa_out, wb_in, wb_out = w_ffa_in, w_ffa_out, w_ffb_in, w_ffb_out
    wi = _prep_w_in(w_in)
    wo = _cast_bf16(w_out, 1024)

    ffn_lat = dict(mod_row=_lat_mod_row, tm=1024, tf=512)
    ffn_ctx = dict(mod_row=_ctx_mod_row, tm=1024, tf=512)
    for l in range(DEPTH):
        last = l == DEPTH - 1
        mod = mod_all[l]
        ng = norm_g[l]
        wq = _prep_w_uq(w_uq[l])
        wkv = w_ukv[l].astype(BF16)
        nq, nkv = mla_q_norm[l][None, :], mla_kv_norm[l][None, :]
        gn = ret_gn[l][None, :]
        conv_misc = jnp.stack([cv_dw_b[l], cv_ln_g[l], cv_ln_b[l]], axis=0)
        pw = cv_pw[l].astype(BF16)

        xl = _ffn(xl, mod, ng[0:2], wa_in, wa_out, l, 0, **ffn_lat)
        xc = _ffn(xc, mod, ng[0:2], wa_in, wa_out, l, 0, **ffn_ctx)

        p_lat = _inproj(xl, mod, ng[2:3], wi, l, _lat_mod_row, 1024, 1408)
        p_ctx = _inproj(xc, mod, ng[2:3], wi, l, _ctx_mod_row, 1024, 1408)

        y_ret_l, y_ret_c = _retention(p_lat, p_ctx, tabs, gn, batch)
        y_conv_l = _conv(p_lat, cv_dw[l], conv_misc, pw, batch, SEQ)
        q_l, k_l, v_l = _mla_proj(p_lat, nq, nkv, wq, wkv, mla_rope, 512)
        q_c, k_c, v_c = _mla_proj(p_ctx, nq, nkv, wq, wkv, None, 512)
        y_mla_l = _attn(q_l, k_l, k_c, v_l, v_c, batch, 2, 512)

        xl = _outproj(xl, y_ret_l, y_conv_l, y_mla_l, mod, ng[3:4], wo, l, _lat_mod_row, 1024)
        xl = _ffn(xl, mod, ng[4:6], wb_in, wb_out, l, 6, **ffn_lat)
        if not last:
            y_conv_c = _conv(p_ctx, cv_dw[l], conv_misc, pw, batch, CTX_LEN)
            y_mla_c = _attn_ctx(q_c, k_c, v_c, batch)
            xc = _outproj(xc, y_ret_c, y_conv_c, y_mla_c, mod, ng[3:4], wo, l, _ctx_mod_row, 1024)
            xc = _ffn(xc, mod, ng[4:6], wb_in, wb_out, l, 6, **ffn_ctx)
    return xl.reshape(batch, seq, d)
```

```python
import functools

import jax
import jax.numpy as jnp
from jax import lax
from jax.experimental import pallas as pl
from jax.experimental.pallas import tpu as pltpu

F32 = jnp.float32
BF16 = jnp.bfloat16

D_MODEL = 2048
SEQ = 2048
DEPTH = 2
GRID_W = 64
CTX_LEN = 256
D_FF = 5632
FFN_RES = 0.5
N_MOD = 9
ROPE_DIM = 64
ROPE_BASE = 10000.0
EPS = 1e-6
RET_HEADS = 6
RET_DK = 64
RET_DV = 128
RET_CHUNK = 128
CONV_CH = 512
CONV_W = 31
MLA_HEADS = 6
MLA_Q_RANK = 512
MLA_KV_RANK = 256
MLA_NOPE = 128
MLA_ROPE = 64
MLA_DV = 128
MLA_SCALE = (MLA_NOPE + MLA_ROPE) ** -0.5
LOG2_E = 1.4426950408889634
RET_W = RET_HEADS * RET_DV
MLA_W = MLA_HEADS * MLA_DV

LANES = 128
MIB = 1024 * 1024
ROW_CHUNK = 128
RET_UNROLL = 8
SUBLANES = 8

P_GLU, P_CQ, P_RV, P_RG, P_RQ, P_RK, P_CKV, P_KR = 0, 1024, 1536, 2304, 3072, 3456, 3840, 4096
P_W = 4224
CTX_MOD_ROW = 4
ROWS_PAD = 8


def _cparams(sem, vmem_mib):
    return pltpu.CompilerParams(dimension_semantics=sem, vmem_limit_bytes=int(vmem_mib * MIB))


def _sigmoid(x):
    return 1.0 / (1.0 + jnp.exp(-x))


def _rms(x, g):
    ms = jnp.mean(x * x, axis=-1, keepdims=True)
    return x * lax.rsqrt(ms + EPS) * g


def _dot(a, b):
    return jnp.dot(a, b, preferred_element_type=F32)


def _half_rows(tm):
    return (slice(0, tm // 2), slice(tm // 2, tm))


def _prenorm_modulate(x_ref, h_ref, gain_row, shift_row, rows):
    for r0 in range(rows.start, rows.stop, ROW_CHUNK):
        x = x_ref[r0:r0 + ROW_CHUNK, :]
        r = lax.rsqrt(jnp.mean(x * x, axis=-1, keepdims=True) + EPS)
        h_ref[r0:r0 + ROW_CHUNK, :] = (x * r * gain_row + shift_row).astype(BF16)


def _postnorm_residual(y_ref, x_ref, gain_row, rows, on_chunk_done=None):
    for r0 in range(rows.start, rows.stop, ROW_CHUNK):
        y = y_ref[r0:r0 + ROW_CHUNK, :]
        r = lax.rsqrt(jnp.mean(y * y, axis=-1, keepdims=True) + EPS)
        y_ref[r0:r0 + ROW_CHUNK, :] = x_ref[r0:r0 + ROW_CHUNK, :] + y * r * gain_row
        if on_chunk_done is not None:
            on_chunk_done(r0 // ROW_CHUNK)


def _dot_nt(a, b):
    return lax.dot_general(a, b, (((1,), (1,)), ((), ())), preferred_element_type=F32)


def _dot_tn(a, b):
    return lax.dot_general(a, b, (((0,), (0,)), ((), ())), preferred_element_type=F32)


def _rope(x, c, s1, s2):
    return x * c + pltpu.roll(x, 96, 1) * s1 + pltpu.roll(x, 32, 1) * s2


def _ada_kernel(s_ref, w_ref, b_ref, o_ref):
    s = s_ref[...]
    s = s * _sigmoid(s)
    o_ref[0] = _dot(s.astype(BF16), w_ref[0].astype(BF16)) + b_ref[0]


def _ada(s_in, w_ada, b_ada):
    depth, d, n = w_ada.shape
    tn = 1024
    return pl.pallas_call(
        _ada_kernel,
        grid=(depth, n // tn),
        in_specs=[
            pl.BlockSpec((ROWS_PAD, d), lambda l, j: (0, 0)),
            pl.BlockSpec((1, d, tn), lambda l, j: (l, 0, j)),
            pl.BlockSpec((1, 1, tn), lambda l, j: (l, 0, j)),
        ],
        out_specs=pl.BlockSpec((1, ROWS_PAD, tn), lambda l, j: (l, 0, j)),
        out_shape=jax.ShapeDtypeStruct((depth, ROWS_PAD, n), F32),
        compiler_params=_cparams(("parallel", "parallel"), 40),
        name="ada",
    )(s_in, w_ada, b_ada.reshape(depth, 1, n))


def _ffn_kernel(x_ref, mod_ref, g_ref, wa_ref, wu_ref, wo_ref, o_hbm, acc_ref, h_ref, sem, *, k0, nj):
    i = pl.program_id(0)
    j = pl.program_id(1)
    tm = acc_ref.shape[0]
    halves = _half_rows(tm)

    def out_copy(c):
        src = acc_ref.at[pl.ds(c * ROW_CHUNK, ROW_CHUNK), :]
        dst = o_hbm.at[pl.ds(pl.multiple_of(i * tm + c * ROW_CHUNK, ROW_CHUNK), ROW_CHUNK), :]
        return pltpu.make_async_copy(src, dst, sem.at[c])

    def chunk_update(rows, wa, wu, wo):
        h = h_ref[rows, :]
        a = _dot(h, wa)
        u = _dot(h, wu)
        act = (a * _sigmoid(a) * u).astype(BF16)
        return _dot(act, wo)

    def weights():
        return wa_ref[...].astype(BF16), wu_ref[...].astype(BF16), wo_ref[...].astype(BF16)

    @pl.when(j == 0)
    def _():
        w = weights()
        gain = g_ref[0:1, :] * (1.0 + mod_ref[0, k0 + 1:k0 + 2, :])
        shift = mod_ref[0, k0:k0 + 1, :]
        for rows in halves:
            _prenorm_modulate(x_ref, h_ref, gain, shift, rows)
            acc_ref[rows, :] = chunk_update(rows, *w)

    @pl.when((j > 0) & (j < nj - 1))
    def _():
        acc_ref[...] += chunk_update(slice(0, tm), *weights())

    @pl.when(j == nj - 1)
    def _():
        w = weights()
        gain = g_ref[1:2, :] * (FFN_RES * mod_ref[0, k0 + 2:k0 + 3, :])
        for rows in halves:
            acc_ref[rows, :] += chunk_update(rows, *w)
            _postnorm_residual(acc_ref, x_ref, gain, rows, lambda c: out_copy(c).start())
        for c in range(tm // ROW_CHUNK):
            out_copy(c).wait()


def _ffn(x, mod, g2, w_in, w_out, layer, k0, mod_row, tm, tf):
    r, d = x.shape
    nj = D_FF // tf
    assert nj >= 2 and tm % (2 * ROW_CHUNK) == 0
    return pl.pallas_call(
        functools.partial(_ffn_kernel, k0=k0, nj=nj),
        grid=(r // tm, nj),
        in_specs=[
            pl.BlockSpec((tm, d), lambda i, j: (i, 0)),
            pl.BlockSpec((1, N_MOD, d), lambda i, j: (mod_row(i, tm), 0, 0)),
            pl.BlockSpec((2, d), lambda i, j: (0, 0)),
            pl.BlockSpec((None, d, tf), lambda i, j: (layer, 0, j)),
            pl.BlockSpec((None, d, tf), lambda i, j: (layer, 0, nj + j)),
            pl.BlockSpec((None, tf, d), lambda i, j: (layer, j, 0)),
        ],
        out_specs=pl.BlockSpec(memory_space=pl.ANY),
        out_shape=jax.ShapeDtypeStruct((r, d), F32),
        scratch_shapes=[
            pltpu.VMEM((tm, d), F32),
            pltpu.VMEM((tm, d), BF16),
            pltpu.SemaphoreType.DMA((tm // ROW_CHUNK,)),
        ],
        compiler_params=_cparams(("arbitrary", "arbitrary"), 60),
        name="ffn",
    )(x, mod, g2, w_in, w_in, w_out)


def _inproj_kernel(x_ref, mod_ref, g_ref, w_ref, o_ref, h_ref):
    j = pl.program_id(1)

    @pl.when(j == 0)
    def _():
        gain = g_ref[...] * (1.0 + mod_ref[0, 4:5, :])
        shift = mod_ref[0, 3:4, :]
        for rows in _half_rows(x_ref.shape[0]):
            _prenorm_modulate(x_ref, h_ref, gain, shift, rows)
            o_ref[rows, :] = _dot_nt(h_ref[rows, :], w_ref[...])

    @pl.when(j > 0)
    def _():
        o_ref[...] = _dot_nt(h_ref[...], w_ref[...])


def _inproj(x, mod, g, w, layer, mod_row, tm, tn):
    r, d = x.shape
    return pl.pallas_call(
        _inproj_kernel,
        grid=(r // tm, P_W // tn),
        in_specs=[
            pl.BlockSpec((tm, d), lambda i, j: (i, 0)),
            pl.BlockSpec((1, N_MOD, d), lambda i, j: (mod_row(i, tm), 0, 0)),
            pl.BlockSpec((1, d), lambda i, j: (0, 0)),
            pl.BlockSpec((None, tn, d), lambda i, j: (layer, j, 0)),
        ],
        out_specs=pl.BlockSpec((tm, tn), lambda i, j: (i, j)),
        out_shape=jax.ShapeDtypeStruct((r, P_W), F32),
        scratch_shapes=[pltpu.VMEM((tm, d), BF16)],
        compiler_params=_cparams(("parallel", "arbitrary"), 56),
        name="inproj",
    )(x, mod, g, w)


def _outproj_kernel(x_ref, yr_ref, yc_ref, ym_ref, mod_ref, g_ref, w_ref, o_ref):
    gain = g_ref[...] * mod_ref[0, 5:6, :]
    for rows in _half_rows(x_ref.shape[0]):
        y = _dot(yr_ref[rows, :], w_ref[0:RET_W, :])
        y += _dot(yc_ref[rows, :], w_ref[RET_W:RET_W + CONV_CH, :])
        y += _dot(ym_ref[rows, :], w_ref[RET_W + CONV_CH:, :])
        o_ref[rows, :] = y
        _postnorm_residual(o_ref, x_ref, gain, rows)


def _outproj(x, y_ret, y_conv, y_mla, mod, g, w, layer, mod_row, tm):
    r, d = x.shape
    return pl.pallas_call(
        _outproj_kernel,
        grid=(r // tm,),
        in_specs=[
            pl.BlockSpec((tm, d), lambda i: (i, 0)),
            pl.BlockSpec((tm, RET_W), lambda i: (i, 0)),
            pl.BlockSpec((tm, CONV_CH), lambda i: (i, 0)),
            pl.BlockSpec((tm, MLA_W), lambda i: (i, 0)),
            pl.BlockSpec((1, N_MOD, d), lambda i: (mod_row(i, tm), 0, 0)),
            pl.BlockSpec((1, d), lambda i: (0, 0)),
            pl.BlockSpec((None, d, d), lambda i: (layer, 0, 0), pipeline_mode=pl.Buffered(1)),
        ],
        out_specs=pl.BlockSpec((tm, d), lambda i: (i, 0)),
        out_shape=jax.ShapeDtypeStruct((r, d), F32),
        compiler_params=_cparams(("parallel",), 60),
        name="outproj",
    )(x, y_ret, y_conv, y_mla, mod, g, w)


def _ret_kernel(ql_ref, kl_ref, vl_ref, gl_ref, qc_ref, kc_ref, vc_ref, gc_ref,
                rc_ref, rs1_ref, rs2_ref, dm_ref, kf_ref, kb_ref, qf_ref, qb_ref, cd_ref, gn_ref,
                yl_ref, yc_ref, qr_ref, kr_ref, sf_ref, *, n_lat, n_ctx):
    c_len = RET_CHUNK
    lane = lax.broadcasted_iota(jnp.int32, (1, LANES), 1)
    head_masks = (lane < RET_DK, lane >= RET_DK)
    row = lax.broadcasted_iota(jnp.int32, (LANES, 2 * RET_DV), 0)
    col = lax.broadcasted_iota(jnp.int32, (LANES, 2 * RET_DV), 1)
    block_diag = (row < RET_DK) == (col < RET_DV)
    kf, kb, qf, qb, cd = kf_ref[0], kb_ref[0], qf_ref[0], qb_ref[0], cd_ref[0]
    gn = gn_ref[...]
    zero_state = jnp.zeros((LANES, 2 * RET_DV), F32)

    def kv_state(k, v16, dec):
        return jnp.where(block_diag, _dot_tn((k * dec).astype(BF16), v16), 0.0)

    def chunk_out(q, k, v16, g, sf16, sb16):
        k16 = k.astype(BF16)
        parts = []
        for e in range(2):
            q16 = jnp.where(head_masks[e], q, 0.0).astype(BF16)
            a = _dot_nt(q16, k16) * dm_ref[0, e]
            parts.append(_dot(a.astype(BF16), v16[:, e * RET_DV:(e + 1) * RET_DV]))
        o = jnp.concatenate(parts, axis=1)
        o = o + _dot((q * qf).astype(BF16), sf16) + _dot((q * qb).astype(BF16), sb16)
        normed = []
        for e in range(2):
            oe = o[:, e * RET_DV:(e + 1) * RET_DV]
            dlt = oe - jnp.mean(oe, axis=-1, keepdims=True)
            var = jnp.mean(dlt * dlt, axis=-1, keepdims=True)
            normed.append(dlt * lax.rsqrt(var + EPS))
        on = jnp.concatenate(normed, axis=1) * gn
        return ((g * _sigmoid(g)) * on).astype(BF16)

    cq = [qc_ref[c * c_len:(c + 1) * c_len, :] for c in range(n_ctx)]
    ck = [kc_ref[c * c_len:(c + 1) * c_len, :] for c in range(n_ctx)]
    cv = [vc_ref[c * c_len:(c + 1) * c_len, :].astype(BF16) for c in range(n_ctx)]
    kvf = [kv_state(ck[c], cv[c], kf) for c in range(n_ctx)]
    kvb = [kv_state(ck[c], cv[c], kb) for c in range(n_ctx)]
    sf_list = [zero_state]
    for c in range(n_ctx):
        sf_list.append(cd * sf_list[c] + kvf[c])
    sb_list = [zero_state] * (n_ctx + 1)
    for c in range(n_ctx - 1, -1, -1):
        sb_list[c] = cd * sb_list[c + 1] + kvb[c]
    for c in range(n_ctx):
        yc_ref[c * c_len:(c + 1) * c_len, :] = chunk_out(
            cq[c], ck[c], cv[c], gc_ref[c * c_len:(c + 1) * c_len, :],
            sf_list[c].astype(BF16), sb_list[c + 1].astype(BF16))
    s0f, s0b = sf_list[n_ctx], sb_list[0]

    def fwd(c, sf):
        r0 = pl.multiple_of(c * c_len, c_len)
        rows = pl.ds(r0, c_len)
        rc, rs1, rs2 = rc_ref[rows, :], rs1_ref[rows, :], rs2_ref[rows, :]
        q = _rope(ql_ref[rows, :], rc, rs1, rs2)
        k = _rope(kl_ref[rows, :], rc, rs1, rs2)
        qr_ref[rows, :] = q
        kr_ref[rows, :] = k
        sf_ref[c] = sf.astype(BF16)
        return cd * sf + kv_state(k, vl_ref[rows, :].astype(BF16), kf)

    lax.fori_loop(0, n_lat, fwd, s0f, unroll=RET_UNROLL)

    def bwd(i, sb):
        c = n_lat - 1 - i
        r0 = pl.multiple_of(c * c_len, c_len)
        rows = pl.ds(r0, c_len)
        q, k = qr_ref[rows, :], kr_ref[rows, :]
        v16 = vl_ref[rows, :].astype(BF16)
        yl_ref[rows, :] = chunk_out(q, k, v16, gl_ref[rows, :], sf_ref[c], sb.astype(BF16))
        return cd * sb + kv_state(k, v16, kb)

    lax.fori_loop(0, n_lat, bwd, s0b, unroll=RET_UNROLL)


def _retention(p_lat, p_ctx, tabs, gn, batch):
    t, lc = SEQ, CTX_LEN
    n_pairs = RET_HEADS // 2
    n_lat, n_ctx = t // RET_CHUNK, lc // RET_CHUNK
    qw, vw = 2 * RET_DK, 2 * RET_DV

    def pspec(rows, width, col0):
        return pl.BlockSpec((rows, width), lambda b, hp: (b, col0 // width + hp))

    tab3 = pl.BlockSpec((1, LANES, LANES), lambda b, hp: (hp, 0, 0))
    rope_spec = pl.BlockSpec((t, LANES), lambda b, hp: (0, 0))
    return pl.pallas_call(
        functools.partial(_ret_kernel, n_lat=n_lat, n_ctx=n_ctx),
        grid=(batch, n_pairs),
        in_specs=[
            pspec(t, qw, P_RQ), pspec(t, qw, P_RK), pspec(t, vw, P_RV), pspec(t, vw, P_RG),
            pspec(lc, qw, P_RQ), pspec(lc, qw, P_RK), pspec(lc, vw, P_RV), pspec(lc, vw, P_RG),
            rope_spec, rope_spec, rope_spec,
            pl.BlockSpec((1, 2, LANES, LANES), lambda b, hp: (hp, 0, 0, 0)),
            tab3, tab3, tab3, tab3,
            pl.BlockSpec((1, LANES, vw), lambda b, hp: (hp, 0, 0)),
            pl.BlockSpec((1, vw), lambda b, hp: (0, hp)),
        ],
        out_specs=[
            pl.BlockSpec((t, vw), lambda b, hp: (b, hp)),
            pl.BlockSpec((lc, vw), lambda b, hp: (b, hp)),
        ],
        out_shape=[
            jax.ShapeDtypeStruct((batch * t, RET_W), BF16),
            jax.ShapeDtypeStruct((batch * lc, RET_W), BF16),
        ],
        scratch_shapes=[
            pltpu.VMEM((t, LANES), F32),
            pltpu.VMEM((t, LANES), F32),
            pltpu.VMEM((n_lat, LANES, vw), BF16),
        ],
        compiler_params=_cparams(("parallel", "parallel"), 48),
        name="retention",
    )(p_lat, p_lat, p_lat, p_lat, p_ctx, p_ctx, p_ctx, p_ctx,
      tabs["ret_c"], tabs["ret_s1"], tabs["ret_s2"],
      tabs["dm"], tabs["kf"], tabs["kb"], tabs["qf"], tabs["qb"], tabs["cd"], gn)


def _conv_kernel(glu_ref, dw_ref, misc_ref, pw_ref, o_ref, pad_ref, *, t, rt):
    half = CONV_W // 2
    lo = 16
    zeros = jnp.zeros((lo, CONV_CH), F32)
    pad_ref[0:lo, :] = zeros
    pad_ref[lo + t:lo + t + lo, :] = zeros

    def glu(r, carry):
        rows = pl.ds(pl.multiple_of(r * rt, rt), rt)
        a = glu_ref[rows, 0:CONV_CH]
        b = glu_ref[rows, CONV_CH:2 * CONV_CH]
        pad_ref[pl.ds(pl.multiple_of(r * rt + lo, 8), rt), :] = a * _sigmoid(b)
        return carry

    lax.fori_loop(0, t // rt, glu, 0)

    def conv(r, carry):
        base = pl.multiple_of(r * rt, rt)
        win_rows = rt + 2 * lo
        cg_w = LANES
        accs = []
        for cg in range(CONV_CH // cg_w):
            cols = slice(cg * cg_w, (cg + 1) * cg_w)
            win = pad_ref[pl.ds(base, win_rows), cols]
            acc = jnp.zeros((rt, cg_w), F32) + misc_ref[0:1, cols]
            for phase in range(SUBLANES):
                shifted = win if phase == 0 else pltpu.roll(win, win_rows - phase, 0)
                for off in range(phase, lo + half + 1, SUBLANES):
                    k = off - (lo - half)
                    if 0 <= k < CONV_W:
                        a0 = off - phase
                        acc = acc + shifted[a0:a0 + rt, :] * dw_ref[k:k + 1, cols]
            accs.append(acc)
        acc = jnp.concatenate(accs, axis=1)
        dlt = acc - jnp.mean(acc, axis=-1, keepdims=True)
        var = jnp.mean(dlt * dlt, axis=-1, keepdims=True)
        y = dlt * lax.rsqrt(var + EPS) * misc_ref[1:2, :] + misc_ref[2:3, :]
        y = y * _sigmoid(y)
        o_ref[pl.ds(base, rt), :] = _dot(y.astype(BF16), pw_ref[...]).astype(BF16)
        return carry

    lax.fori_loop(0, t // rt, conv, 0, unroll=2)


def _conv(p, dw, misc, pw, batch, t):
    rt = 64
    return pl.pallas_call(
        functools.partial(_conv_kernel, t=t, rt=rt),
        grid=(batch,),
        in_specs=[
            pl.BlockSpec((t, 2 * CONV_CH), lambda b: (b, P_GLU // (2 * CONV_CH))),
            pl.BlockSpec((CONV_W, CONV_CH), lambda b: (0, 0)),
            pl.BlockSpec((3, CONV_CH), lambda b: (0, 0)),
            pl.BlockSpec((CONV_CH, CONV_CH), lambda b: (0, 0)),
        ],
        out_specs=pl.BlockSpec((t, CONV_CH), lambda b: (b, 0)),
        out_shape=jax.ShapeDtypeStruct((batch * t, CONV_CH), BF16),
        scratch_shapes=[pltpu.VMEM((t + 32, CONV_CH), F32)],
        compiler_params=_cparams(("parallel",), 40),
        name="conv",
    )(p, dw, misc, pw)


def _mla_proj_kernel(cq_ref, ckv_ref, kr_ref, nq_ref, nkv_ref, wq_ref, wkv_ref, *rest, rope):
    if rope:
        rc_ref, rs1_ref, rs2_ref, q_ref, k_ref, v_ref = rest
        rot = lambda z: _rope(z, rc_ref[...], rs1_ref[...], rs2_ref[...])
    else:
        q_ref, k_ref, v_ref = rest
        rot = lambda z: z
    qq = _dot(_rms(cq_ref[...], nq_ref[...]).astype(BF16), wq_ref[...]) * (MLA_SCALE * LOG2_E)
    kv = _dot(_rms(ckv_ref[...], nkv_ref[...]).astype(BF16), wkv_ref[...])
    kr = rot(kr_ref[...]).astype(BF16)
    hw = MLA_NOPE + LANES
    for h in range(MLA_HEADS):
        q_ref[:, h * hw:h * hw + MLA_NOPE] = qq[:, h * MLA_NOPE:(h + 1) * MLA_NOPE].astype(BF16)
        qr = qq[:, MLA_HEADS * MLA_NOPE + h * LANES:MLA_HEADS * MLA_NOPE + (h + 1) * LANES]
        q_ref[:, h * hw + MLA_NOPE:(h + 1) * hw] = rot(qr).astype(BF16)
        k_ref[:, h * hw:h * hw + MLA_NOPE] = kv[:, 2 * h * LANES:(2 * h + 1) * LANES].astype(BF16)
        k_ref[:, h * hw + MLA_NOPE:(h + 1) * hw] = kr
        v_ref[:, h * MLA_DV:(h + 1) * MLA_DV] = kv[:, (2 * h + 1) * LANES:(2 * h + 2) * LANES].astype(BF16)


def _mla_proj(p, nq, nkv, wq, wkv, rope_tabs, tm):
    r = p.shape[0]
    hw = MLA_NOPE + LANES
    rope = rope_tabs is not None
    in_specs = [
        pl.BlockSpec((tm, MLA_Q_RANK), lambda i: (i, P_CQ // MLA_Q_RANK)),
        pl.BlockSpec((tm, MLA_KV_RANK), lambda i: (i, P_CKV // MLA_KV_RANK)),
        pl.BlockSpec((tm, LANES), lambda i: (i, P_KR // LANES)),
        pl.BlockSpec((1, MLA_Q_RANK), lambda i: (0, 0)),
        pl.BlockSpec((1, MLA_KV_RANK), lambda i: (0, 0)),
        pl.BlockSpec(wq.shape, lambda i: (0, 0)),
        pl.BlockSpec(wkv.shape, lambda i: (0, 0)),
    ]
    args = [p, p, p, nq, nkv, wq, wkv]
    if rope:
        n_pos = SEQ // tm
        in_specs += [pl.BlockSpec((tm, LANES), lambda i: (i % n_pos, 0))] * 3
        args += list(rope_tabs)
    return pl.pallas_call(
        functools.partial(_mla_proj_kernel, rope=rope),
        grid=(r // tm,),
        in_specs=in_specs,
        out_specs=[
            pl.BlockSpec((tm, MLA_HEADS * hw), lambda i: (i, 0)),
            pl.BlockSpec((tm, MLA_HEADS * hw), lambda i: (i, 0)),
            pl.BlockSpec((tm, MLA_W), lambda i: (i, 0)),
        ],
        out_shape=[
            jax.ShapeDtypeStruct((r, MLA_HEADS * hw), BF16),
            jax.ShapeDtypeStruct((r, MLA_HEADS * hw), BF16),
            jax.ShapeDtypeStruct((r, MLA_W), BF16),
        ],
        compiler_params=_cparams(("parallel",), 40),
        name="mla_proj",
    )(*args)


def _attn_kernel(q_ref, kl_ref, kc_ref, vl_ref, vc_ref, o_ref, *, n_heads, sub):
    hw = MLA_NOPE + LANES
    for e in range(n_heads):
        kcols = slice(e * hw, (e + 1) * hw)
        vcols = slice(e * MLA_DV, (e + 1) * MLA_DV)
        for i in range(q_ref.shape[0] // sub):
            rows = slice(i * sub, (i + 1) * sub)
            q = q_ref[rows, kcols]
            sl = _dot_nt(q, kl_ref[:, kcols])
            sc = _dot_nt(q, kc_ref[:, kcols])
            m = jnp.maximum(jnp.max(sl, axis=-1, keepdims=True), jnp.max(sc, axis=-1, keepdims=True))
            el = jnp.exp2(sl - m)
            ec = jnp.exp2(sc - m)
            den = jnp.sum(el, axis=-1, keepdims=True) + jnp.sum(ec, axis=-1, keepdims=True)
            o = _dot(el.astype(BF16), vl_ref[:, vcols]) + _dot(ec.astype(BF16), vc_ref[:, vcols])
            o_ref[rows, vcols] = (o / den).astype(BF16)


def _attn(q, k_lat, k_ctx, v_lat, v_ctx, batch, n_heads, sub):
    hw = n_heads * (MLA_NOPE + LANES)
    vw = n_heads * MLA_DV
    return pl.pallas_call(
        functools.partial(_attn_kernel, n_heads=n_heads, sub=sub),
        grid=(batch, MLA_HEADS // n_heads),
        in_specs=[
            pl.BlockSpec((SEQ, hw), lambda b, h: (b, h)),
            pl.BlockSpec((SEQ, hw), lambda b, h: (b, h)),
            pl.BlockSpec((CTX_LEN, hw), lambda b, h: (b, h)),
            pl.BlockSpec((SEQ, vw), lambda b, h: (b, h)),
            pl.BlockSpec((CTX_LEN, vw), lambda b, h: (b, h)),
        ],
        out_specs=pl.BlockSpec((SEQ, vw), lambda b, h: (b, h)),
        out_shape=jax.ShapeDtypeStruct((batch * SEQ, MLA_W), BF16),
        compiler_params=_cparams(("parallel", "parallel"), 48),
        name="mla_attn",
    )(q, k_lat, k_ctx, v_lat, v_ctx)


def _attn_ctx_kernel(q_ref, k_ref, v_ref, o_ref):
    s = _dot_nt(q_ref[...], k_ref[...])
    e = jnp.exp2(s - jnp.max(s, axis=-1, keepdims=True))
    o = _dot(e.astype(BF16), v_ref[...])
    o_ref[...] = (o / jnp.sum(e, axis=-1, keepdims=True)).astype(BF16)


def _attn_ctx(q, k, v, batch):
    hw = MLA_NOPE + LANES
    return pl.pallas_call(
        _attn_ctx_kernel,
        grid=(batch, MLA_HEADS),
        in_specs=[
            pl.BlockSpec((CTX_LEN, hw), lambda b, h: (b, h)),
            pl.BlockSpec((CTX_LEN, hw), lambda b, h: (b, h)),
            pl.BlockSpec((CTX_LEN, MLA_DV), lambda b, h: (b, h)),
        ],
        out_specs=pl.BlockSpec((CTX_LEN, MLA_DV), lambda b, h: (b, h)),
        out_shape=jax.ShapeDtypeStruct((batch * CTX_LEN, MLA_W), BF16),
        compiler_params=_cparams(("parallel", "parallel"), 32),
        name="mla_attn_ctx",
    )(q, k, v)


def _tables():
    rows = SEQ // GRID_W
    quarter = ROPE_DIM // 4
    inv = ROPE_BASE ** (-jnp.arange(quarter, dtype=F32) / quarter)
    r_idx = jnp.repeat(jnp.arange(rows, dtype=F32), GRID_W)
    c_idx = jnp.tile(jnp.arange(GRID_W, dtype=F32), rows)
    ang = jnp.concatenate([r_idx[:, None] * inv, c_idx[:, None] * inv], axis=-1)
    cos, sin = jnp.cos(ang), jnp.sin(ang)
    z = jnp.zeros_like(cos)
    tabs = {
        "ret_c": jnp.concatenate([cos, cos, cos, cos], axis=1),
        "ret_s1": jnp.concatenate([-sin, z, -sin, z], axis=1),
        "ret_s2": jnp.concatenate([z, sin, z, sin], axis=1),
        "mla_c": jnp.concatenate([cos, cos, z, z], axis=1),
        "mla_s1": jnp.concatenate([-sin, z, z, z], axis=1),
        "mla_s2": jnp.concatenate([z, sin, z, z], axis=1),
    }
    c_len = RET_CHUNK
    log_g = jnp.log1p(-jnp.exp2(-5.0 - jnp.arange(RET_HEADS, dtype=F32)))
    i = jnp.arange(c_len, dtype=F32)
    k_scale = RET_DK ** -0.5
    lg = log_g[:, None, None]
    dm = jnp.exp(lg * jnp.abs(i[:, None] - i[None, :])) * k_scale
    tabs["dm"] = dm.reshape(RET_HEADS // 2, 2, c_len, c_len)

    def lanes_by_head(per_head_rows):
        t = jnp.repeat(per_head_rows[:, :, None], RET_DK, axis=2)
        t = t.reshape(RET_HEADS // 2, 2, c_len, RET_DK)
        return jnp.concatenate([t[:, 0], t[:, 1]], axis=-1)

    tabs["kf"] = lanes_by_head(jnp.exp(log_g[:, None] * (c_len - 1.0 - i)) * k_scale)
    tabs["kb"] = lanes_by_head(jnp.exp(log_g[:, None] * i) * k_scale)
    tabs["qf"] = lanes_by_head(jnp.exp(log_g[:, None] * (i + 1.0)))
    tabs["qb"] = lanes_by_head(jnp.exp(log_g[:, None] * (c_len - i)))
    cdec = jnp.exp(log_g * c_len).reshape(RET_HEADS // 2, 2)
    cd = jnp.repeat(cdec[:, :, None], RET_DK, axis=2).reshape(RET_HEADS // 2, 2 * RET_DK, 1)
    tabs["cd"] = jnp.broadcast_to(cd, (RET_HEADS // 2, 2 * RET_DK, 2 * RET_DV))
    return tabs


def _cast_kernel(x_ref, o_ref):
    o_ref[...] = x_ref[...].astype(BF16)


def _cast_bf16(w, tr):
    depth, r, c = w.shape
    return pl.pallas_call(
        _cast_kernel,
        grid=(depth, r // tr),
        in_specs=[pl.BlockSpec((1, tr, c), lambda l, i: (l, i, 0))],
        out_specs=pl.BlockSpec((1, tr, c), lambda l, i: (l, i, 0)),
        out_shape=jax.ShapeDtypeStruct(w.shape, BF16),
        compiler_params=_cparams(("parallel", "parallel"), 48),
        name="cast",
    )(w)


_W_IN_SRC = ((2304, 1024), (3328, 512), (768, 768), (1536, 768), (0, 384), (384, 384), (3840, 256),
             (4096, 64))


def _prep_w_in_kernel(w_ref, o_ref):
    dst = 0
    for src, width in _W_IN_SRC:
        o_ref[0, dst:dst + width, :] = w_ref[0, src:src + width, :].astype(BF16)
        dst += width
    o_ref[0, dst:, :] = jnp.zeros((P_W - dst, o_ref.shape[2]), BF16)


def _prep_w_in(w):
    depth, d, n = w.shape
    wt = jnp.swapaxes(w, 1, 2)
    tk = 512
    return pl.pallas_call(
        _prep_w_in_kernel,
        grid=(depth, d // tk),
        in_specs=[pl.BlockSpec((1, n, tk), lambda l, i: (l, 0, i))],
        out_specs=pl.BlockSpec((1, P_W, tk), lambda l, i: (l, 0, i)),
        out_shape=jax.ShapeDtypeStruct((depth, P_W, d), BF16),
        compiler_params=_cparams(("parallel", "parallel"), 48),
        name="prep_w_in",
    )(wt)


def _prep_w_uq(w):
    w = w.reshape(MLA_Q_RANK, MLA_HEADS, MLA_NOPE + MLA_ROPE)
    nope = w[:, :, :MLA_NOPE].reshape(MLA_Q_RANK, MLA_HEADS * MLA_NOPE)
    rope = jnp.pad(w[:, :, MLA_NOPE:], ((0, 0), (0, 0), (0, LANES - MLA_ROPE)))
    return jnp.concatenate([nope, rope.reshape(MLA_Q_RANK, MLA_HEADS * LANES)], axis=1).astype(BF16)


def _lat_mod_row(i, tm):
    return (i * tm) // SEQ


def _ctx_mod_row(i, tm):
    return CTX_MOD_ROW


def kernel(x, c, ctx, c_ctx, w_ada, b_ada, norm_g, w_ffa_in, w_ffa_out, w_ffb_in, w_ffb_out,
           w_in, w_out, ret_gn, cv_dw, cv_dw_b, cv_ln_g, cv_ln_b, cv_pw,
           mla_q_norm, w_uq, mla_kv_norm, w_ukv):
    batch, seq, d = x.shape
    assert (seq, d, ctx.shape[1]) == (SEQ, D_MODEL, CTX_LEN)
    xl = x.reshape(batch * seq, d)
    xc = ctx.reshape(batch * CTX_LEN, d)
    s_in = jnp.concatenate([c, c_ctx[None, :], jnp.zeros((ROWS_PAD - batch - 1, d), F32)], axis=0)
    mod_all = _ada(s_in, w_ada, b_ada).reshape(DEPTH, ROWS_PAD, N_MOD, d)
    tabs = _tables()
    mla_rope = (tabs["mla_c"], tabs["mla_s1"], tabs["mla_s2"])

    wa_in, wa_out, wb_in, wb_out = w_ffa_in, w_ffa_out, w_ffb_in, w_ffb_out
    wi = _prep_w_in(w_in)
    wo = _cast_bf16(w_out, 1024)

    ffn_lat = dict(mod_row=_lat_mod_row, tm=1024, tf=512)
    ffn_ctx = dict(mod_row=_ctx_mod_row, tm=1024, tf=512)
    for l in range(DEPTH):
        last = l == DEPTH - 1
        mod = mod_all[l]
        ng = norm_g[l]
        wq = _prep_w_uq(w_uq[l])
        wkv = w_ukv[l].astype(BF16)
        nq, nkv = mla_q_norm[l][None, :], mla_kv_norm[l][None, :]
        gn = ret_gn[l][None, :]
        conv_misc = jnp.stack([cv_dw_b[l], cv_ln_g[l], cv_ln_b[l]], axis=0)
        pw = cv_pw[l].astype(BF16)

        xl = _ffn(xl, mod, ng[0:2], wa_in, wa_out, l, 0, **ffn_lat)
        xc = _ffn(xc, mod, ng[0:2], wa_in, wa_out, l, 0, **ffn_ctx)

        p_lat = _inproj(xl, mod, ng[2:3], wi, l, _lat_mod_row, 1024, 1408)
        p_ctx = _inproj(xc, mod, ng[2:3], wi, l, _ctx_mod_row, 1024, 1408)

        y_ret_l, y_ret_c = _retention(p_lat, p_ctx, tabs, gn, batch)
        y_conv_l = _conv(p_lat, cv_dw[l], conv_misc, pw, batch, SEQ)
        q_l, k_l, v_l = _mla_proj(p_lat, nq, nkv, wq, wkv, mla_rope, 512)
        q_c, k_c, v_c = _mla_proj(p_ctx, nq, nkv, wq, wkv, None, 512)
        y_mla_l = _attn(q_l, k_l, k_c, v_l, v_c, batch, 2, 512)

        xl = _outproj(xl, y_ret_l, y_conv_l, y_mla_l, mod, ng[3:4], wo, l, _lat_mod_row, 1024)
        xl = _ffn(xl, mod, ng[4:6], wb_in, wb_out, l, 6, **ffn_lat)
        if not last:
            y_conv_c = _conv(p_ctx, cv_dw[l], conv_misc, pw, batch, CTX_LEN)
            y_mla_c = _attn_ctx(q_c, k_c, v_c, batch)
            xc = _outproj(xc, y_ret_c, y_conv_c, y_mla_c, mod, ng[3:4], wo, l, _ctx_mod_row, 1024)
            xc = _ffn(xc, mod, ng[4:6], wb_in, wb_out, l, 6, **ffn_ctx)
    return xl.reshape(batch, seq, d)
```

```python
import functools

import jax
import jax.numpy as jnp
from jax import lax
from jax.experimental import pallas as pl
from jax.experimental.pallas import tpu as pltpu

F32 = jnp.float32
BF16 = jnp.bfloat16

D_MODEL = 2048
SEQ = 2048
DEPTH = 2
GRID_W = 64
CTX_LEN = 256
D_FF = 5632
FFN_RES = 0.5
N_MOD = 9
ROPE_DIM = 64
ROPE_BASE = 10000.0
EPS = 1e-6
RET_HEADS = 6
RET_DK = 64
RET_DV = 128
RET_CHUNK = 128
CONV_CH = 512
CONV_W = 31
MLA_HEADS = 6
MLA_Q_RANK = 512
MLA_KV_RANK = 256
MLA_NOPE = 128
MLA_ROPE = 64
MLA_DV = 128
MLA_SCALE = (MLA_NOPE + MLA_ROPE) ** -0.5
LOG2_E = 1.4426950408889634
RET_W = RET_HEADS * RET_DV
MLA_W = MLA_HEADS * MLA_DV

LANES = 128
MIB = 1024 * 1024
ROW_CHUNK = 128
RET_UNROLL = 16
SUBLANES = 8

P_GLU, P_CQ, P_RV, P_RG, P_RQ, P_RK, P_CKV, P_KR = 0, 1024, 1536, 2304, 3072, 3456, 3840, 4096
P_W = 4224
CTX_MOD_ROW = 4
ROWS_PAD = 8


def _cparams(sem, vmem_mib):
    return pltpu.CompilerParams(dimension_semantics=sem, vmem_limit_bytes=int(vmem_mib * MIB))


def _sigmoid(x):
    return 1.0 / (1.0 + jnp.exp(-x))


def _rms(x, g):
    ms = jnp.mean(x * x, axis=-1, keepdims=True)
    return x * lax.rsqrt(ms + EPS) * g


def _dot(a, b):
    return jnp.dot(a, b, preferred_element_type=F32)


def _half_rows(tm):
    return (slice(0, tm // 2), slice(tm // 2, tm))


def _prenorm_modulate(x_ref, h_ref, gain_row, shift_row, rows):
    for r0 in range(rows.start, rows.stop, ROW_CHUNK):
        x = x_ref[r0:r0 + ROW_CHUNK, :]
        r = lax.rsqrt(jnp.mean(x * x, axis=-1, keepdims=True) + EPS)
        h_ref[r0:r0 + ROW_CHUNK, :] = (x * r * gain_row + shift_row).astype(BF16)


def _postnorm_residual(y_ref, x_ref, gain_row, rows, on_chunk_done=None):
    for r0 in range(rows.start, rows.stop, ROW_CHUNK):
        y = y_ref[r0:r0 + ROW_CHUNK, :]
        r = lax.rsqrt(jnp.mean(y * y, axis=-1, keepdims=True) + EPS)
        y_ref[r0:r0 + ROW_CHUNK, :] = x_ref[r0:r0 + ROW_CHUNK, :] + y * r * gain_row
        if on_chunk_done is not None:
            on_chunk_done(r0 // ROW_CHUNK)


def _dot_nt(a, b):
    return lax.dot_general(a, b, (((1,), (1,)), ((), ())), preferred_element_type=F32)


def _dot_tn(a, b):
    return lax.dot_general(a, b, (((0,), (0,)), ((), ())), preferred_element_type=F32)


def _rope(x, c, s1, s2):
    return x * c + pltpu.roll(x, 96, 1) * s1 + pltpu.roll(x, 32, 1) * s2


def _ada_kernel(s_ref, w_ref, b_ref, o_ref):
    s = s_ref[...]
    s = s * _sigmoid(s)
    o_ref[0] = _dot(s.astype(BF16), w_ref[0].astype(BF16)) + b_ref[0]


def _ada(s_in, w_ada, b_ada):
    depth, d, n = w_ada.shape
    tn = 1024
    return pl.pallas_call(
        _ada_kernel,
        grid=(depth, n // tn),
        in_specs=[
            pl.BlockSpec((ROWS_PAD, d), lambda l, j: (0, 0)),
            pl.BlockSpec((1, d, tn), lambda l, j: (l, 0, j)),
            pl.BlockSpec((1, 1, tn), lambda l, j: (l, 0, j)),
        ],
        out_specs=pl.BlockSpec((1, ROWS_PAD, tn), lambda l, j: (l, 0, j)),
        out_shape=jax.ShapeDtypeStruct((depth, ROWS_PAD, n), F32),
        compiler_params=_cparams(("parallel", "parallel"), 40),
        name="ada",
    )(s_in, w_ada, b_ada.reshape(depth, 1, n))


def _ffn_kernel(x_ref, mod_ref, g_ref, wa_ref, wu_ref, wo_ref, o_hbm, acc_ref, h_ref, sem, *, k0, nj):
    i = pl.program_id(0)
    j = pl.program_id(1)
    tm = acc_ref.shape[0]
    halves = _half_rows(tm)

    def out_copy(c):
        src = acc_ref.at[pl.ds(c * ROW_CHUNK, ROW_CHUNK), :]
        dst = o_hbm.at[pl.ds(pl.multiple_of(i * tm + c * ROW_CHUNK, ROW_CHUNK), ROW_CHUNK), :]
        return pltpu.make_async_copy(src, dst, sem.at[c])

    def chunk_update(rows, wa, wu, wo):
        h = h_ref[rows, :]
        a = _dot(h, wa)
        u = _dot(h, wu)
        act = (a * _sigmoid(a) * u).astype(BF16)
        return _dot(act, wo)

    def weights():
        return wa_ref[...].astype(BF16), wu_ref[...].astype(BF16), wo_ref[...].astype(BF16)

    @pl.when(j == 0)
    def _():
        w = weights()
        gain = g_ref[0:1, :] * (1.0 + mod_ref[0, k0 + 1:k0 + 2, :])
        shift = mod_ref[0, k0:k0 + 1, :]
        for rows in halves:
            _prenorm_modulate(x_ref, h_ref, gain, shift, rows)
            acc_ref[rows, :] = chunk_update(rows, *w)

    @pl.when((j > 0) & (j < nj - 1))
    def _():
        acc_ref[...] += chunk_update(slice(0, tm), *weights())

    @pl.when(j == nj - 1)
    def _():
        w = weights()
        gain = g_ref[1:2, :] * (FFN_RES * mod_ref[0, k0 + 2:k0 + 3, :])
        for rows in halves:
            acc_ref[rows, :] += chunk_update(rows, *w)
            _postnorm_residual(acc_ref, x_ref, gain, rows, lambda c: out_copy(c).start())
        for c in range(tm // ROW_CHUNK):
            out_copy(c).wait()


def _ffn(x, mod, g2, w_in, w_out, layer, k0, mod_row, tm, tf):
    r, d = x.shape
    nj = D_FF // tf
    assert nj >= 2 and tm % (2 * ROW_CHUNK) == 0
    return pl.pallas_call(
        functools.partial(_ffn_kernel, k0=k0, nj=nj),
        grid=(r // tm, nj),
        in_specs=[
            pl.BlockSpec((tm, d), lambda i, j: (i, 0)),
            pl.BlockSpec((1, N_MOD, d), lambda i, j: (mod_row(i, tm), 0, 0)),
            pl.BlockSpec((2, d), lambda i, j: (0, 0)),
            pl.BlockSpec((None, d, tf), lambda i, j: (layer, 0, j)),
            pl.BlockSpec((None, d, tf), lambda i, j: (layer, 0, nj + j)),
            pl.BlockSpec((None, tf, d), lambda i, j: (layer, j, 0)),
        ],
        out_specs=pl.BlockSpec(memory_space=pl.ANY),
        out_shape=jax.ShapeDtypeStruct((r, d), F32),
        scratch_shapes=[
            pltpu.VMEM((tm, d), F32),
            pltpu.VMEM((tm, d), BF16),
            pltpu.SemaphoreType.DMA((tm // ROW_CHUNK,)),
        ],
        compiler_params=_cparams(("arbitrary", "arbitrary"), 60),
        name="ffn",
    )(x, mod, g2, w_in, w_in, w_out)


def _inproj_kernel(x_ref, mod_ref, g_ref, w_ref, o_ref, h_ref):
    j = pl.program_id(1)

    @pl.when(j == 0)
    def _():
        gain = g_ref[...] * (1.0 + mod_ref[0, 4:5, :])
        shift = mod_ref[0, 3:4, :]
        for rows in _half_rows(x_ref.shape[0]):
            _prenorm_modulate(x_ref, h_ref, gain, shift, rows)
            o_ref[rows, :] = _dot_nt(h_ref[rows, :], w_ref[...])

    @pl.when(j > 0)
    def _():
        o_ref[...] = _dot_nt(h_ref[...], w_ref[...])


def _inproj(x, mod, g, w, layer, mod_row, tm, tn):
    r, d = x.shape
    return pl.pallas_call(
        _inproj_kernel,
        grid=(r // tm, P_W // tn),
        in_specs=[
            pl.BlockSpec((tm, d), lambda i, j: (i, 0)),
            pl.BlockSpec((1, N_MOD, d), lambda i, j: (mod_row(i, tm), 0, 0)),
            pl.BlockSpec((1, d), lambda i, j: (0, 0)),
            pl.BlockSpec((None, tn, d), lambda i, j: (layer, j, 0)),
        ],
        out_specs=pl.BlockSpec((tm, tn), lambda i, j: (i, j)),
        out_shape=jax.ShapeDtypeStruct((r, P_W), F32),
        scratch_shapes=[pltpu.VMEM((tm, d), BF16)],
        compiler_params=_cparams(("parallel", "arbitrary"), 56),
        name="inproj",
    )(x, mod, g, w)


def _outproj_kernel(x_ref, yr_ref, yc_ref, ym_ref, mod_ref, g_ref, w_ref, o_ref):
    gain = g_ref[...] * mod_ref[0, 5:6, :]
    for rows in _half_rows(x_ref.shape[0]):
        y = _dot(yr_ref[rows, :], w_ref[0:RET_W, :])
        y += _dot(yc_ref[rows, :], w_ref[RET_W:RET_W + CONV_CH, :])
        y += _dot(ym_ref[rows, :], w_ref[RET_W + CONV_CH:, :])
        o_ref[rows, :] = y
        _postnorm_residual(o_ref, x_ref, gain, rows)


def _outproj(x, y_ret, y_conv, y_mla, mod, g, w, layer, mod_row, tm):
    r, d = x.shape
    return pl.pallas_call(
        _outproj_kernel,
        grid=(r // tm,),
        in_specs=[
            pl.BlockSpec((tm, d), lambda i: (i, 0)),
            pl.BlockSpec((tm, RET_W), lambda i: (i, 0)),
            pl.BlockSpec((tm, CONV_CH), lambda i: (i, 0)),
            pl.BlockSpec((tm, MLA_W), lambda i: (i, 0)),
            pl.BlockSpec((1, N_MOD, d), lambda i: (mod_row(i, tm), 0, 0)),
            pl.BlockSpec((1, d), lambda i: (0, 0)),
            pl.BlockSpec((None, d, d), lambda i: (layer, 0, 0), pipeline_mode=pl.Buffered(1)),
        ],
        out_specs=pl.BlockSpec((tm, d), lambda i: (i, 0)),
        out_shape=jax.ShapeDtypeStruct((r, d), F32),
        compiler_params=_cparams(("parallel",), 60),
        name="outproj",
    )(x, y_ret, y_conv, y_mla, mod, g, w)


def _ret_kernel(ql_ref, kl_ref, vl_ref, gl_ref, qc_ref, kc_ref, vc_ref, gc_ref,
                rc_ref, rs1_ref, rs2_ref, dm_ref, kf_ref, kb_ref, qf_ref, qb_ref, cd_ref, gn_ref,
                yl_ref, yc_ref, qr_ref, kr_ref, sf_ref, *, n_lat, n_ctx):
    c_len = RET_CHUNK
    lane = lax.broadcasted_iota(jnp.int32, (1, LANES), 1)
    head_masks = (lane < RET_DK, lane >= RET_DK)
    row = lax.broadcasted_iota(jnp.int32, (LANES, 2 * RET_DV), 0)
    col = lax.broadcasted_iota(jnp.int32, (LANES, 2 * RET_DV), 1)
    block_diag = (row < RET_DK) == (col < RET_DV)
    kf, kb, qf, qb, cd = kf_ref[0], kb_ref[0], qf_ref[0], qb_ref[0], cd_ref[0]
    gn = gn_ref[...]
    zero_state = jnp.zeros((LANES, 2 * RET_DV), F32)

    def kv_state(k, v16, dec):
        return jnp.where(block_diag, _dot_tn((k * dec).astype(BF16), v16), 0.0)

    def chunk_out(q, k, v16, g, sf16, sb16):
        k16 = k.astype(BF16)
        parts = []
        for e in range(2):
            q16 = jnp.where(head_masks[e], q, 0.0).astype(BF16)
            a = _dot_nt(q16, k16) * dm_ref[0, e]
            parts.append(_dot(a.astype(BF16), v16[:, e * RET_DV:(e + 1) * RET_DV]))
        o = jnp.concatenate(parts, axis=1)
        o = o + _dot((q * qf).astype(BF16), sf16) + _dot((q * qb).astype(BF16), sb16)
        normed = []
        for e in range(2):
            oe = o[:, e * RET_DV:(e + 1) * RET_DV]
            dlt = oe - jnp.mean(oe, axis=-1, keepdims=True)
            var = jnp.mean(dlt * dlt, axis=-1, keepdims=True)
            normed.append(dlt * lax.rsqrt(var + EPS))
        on = jnp.concatenate(normed, axis=1) * gn
        return ((g * _sigmoid(g)) * on).astype(BF16)

    cq = [qc_ref[c * c_len:(c + 1) * c_len, :] for c in range(n_ctx)]
    ck = [kc_ref[c * c_len:(c + 1) * c_len, :] for c in range(n_ctx)]
    cv = [vc_ref[c * c_len:(c + 1) * c_len, :].astype(BF16) for c in range(n_ctx)]
    kvf = [kv_state(ck[c], cv[c], kf) for c in range(n_ctx)]
    kvb = [kv_state(ck[c], cv[c], kb) for c in range(n_ctx)]
    sf_list = [zero_state]
    for c in range(n_ctx):
        sf_list.append(cd * sf_list[c] + kvf[c])
    sb_list = [zero_state] * (n_ctx + 1)
    for c in range(n_ctx - 1, -1, -1):
        sb_list[c] = cd * sb_list[c + 1] + kvb[c]
    for c in range(n_ctx):
        yc_ref[c * c_len:(c + 1) * c_len, :] = chunk_out(
            cq[c], ck[c], cv[c], gc_ref[c * c_len:(c + 1) * c_len, :],
            sf_list[c].astype(BF16), sb_list[c + 1].astype(BF16))
    s0f, s0b = sf_list[n_ctx], sb_list[0]

    def fwd(c, sf):
        r0 = pl.multiple_of(c * c_len, c_len)
        rows = pl.ds(r0, c_len)
        rc, rs1, rs2 = rc_ref[rows, :], rs1_ref[rows, :], rs2_ref[rows, :]
        q = _rope(ql_ref[rows, :], rc, rs1, rs2)
        k = _rope(kl_ref[rows, :], rc, rs1, rs2)
        qr_ref[rows, :] = q
        kr_ref[rows, :] = k
        sf_ref[c] = sf.astype(BF16)
        return cd * sf + kv_state(k, vl_ref[rows, :].astype(BF16), kf)

    lax.fori_loop(0, n_lat, fwd, s0f, unroll=RET_UNROLL)

    def bwd(i, sb):
        c = n_lat - 1 - i
        r0 = pl.multiple_of(c * c_len, c_len)
        rows = pl.ds(r0, c_len)
        q, k = qr_ref[rows, :], kr_ref[rows, :]
        v16 = vl_ref[rows, :].astype(BF16)
        yl_ref[rows, :] = chunk_out(q, k, v16, gl_ref[rows, :], sf_ref[c], sb.astype(BF16))
        return cd * sb + kv_state(k, v16, kb)

    lax.fori_loop(0, n_lat, bwd, s0b, unroll=RET_UNROLL)


def _retention(p_lat, p_ctx, tabs, gn, batch):
    t, lc = SEQ, CTX_LEN
    n_pairs = RET_HEADS // 2
    n_lat, n_ctx = t // RET_CHUNK, lc // RET_CHUNK
    qw, vw = 2 * RET_DK, 2 * RET_DV

    def pspec(rows, width, col0):
        return pl.BlockSpec((rows, width), lambda b, hp: (b, col0 // width + hp))

    tab3 = pl.BlockSpec((1, LANES, LANES), lambda b, hp: (hp, 0, 0))
    rope_spec = pl.BlockSpec((t, LANES), lambda b, hp: (0, 0))
    return pl.pallas_call(
        functools.partial(_ret_kernel, n_lat=n_lat, n_ctx=n_ctx),
        grid=(batch, n_pairs),
        in_specs=[
            pspec(t, qw, P_RQ), pspec(t, qw, P_RK), pspec(t, vw, P_RV), pspec(t, vw, P_RG),
            pspec(lc, qw, P_RQ), pspec(lc, qw, P_RK), pspec(lc, vw, P_RV), pspec(lc, vw, P_RG),
            rope_spec, rope_spec, rope_spec,
            pl.BlockSpec((1, 2, LANES, LANES), lambda b, hp: (hp, 0, 0, 0)),
            tab3, tab3, tab3, tab3,
            pl.BlockSpec((1, LANES, vw), lambda b, hp: (hp, 0, 0)),
            pl.BlockSpec((1, vw), lambda b, hp: (0, hp)),
        ],
        out_specs=[
            pl.BlockSpec((t, vw), lambda b, hp: (b, hp)),
            pl.BlockSpec((lc, vw), lambda b, hp: (b, hp)),
        ],
        out_shape=[
            jax.ShapeDtypeStruct((batch * t, RET_W), BF16),
            jax.ShapeDtypeStruct((batch * lc, RET_W), BF16),
        ],
        scratch_shapes=[
            pltpu.VMEM((t, LANES), F32),
            pltpu.VMEM((t, LANES), F32),
            pltpu.VMEM((n_lat, LANES, vw), BF16),
        ],
        compiler_params=_cparams(("parallel", "parallel"), 48),
        name="retention",
    )(p_lat, p_lat, p_lat, p_lat, p_ctx, p_ctx, p_ctx, p_ctx,
      tabs["ret_c"], tabs["ret_s1"], tabs["ret_s2"],
      tabs["dm"], tabs["kf"], tabs["kb"], tabs["qf"], tabs["qb"], tabs["cd"], gn)


def _conv_kernel(glu_ref, dw_ref, misc_ref, pw_ref, o_ref, pad_ref, *, t, rt):
    half = CONV_W // 2
    lo = 16
    zeros = jnp.zeros((lo, CONV_CH), F32)
    pad_ref[0:lo, :] = zeros
    pad_ref[lo + t:lo + t + lo, :] = zeros

    def glu(r, carry):
        rows = pl.ds(pl.multiple_of(r * rt, rt), rt)
        a = glu_ref[rows, 0:CONV_CH]
        b = glu_ref[rows, CONV_CH:2 * CONV_CH]
        pad_ref[pl.ds(pl.multiple_of(r * rt + lo, 8), rt), :] = a * _sigmoid(b)
        return carry

    lax.fori_loop(0, t // rt, glu, 0)

    def conv(r, carry):
        base = pl.multiple_of(r * rt, rt)
        win_rows = rt + 2 * lo
        cg_w = LANES
        accs = []
        for cg in range(CONV_CH // cg_w):
            cols = slice(cg * cg_w, (cg + 1) * cg_w)
            win = pad_ref[pl.ds(base, win_rows), cols]
            acc = jnp.zeros((rt, cg_w), F32) + misc_ref[0:1, cols]
            for phase in range(SUBLANES):
                shifted = win if phase == 0 else pltpu.roll(win, win_rows - phase, 0)
                for off in range(phase, lo + half + 1, SUBLANES):
                    k = off - (lo - half)
                    if 0 <= k < CONV_W:
                        a0 = off - phase
                        acc = acc + shifted[a0:a0 + rt, :] * dw_ref[k:k + 1, cols]
            accs.append(acc)
        acc = jnp.concatenate(accs, axis=1)
        dlt = acc - jnp.mean(acc, axis=-1, keepdims=True)
        var = jnp.mean(dlt * dlt, axis=-1, keepdims=True)
        y = dlt * lax.rsqrt(var + EPS) * misc_ref[1:2, :] + misc_ref[2:3, :]
        y = y * _sigmoid(y)
        o_ref[pl.ds(base, rt), :] = _dot(y.astype(BF16), pw_ref[...]).astype(BF16)
        return carry

    lax.fori_loop(0, t // rt, conv, 0, unroll=4)


def _conv(p, dw, misc, pw, batch, t):
    rt = 64
    return pl.pallas_call(
        functools.partial(_conv_kernel, t=t, rt=rt),
        grid=(batch,),
        in_specs=[
            pl.BlockSpec((t, 2 * CONV_CH), lambda b: (b, P_GLU // (2 * CONV_CH))),
            pl.BlockSpec((CONV_W, CONV_CH), lambda b: (0, 0)),
            pl.BlockSpec((3, CONV_CH), lambda b: (0, 0)),
            pl.BlockSpec((CONV_CH, CONV_CH), lambda b: (0, 0)),
        ],
        out_specs=pl.BlockSpec((t, CONV_CH), lambda b: (b, 0)),
        out_shape=jax.ShapeDtypeStruct((batch * t, CONV_CH), BF16),
        scratch_shapes=[pltpu.VMEM((t + 32, CONV_CH), F32)],
        compiler_params=_cparams(("parallel",), 40),
        name="conv",
    )(p, dw, misc, pw)


def _mla_proj_kernel(cq_ref, ckv_ref, kr_ref, nq_ref, nkv_ref, wq_ref, wkv_ref, *rest, rope):
    if rope:
        rc_ref, rs1_ref, rs2_ref, q_ref, k_ref, v_ref = rest
        rot = lambda z: _rope(z, rc_ref[...], rs1_ref[...], rs2_ref[...])
    else:
        q_ref, k_ref, v_ref = rest
        rot = lambda z: z
    qq = _dot(_rms(cq_ref[...], nq_ref[...]).astype(BF16), wq_ref[...]) * (MLA_SCALE * LOG2_E)
    kv = _dot(_rms(ckv_ref[...], nkv_ref[...]).astype(BF16), wkv_ref[...])
    kr = rot(kr_ref[...]).astype(BF16)
    hw = MLA_NOPE + LANES
    for h in range(MLA_HEADS):
        q_ref[:, h * hw:h * hw + MLA_NOPE] = qq[:, h * MLA_NOPE:(h + 1) * MLA_NOPE].astype(BF16)
        qr = qq[:, MLA_HEADS * MLA_NOPE + h * LANES:MLA_HEADS * MLA_NOPE + (h + 1) * LANES]
        q_ref[:, h * hw + MLA_NOPE:(h + 1) * hw] = rot(qr).astype(BF16)
        k_ref[:, h * hw:h * hw + MLA_NOPE] = kv[:, 2 * h * LANES:(2 * h + 1) * LANES].astype(BF16)
        k_ref[:, h * hw + MLA_NOPE:(h + 1) * hw] = kr
        v_ref[:, h * MLA_DV:(h + 1) * MLA_DV] = kv[:, (2 * h + 1) * LANES:(2 * h + 2) * LANES].astype(BF16)


def _mla_proj(p, nq, nkv, wq, wkv, rope_tabs, tm):
    r = p.shape[0]
    hw = MLA_NOPE + LANES
    rope = rope_tabs is not None
    in_specs = [
        pl.BlockSpec((tm, MLA_Q_RANK), lambda i: (i, P_CQ // MLA_Q_RANK)),
        pl.BlockSpec((tm, MLA_KV_RANK), lambda i: (i, P_CKV // MLA_KV_RANK)),
        pl.BlockSpec((tm, LANES), lambda i: (i, P_KR // LANES)),
        pl.BlockSpec((1, MLA_Q_RANK), lambda i: (0, 0)),
        pl.BlockSpec((1, MLA_KV_RANK), lambda i: (0, 0)),
        pl.BlockSpec(wq.shape, lambda i: (0, 0)),
        pl.BlockSpec(wkv.shape, lambda i: (0, 0)),
    ]
    args = [p, p, p, nq, nkv, wq, wkv]
    if rope:
        n_pos = SEQ // tm
        in_specs += [pl.BlockSpec((tm, LANES), lambda i: (i % n_pos, 0))] * 3
        args += list(rope_tabs)
    return pl.pallas_call(
        functools.partial(_mla_proj_kernel, rope=rope),
        grid=(r // tm,),
        in_specs=in_specs,
        out_specs=[
            pl.BlockSpec((tm, MLA_HEADS * hw), lambda i: (i, 0)),
            pl.BlockSpec((tm, MLA_HEADS * hw), lambda i: (i, 0)),
            pl.BlockSpec((tm, MLA_W), lambda i: (i, 0)),
        ],
        out_shape=[
            jax.ShapeDtypeStruct((r, MLA_HEADS * hw), BF16),
            jax.ShapeDtypeStruct((r, MLA_HEADS * hw), BF16),
            jax.ShapeDtypeStruct((r, MLA_W), BF16),
        ],
        compiler_params=_cparams(("parallel",), 40),
        name="mla_proj",
    )(*args)


def _attn_kernel(q_ref, kl_ref, kc_ref, vl_ref, vc_ref, o_ref, *, n_heads, sub):
    hw = MLA_NOPE + LANES
    for e in range(n_heads):
        kcols = slice(e * hw, (e + 1) * hw)
        vcols = slice(e * MLA_DV, (e + 1) * MLA_DV)
        for i in range(q_ref.shape[0] // sub):
            rows = slice(i * sub, (i + 1) * sub)
            q = q_ref[rows, kcols]
            sl = _dot_nt(q, kl_ref[:, kcols])
            sc = _dot_nt(q, kc_ref[:, kcols])
            m = jnp.maximum(jnp.max(sl, axis=-1, keepdims=True), jnp.max(sc, axis=-1, keepdims=True))
            el = jnp.exp2(sl - m)
            ec = jnp.exp2(sc - m)
            den = jnp.sum(el, axis=-1, keepdims=True) + jnp.sum(ec, axis=-1, keepdims=True)
            o = _dot(el.astype(BF16), vl_ref[:, vcols]) + _dot(ec.astype(BF16), vc_ref[:, vcols])
            o_ref[rows, vcols] = (o / den).astype(BF16)


def _attn(q, k_lat, k_ctx, v_lat, v_ctx, batch, n_heads, sub):
    hw = n_heads * (MLA_NOPE + LANES)
    vw = n_heads * MLA_DV
    return pl.pallas_call(
        functools.partial(_attn_kernel, n_heads=n_heads, sub=sub),
        grid=(batch, MLA_HEADS // n_heads),
        in_specs=[
            pl.BlockSpec((SEQ, hw), lambda b, h: (b, h)),
            pl.BlockSpec((SEQ, hw), lambda b, h: (b, h)),
            pl.BlockSpec((CTX_LEN, hw), lambda b, h: (b, h)),
            pl.BlockSpec((SEQ, vw), lambda b, h: (b, h)),
            pl.BlockSpec((CTX_LEN, vw), lambda b, h: (b, h)),
        ],
        out_specs=pl.BlockSpec((SEQ, vw), lambda b, h: (b, h)),
        out_shape=jax.ShapeDtypeStruct((batch * SEQ, MLA_W), BF16),
        compiler_params=_cparams(("parallel", "parallel"), 48),
        name="mla_attn",
    )(q, k_lat, k_ctx, v_lat, v_ctx)


def _attn_ctx_kernel(q_ref, k_ref, v_ref, o_ref):
    s = _dot_nt(q_ref[...], k_ref[...])
    e = jnp.exp2(s - jnp.max(s, axis=-1, keepdims=True))
    o = _dot(e.astype(BF16), v_ref[...])
    o_ref[...] = (o / jnp.sum(e, axis=-1, keepdims=True)).astype(BF16)


def _attn_ctx(q, k, v, batch):
    hw = MLA_NOPE + LANES
    return pl.pallas_call(
        _attn_ctx_kernel,
        grid=(batch, MLA_HEADS),
        in_specs=[
            pl.BlockSpec((CTX_LEN, hw), lambda b, h: (b, h)),
            pl.BlockSpec((CTX_LEN, hw), lambda b, h: (b, h)),
            pl.BlockSpec((CTX_LEN, MLA_DV), lambda b, h: (b, h)),
        ],
        out_specs=pl.BlockSpec((CTX_LEN, MLA_DV), lambda b, h: (b, h)),
        out_shape=jax.ShapeDtypeStruct((batch * CTX_LEN, MLA_W), BF16),
        compiler_params=_cparams(("parallel", "parallel"), 32),
        name="mla_attn_ctx",
    )(q, k, v)


def _tables():
    rows = SEQ // GRID_W
    quarter = ROPE_DIM // 4
    inv = ROPE_BASE ** (-jnp.arange(quarter, dtype=F32) / quarter)
    r_idx = jnp.repeat(jnp.arange(rows, dtype=F32), GRID_W)
    c_idx = jnp.tile(jnp.arange(GRID_W, dtype=F32), rows)
    ang = jnp.concatenate([r_idx[:, None] * inv, c_idx[:, None] * inv], axis=-1)
    cos, sin = jnp.cos(ang), jnp.sin(ang)
    z = jnp.zeros_like(cos)
    tabs = {
        "ret_c": jnp.concatenate([cos, cos, cos, cos], axis=1),
        "ret_s1": jnp.concatenate([-sin, z, -sin, z], axis=1),
        "ret_s2": jnp.concatenate([z, sin, z, sin], axis=1),
        "mla_c": jnp.concatenate([cos, cos, z, z], axis=1),
        "mla_s1": jnp.concatenate([-sin, z, z, z], axis=1),
        "mla_s2": jnp.concatenate([z, sin, z, z], axis=1),
    }
    c_len = RET_CHUNK
    log_g = jnp.log1p(-jnp.exp2(-5.0 - jnp.arange(RET_HEADS, dtype=F32)))
    i = jnp.arange(c_len, dtype=F32)
    k_scale = RET_DK ** -0.5
    lg = log_g[:, None, None]
    dm = jnp.exp(lg * jnp.abs(i[:, None] - i[None, :])) * k_scale
    tabs["dm"] = dm.reshape(RET_HEADS // 2, 2, c_len, c_len)

    def lanes_by_head(per_head_rows):
        t = jnp.repeat(per_head_rows[:, :, None], RET_DK, axis=2)
        t = t.reshape(RET_HEADS // 2, 2, c_len, RET_DK)
        return jnp.concatenate([t[:, 0], t[:, 1]], axis=-1)

    tabs["kf"] = lanes_by_head(jnp.exp(log_g[:, None] * (c_len - 1.0 - i)) * k_scale)
    tabs["kb"] = lanes_by_head(jnp.exp(log_g[:, None] * i) * k_scale)
    tabs["qf"] = lanes_by_head(jnp.exp(log_g[:, None] * (i + 1.0)))
    tabs["qb"] = lanes_by_head(jnp.exp(log_g[:, None] * (c_len - i)))
    cdec = jnp.exp(log_g * c_len).reshape(RET_HEADS // 2, 2)
    cd = jnp.repeat(cdec[:, :, None], RET_DK, axis=2).reshape(RET_HEADS // 2, 2 * RET_DK, 1)
    tabs["cd"] = jnp.broadcast_to(cd, (RET_HEADS // 2, 2 * RET_DK, 2 * RET_DV))
    return tabs


def _cast_kernel(x_ref, o_ref):
    o_ref[...] = x_ref[...].astype(BF16)


def _cast_bf16(w, tr):
    depth, r, c = w.shape
    return pl.pallas_call(
        _cast_kernel,
        grid=(depth, r // tr),
        in_specs=[pl.BlockSpec((1, tr, c), lambda l, i: (l, i, 0))],
        out_specs=pl.BlockSpec((1, tr, c), lambda l, i: (l, i, 0)),
        out_shape=jax.ShapeDtypeStruct(w.shape, BF16),
        compiler_params=_cparams(("parallel", "parallel"), 48),
        name="cast",
    )(w)


_W_IN_SRC = ((2304, 1024), (3328, 512), (768, 768), (1536, 768), (0, 384), (384, 384), (3840, 256),
             (4096, 64))


def _prep_w_in_kernel(w_ref, o_ref):
    dst = 0
    for src, width in _W_IN_SRC:
        o_ref[0, dst:dst + width, :] = w_ref[0, src:src + width, :].astype(BF16)
        dst += width
    o_ref[0, dst:, :] = jnp.zeros((P_W - dst, o_ref.shape[2]), BF16)


def _prep_w_in(w):
    depth, d, n = w.shape
    wt = jnp.swapaxes(w, 1, 2)
    tk = 512
    return pl.pallas_call(
        _prep_w_in_kernel,
        grid=(depth, d // tk),
        in_specs=[pl.BlockSpec((1, n, tk), lambda l, i: (l, 0, i))],
        out_specs=pl.BlockSpec((1, P_W, tk), lambda l, i: (l, 0, i)),
        out_shape=jax.ShapeDtypeStruct((depth, P_W, d), BF16),
        compiler_params=_cparams(("parallel", "parallel"), 48),
        name="prep_w_in",
    )(wt)


def _prep_w_uq(w):
    w = w.reshape(MLA_Q_RANK, MLA_HEADS, MLA_NOPE + MLA_ROPE)
    nope = w[:, :, :MLA_NOPE].reshape(MLA_Q_RANK, MLA_HEADS * MLA_NOPE)
    rope = jnp.pad(w[:, :, MLA_NOPE:], ((0, 0), (0, 0), (0, LANES - MLA_ROPE)))
    return jnp.concatenate([nope, rope.reshape(MLA_Q_RANK, MLA_HEADS * LANES)], axis=1).astype(BF16)


def _lat_mod_row(i, tm):
    return (i * tm) // SEQ


def _ctx_mod_row(i, tm):
    return CTX_MOD_ROW


def kernel(x, c, ctx, c_ctx, w_ada, b_ada, norm_g, w_ffa_in, w_ffa_out, w_ffb_in, w_ffb_out,
           w_in, w_out, ret_gn, cv_dw, cv_dw_b, cv_ln_g, cv_ln_b, cv_pw,
           mla_q_norm, w_uq, mla_kv_norm, w_ukv):
    batch, seq, d = x.shape
    assert (seq, d, ctx.shape[1]) == (SEQ, D_MODEL, CTX_LEN)
    xl = x.reshape(batch * seq, d)
    xc = ctx.reshape(batch * CTX_LEN, d)
    s_in = jnp.concatenate([c, c_ctx[None, :], jnp.zeros((ROWS_PAD - batch - 1, d), F32)], axis=0)
    mod_all = _ada(s_in, w_ada, b_ada).reshape(DEPTH, ROWS_PAD, N_MOD, d)
    tabs = _tables()
    mla_rope = (tabs["mla_c"], tabs["mla_s1"], tabs["mla_s2"])

    wa_in, wa_out, wb_in, wb_out = w_ffa_in, w_ffa_out, w_ffb_in, w_ffb_out
    wi = _prep_w_in(w_in)
    wo = _cast_bf16(w_out, 1024)

    ffn_lat = dict(mod_row=_lat_mod_row, tm=1024, tf=512)
    ffn_ctx = dict(mod_row=_ctx_mod_row, tm=1024, tf=512)
    for l in range(DEPTH):
        last = l == DEPTH - 1
        mod = mod_all[l]
        ng = norm_g[l]
        wq = _prep_w_uq(w_uq[l])
        wkv = w_ukv[l].astype(BF16)
        nq, nkv = mla_q_norm[l][None, :], mla_kv_norm[l][None, :]
        gn = ret_gn[l][None, :]
        conv_misc = jnp.stack([cv_dw_b[l], cv_ln_g[l], cv_ln_b[l]], axis=0)
        pw = cv_pw[l].astype(BF16)

        xl = _ffn(xl, mod, ng[0:2], wa_in, wa_out, l, 0, **ffn_lat)
        xc = _ffn(xc, mod, ng[0:2], wa_in, wa_out, l, 0, **ffn_ctx)

        p_lat = _inproj(xl, mod, ng[2:3], wi, l, _lat_mod_row, 1024, 1408)
        p_ctx = _inproj(xc, mod, ng[2:3], wi, l, _ctx_mod_row, 1024, 1408)

        y_ret_l, y_ret_c = _retention(p_lat, p_ctx, tabs, gn, batch)
        y_conv_l = _conv(p_lat, cv_dw[l], conv_misc, pw, batch, SEQ)
        q_l, k_l, v_l = _mla_proj(p_lat, nq, nkv, wq, wkv, mla_rope, 512)
        q_c, k_c, v_c = _mla_proj(p_ctx, nq, nkv, wq, wkv, None, 512)
        y_mla_l = _attn(q_l, k_l, k_c, v_l, v_c, batch, 2, 512)

        xl = _outproj(xl, y_ret_l, y_conv_l, y_mla_l, mod, ng[3:4], wo, l, _lat_mod_row, 1024)
        xl = _ffn(xl, mod, ng[4:6], wb_in, wb_out, l, 6, **ffn_lat)
        if not last:
            y_conv_c = _conv(p_ctx, cv_dw[l], conv_misc, pw, batch, CTX_LEN)
            y_mla_c = _attn_ctx(q_c, k_c, v_c, batch)
            xc = _outproj(xc, y_ret_c, y_conv_c, y_mla_c, mod, ng[3:4], wo, l, _ctx_mod_row, 1024)
            xc = _ffn(xc, mod, ng[4:6], wb_in, wb_out, l, 6, **ffn_ctx)
    return xl.reshape(batch, seq, d)
```

```python
import functools

import jax
import jax.numpy as jnp
from jax import lax
from jax.experimental import pallas as pl
from jax.experimental.pallas import tpu as pltpu

F32 = jnp.float32
BF16 = jnp.bfloat16

D_MODEL = 2048
SEQ = 2048
DEPTH = 2
GRID_W = 64
CTX_LEN = 256
D_FF = 5632
FFN_RES = 0.5
N_MOD = 9
ROPE_DIM = 64
ROPE_BASE = 10000.0
EPS = 1e-6
RET_HEADS = 6
RET_DK = 64
RET_DV = 128
RET_CHUNK = 128
CONV_CH = 512
CONV_W = 31
MLA_HEADS = 6
MLA_Q_RANK = 512
MLA_KV_RANK = 256
MLA_NOPE = 128
MLA_ROPE = 64
MLA_DV = 128
MLA_SCALE = (MLA_NOPE + MLA_ROPE) ** -0.5
LOG2_E = 1.4426950408889634
RET_W = RET_HEADS * RET_DV
MLA_W = MLA_HEADS * MLA_DV

LANES = 128
SUBLANES = 8
MIB = 1024 * 1024
VMEM_V7X_MIB = 64
VMEM_MATMUL_MIB = VMEM_V7X_MIB - 4
VMEM_INPROJ_MIB = 56
VMEM_MIX_MIB = 48
VMEM_SMALL_MIB = 40

FFN_TM, FFN_TF = 1024, 512
INPROJ_TM, INPROJ_TN = 1024, 1408
OUTPROJ_TM = 1024
MLA_PROJ_TM = 1024
ATTN_HEADS, ATTN_SUB = 2, 512
CONV_RT, CONV_UNROLL = 64, 4
ADA_TN = 2048
W_OUT_CAST_ROWS = 1024
W_IN_PREP_COLS = 512
ROW_CHUNK = 128
RET_UNROLL = 16

P_GLU, P_CQ, P_RV, P_RG, P_RQ, P_RK, P_CKV, P_KR = 0, 1024, 1536, 2304, 3072, 3456, 3840, 4096
P_W = 4224
CTX_MOD_ROW = 4
ROWS_PAD = 8


def _cparams(sem, vmem_mib):
    return pltpu.CompilerParams(dimension_semantics=sem, vmem_limit_bytes=int(vmem_mib * MIB))


def _sigmoid(x):
    return 1.0 / (1.0 + jnp.exp(-x))


def _rms(x, g):
    ms = jnp.mean(x * x, axis=-1, keepdims=True)
    return x * lax.rsqrt(ms + EPS) * g


def _dot(a, b):
    return jnp.dot(a, b, preferred_element_type=F32)


def _half_rows(tm):
    return (slice(0, tm // 2), slice(tm // 2, tm))


def _prenorm_modulate(x_ref, h_ref, gain_row, shift_row, rows):
    for r0 in range(rows.start, rows.stop, ROW_CHUNK):
        x = x_ref[r0:r0 + ROW_CHUNK, :]
        r = lax.rsqrt(jnp.mean(x * x, axis=-1, keepdims=True) + EPS)
        h_ref[r0:r0 + ROW_CHUNK, :] = (x * r * gain_row + shift_row).astype(BF16)


def _postnorm_residual(y_ref, x_ref, gain_row, rows, on_chunk_done=None):
    for r0 in range(rows.start, rows.stop, ROW_CHUNK):
        y = y_ref[r0:r0 + ROW_CHUNK, :]
        r = lax.rsqrt(jnp.mean(y * y, axis=-1, keepdims=True) + EPS)
        y_ref[r0:r0 + ROW_CHUNK, :] = x_ref[r0:r0 + ROW_CHUNK, :] + y * r * gain_row
        if on_chunk_done is not None:
            on_chunk_done(r0 // ROW_CHUNK)


def _dot_nt(a, b):
    return lax.dot_general(a, b, (((1,), (1,)), ((), ())), preferred_element_type=F32)


def _dot_tn(a, b):
    return lax.dot_general(a, b, (((0,), (0,)), ((), ())), preferred_element_type=F32)


def _rope(x, c, s1, s2):
    return x * c + pltpu.roll(x, 96, 1) * s1 + pltpu.roll(x, 32, 1) * s2


def _ada_kernel(s_ref, w_ref, b_ref, o_ref):
    s = s_ref[...]
    s = s * _sigmoid(s)
    o_ref[0] = _dot(s.astype(BF16), w_ref[0].astype(BF16)) + b_ref[0]


def _ada(s_in, w_ada, b_ada):
    depth, d, n = w_ada.shape
    tn = ADA_TN
    return pl.pallas_call(
        _ada_kernel,
        grid=(depth, n // tn),
        in_specs=[
            pl.BlockSpec((ROWS_PAD, d), lambda l, j: (0, 0)),
            pl.BlockSpec((1, d, tn), lambda l, j: (l, 0, j)),
            pl.BlockSpec((1, 1, tn), lambda l, j: (l, 0, j)),
        ],
        out_specs=pl.BlockSpec((1, ROWS_PAD, tn), lambda l, j: (l, 0, j)),
        out_shape=jax.ShapeDtypeStruct((depth, ROWS_PAD, n), F32),
        compiler_params=_cparams(("parallel", "parallel"), VMEM_MIX_MIB),
        name="ada",
    )(s_in, w_ada, b_ada.reshape(depth, 1, n))


def _ffn_kernel(x_ref, mod_ref, g_ref, wa_ref, wu_ref, wo_ref, o_hbm, acc_ref, h_ref, sem, *, k0, nj):
    i = pl.program_id(0)
    j = pl.program_id(1)
    tm = acc_ref.shape[0]
    halves = _half_rows(tm)

    def out_copy(c):
        src = acc_ref.at[pl.ds(c * ROW_CHUNK, ROW_CHUNK), :]
        dst = o_hbm.at[pl.ds(pl.multiple_of(i * tm + c * ROW_CHUNK, ROW_CHUNK), ROW_CHUNK), :]
        return pltpu.make_async_copy(src, dst, sem.at[c])

    def chunk_update(rows, wa, wu, wo):
        h = h_ref[rows, :]
        a = _dot(h, wa)
        u = _dot(h, wu)
        act = (a * _sigmoid(a) * u).astype(BF16)
        return _dot(act, wo)

    def weights():
        return wa_ref[...].astype(BF16), wu_ref[...].astype(BF16), wo_ref[...].astype(BF16)

    @pl.when(j == 0)
    def _():
        w = weights()
        gain = g_ref[0:1, :] * (1.0 + mod_ref[0, k0 + 1:k0 + 2, :])
        shift = mod_ref[0, k0:k0 + 1, :]
        for rows in halves:
            _prenorm_modulate(x_ref, h_ref, gain, shift, rows)
            acc_ref[rows, :] = chunk_update(rows, *w)

    @pl.when((j > 0) & (j < nj - 1))
    def _():
        acc_ref[...] += chunk_update(slice(0, tm), *weights())

    @pl.when(j == nj - 1)
    def _():
        w = weights()
        gain = g_ref[1:2, :] * (FFN_RES * mod_ref[0, k0 + 2:k0 + 3, :])
        for rows in halves:
            acc_ref[rows, :] += chunk_update(rows, *w)
            _postnorm_residual(acc_ref, x_ref, gain, rows, lambda c: out_copy(c).start())
        for c in range(tm // ROW_CHUNK):
            out_copy(c).wait()


def _ffn(x, mod, g2, w_in, w_out, layer, k0, mod_row, tm, tf):
    r, d = x.shape
    nj = D_FF // tf
    assert nj >= 2 and tm % (2 * ROW_CHUNK) == 0
    return pl.pallas_call(
        functools.partial(_ffn_kernel, k0=k0, nj=nj),
        grid=(r // tm, nj),
        in_specs=[
            pl.BlockSpec((tm, d), lambda i, j: (i, 0)),
            pl.BlockSpec((1, N_MOD, d), lambda i, j: (mod_row(i, tm), 0, 0)),
            pl.BlockSpec((2, d), lambda i, j: (0, 0)),
            pl.BlockSpec((None, d, tf), lambda i, j: (layer, 0, j)),
            pl.BlockSpec((None, d, tf), lambda i, j: (layer, 0, nj + j)),
            pl.BlockSpec((None, tf, d), lambda i, j: (layer, j, 0)),
        ],
        out_specs=pl.BlockSpec(memory_space=pl.ANY),
        out_shape=jax.ShapeDtypeStruct((r, d), F32),
        scratch_shapes=[
            pltpu.VMEM((tm, d), F32),
            pltpu.VMEM((tm, d), BF16),
            pltpu.SemaphoreType.DMA((tm // ROW_CHUNK,)),
        ],
        compiler_params=_cparams(("arbitrary", "arbitrary"), VMEM_MATMUL_MIB),
        name="ffn",
    )(x, mod, g2, w_in, w_in, w_out)


def _inproj_kernel(x_ref, mod_ref, g_ref, w_ref, o_ref, h_ref):
    j = pl.program_id(1)

    @pl.when(j == 0)
    def _():
        gain = g_ref[...] * (1.0 + mod_ref[0, 4:5, :])
        shift = mod_ref[0, 3:4, :]
        for rows in _half_rows(x_ref.shape[0]):
            _prenorm_modulate(x_ref, h_ref, gain, shift, rows)
            o_ref[rows, :] = _dot_nt(h_ref[rows, :], w_ref[...])

    @pl.when(j > 0)
    def _():
        o_ref[...] = _dot_nt(h_ref[...], w_ref[...])


def _inproj(x, mod, g, w, layer, mod_row, tm, tn):
    r, d = x.shape
    return pl.pallas_call(
        _inproj_kernel,
        grid=(r // tm, P_W // tn),
        in_specs=[
            pl.BlockSpec((tm, d), lambda i, j: (i, 0)),
            pl.BlockSpec((1, N_MOD, d), lambda i, j: (mod_row(i, tm), 0, 0)),
            pl.BlockSpec((1, d), lambda i, j: (0, 0)),
            pl.BlockSpec((None, tn, d), lambda i, j: (layer, j, 0)),
        ],
        out_specs=pl.BlockSpec((tm, tn), lambda i, j: (i, j)),
        out_shape=jax.ShapeDtypeStruct((r, P_W), F32),
        scratch_shapes=[pltpu.VMEM((tm, d), BF16)],
        compiler_params=_cparams(("parallel", "arbitrary"), VMEM_INPROJ_MIB),
        name="inproj",
    )(x, mod, g, w)


def _outproj_kernel(x_ref, yr_ref, yc_ref, ym_ref, mod_ref, g_ref, w_ref, o_ref):
    gain = g_ref[...] * mod_ref[0, 5:6, :]
    for rows in _half_rows(x_ref.shape[0]):
        y = _dot(yr_ref[rows, :], w_ref[0:RET_W, :])
        y += _dot(yc_ref[rows, :], w_ref[RET_W:RET_W + CONV_CH, :])
        y += _dot(ym_ref[rows, :], w_ref[RET_W + CONV_CH:, :])
        o_ref[rows, :] = y
        _postnorm_residual(o_ref, x_ref, gain, rows)


def _outproj(x, y_ret, y_conv, y_mla, mod, g, w, layer, mod_row, tm):
    r, d = x.shape
    return pl.pallas_call(
        _outproj_kernel,
        grid=(r // tm,),
        in_specs=[
            pl.BlockSpec((tm, d), lambda i: (i, 0)),
            pl.BlockSpec((tm, RET_W), lambda i: (i, 0)),
            pl.BlockSpec((tm, CONV_CH), lambda i: (i, 0)),
            pl.BlockSpec((tm, MLA_W), lambda i: (i, 0)),
            pl.BlockSpec((1, N_MOD, d), lambda i: (mod_row(i, tm), 0, 0)),
            pl.BlockSpec((1, d), lambda i: (0, 0)),
            pl.BlockSpec((None, d, d), lambda i: (layer, 0, 0), pipeline_mode=pl.Buffered(1)),
        ],
        out_specs=pl.BlockSpec((tm, d), lambda i: (i, 0)),
        out_shape=jax.ShapeDtypeStruct((r, d), F32),
        compiler_params=_cparams(("parallel",), VMEM_MATMUL_MIB),
        name="outproj",
    )(x, y_ret, y_conv, y_mla, mod, g, w)


def _ret_kernel(ql_ref, kl_ref, vl_ref, gl_ref, qc_ref, kc_ref, vc_ref, gc_ref,
                rc_ref, rs1_ref, rs2_ref, dm_ref, kf_ref, kb_ref, qf_ref, qb_ref, cd_ref, gn_ref,
                yl_ref, yc_ref, qr_ref, kr_ref, sf_ref, *, n_lat, n_ctx):
    c_len = RET_CHUNK
    lane = lax.broadcasted_iota(jnp.int32, (1, LANES), 1)
    head_masks = (lane < RET_DK, lane >= RET_DK)
    row = lax.broadcasted_iota(jnp.int32, (LANES, 2 * RET_DV), 0)
    col = lax.broadcasted_iota(jnp.int32, (LANES, 2 * RET_DV), 1)
    block_diag = (row < RET_DK) == (col < RET_DV)
    kf, kb, qf, qb, cd = kf_ref[0], kb_ref[0], qf_ref[0], qb_ref[0], cd_ref[0]
    gn = gn_ref[...]
    zero_state = jnp.zeros((LANES, 2 * RET_DV), F32)

    def kv_state(k, v16, dec):
        return jnp.where(block_diag, _dot_tn((k * dec).astype(BF16), v16), 0.0)

    def chunk_out(q, k, v16, g, sf16, sb16):
        k16 = k.astype(BF16)
        parts = []
        for e in range(2):
            q16 = jnp.where(head_masks[e], q, 0.0).astype(BF16)
            a = _dot_nt(q16, k16) * dm_ref[0, e]
            parts.append(_dot(a.astype(BF16), v16[:, e * RET_DV:(e + 1) * RET_DV]))
        o = jnp.concatenate(parts, axis=1)
        o = o + _dot((q * qf).astype(BF16), sf16) + _dot((q * qb).astype(BF16), sb16)
        normed = []
        for e in range(2):
            oe = o[:, e * RET_DV:(e + 1) * RET_DV]
            dlt = oe - jnp.mean(oe, axis=-1, keepdims=True)
            var = jnp.mean(dlt * dlt, axis=-1, keepdims=True)
            normed.append(dlt * lax.rsqrt(var + EPS))
        on = jnp.concatenate(normed, axis=1) * gn
        return ((g * _sigmoid(g)) * on).astype(BF16)

    cq = [qc_ref[c * c_len:(c + 1) * c_len, :] for c in range(n_ctx)]
    ck = [kc_ref[c * c_len:(c + 1) * c_len, :] for c in range(n_ctx)]
    cv = [vc_ref[c * c_len:(c + 1) * c_len, :].astype(BF16) for c in range(n_ctx)]
    kvf = [kv_state(ck[c], cv[c], kf) for c in range(n_ctx)]
    kvb = [kv_state(ck[c], cv[c], kb) for c in range(n_ctx)]
    sf_list = [zero_state]
    for c in range(n_ctx):
        sf_list.append(cd * sf_list[c] + kvf[c])
    sb_list = [zero_state] * (n_ctx + 1)
    for c in range(n_ctx - 1, -1, -1):
        sb_list[c] = cd * sb_list[c + 1] + kvb[c]
    for c in range(n_ctx):
        yc_ref[c * c_len:(c + 1) * c_len, :] = chunk_out(
            cq[c], ck[c], cv[c], gc_ref[c * c_len:(c + 1) * c_len, :],
            sf_list[c].astype(BF16), sb_list[c + 1].astype(BF16))
    s0f, s0b = sf_list[n_ctx], sb_list[0]

    def fwd(c, sf):
        r0 = pl.multiple_of(c * c_len, c_len)
        rows = pl.ds(r0, c_len)
        rc, rs1, rs2 = rc_ref[rows, :], rs1_ref[rows, :], rs2_ref[rows, :]
        q = _rope(ql_ref[rows, :], rc, rs1, rs2)
        k = _rope(kl_ref[rows, :], rc, rs1, rs2)
        qr_ref[rows, :] = q
        kr_ref[rows, :] = k
        sf_ref[c] = sf.astype(BF16)
        return cd * sf + kv_state(k, vl_ref[rows, :].astype(BF16), kf)

    lax.fori_loop(0, n_lat, fwd, s0f, unroll=RET_UNROLL)

    def bwd(i, sb):
        c = n_lat - 1 - i
        r0 = pl.multiple_of(c * c_len, c_len)
        rows = pl.ds(r0, c_len)
        q, k = qr_ref[rows, :], kr_ref[rows, :]
        v16 = vl_ref[rows, :].astype(BF16)
        yl_ref[rows, :] = chunk_out(q, k, v16, gl_ref[rows, :], sf_ref[c], sb.astype(BF16))
        return cd * sb + kv_state(k, v16, kb)

    lax.fori_loop(0, n_lat, bwd, s0b, unroll=RET_UNROLL)


def _retention(p_lat, p_ctx, tabs, gn, batch):
    t, lc = SEQ, CTX_LEN
    n_pairs = RET_HEADS // 2
    n_lat, n_ctx = t // RET_CHUNK, lc // RET_CHUNK
    qw, vw = 2 * RET_DK, 2 * RET_DV

    def pspec(rows, width, col0):
        return pl.BlockSpec((rows, width), lambda b, hp: (b, col0 // width + hp))

    tab3 = pl.BlockSpec((1, LANES, LANES), lambda b, hp: (hp, 0, 0))
    rope_spec = pl.BlockSpec((t, LANES), lambda b, hp: (0, 0))
    return pl.pallas_call(
        functools.partial(_ret_kernel, n_lat=n_lat, n_ctx=n_ctx),
        grid=(batch, n_pairs),
        in_specs=[
            pspec(t, qw, P_RQ), pspec(t, qw, P_RK), pspec(t, vw, P_RV), pspec(t, vw, P_RG),
            pspec(lc, qw, P_RQ), pspec(lc, qw, P_RK), pspec(lc, vw, P_RV), pspec(lc, vw, P_RG),
            rope_spec, rope_spec, rope_spec,
            pl.BlockSpec((1, 2, LANES, LANES), lambda b, hp: (hp, 0, 0, 0)),
            tab3, tab3, tab3, tab3,
            pl.BlockSpec((1, LANES, vw), lambda b, hp: (hp, 0, 0)),
            pl.BlockSpec((1, vw), lambda b, hp: (0, hp)),
        ],
        out_specs=[
            pl.BlockSpec((t, vw), lambda b, hp: (b, hp)),
            pl.BlockSpec((lc, vw), lambda b, hp: (b, hp)),
        ],
        out_shape=[
            jax.ShapeDtypeStruct((batch * t, RET_W), BF16),
            jax.ShapeDtypeStruct((batch * lc, RET_W), BF16),
        ],
        scratch_shapes=[
            pltpu.VMEM((t, LANES), F32),
            pltpu.VMEM((t, LANES), F32),
            pltpu.VMEM((n_lat, LANES, vw), BF16),
        ],
        compiler_params=_cparams(("parallel", "parallel"), VMEM_MIX_MIB),
        name="retention",
    )(p_lat, p_lat, p_lat, p_lat, p_ctx, p_ctx, p_ctx, p_ctx,
      tabs["ret_c"], tabs["ret_s1"], tabs["ret_s2"],
      tabs["dm"], tabs["kf"], tabs["kb"], tabs["qf"], tabs["qb"], tabs["cd"], gn)


CONV_HALF = CONV_W // 2
CONV_MARGIN = -(-CONV_HALF // SUBLANES) * SUBLANES


def _conv_kernel(glu_ref, dw_ref, misc_ref, pw_ref, o_ref, pad_ref, *, t, rt):
    half = CONV_HALF
    lo = CONV_MARGIN
    zeros = jnp.zeros((lo, CONV_CH), F32)
    pad_ref[0:lo, :] = zeros
    pad_ref[lo + t:lo + t + lo, :] = zeros

    def glu(r, carry):
        rows = pl.ds(pl.multiple_of(r * rt, rt), rt)
        a = glu_ref[rows, 0:CONV_CH]
        b = glu_ref[rows, CONV_CH:2 * CONV_CH]
        pad_ref[pl.ds(pl.multiple_of(r * rt + lo, SUBLANES), rt), :] = a * _sigmoid(b)
        return carry

    lax.fori_loop(0, t // rt, glu, 0)

    def conv(r, carry):
        base = pl.multiple_of(r * rt, rt)
        win_rows = rt + 2 * lo
        cg_w = LANES
        accs = []
        for cg in range(CONV_CH // cg_w):
            cols = slice(cg * cg_w, (cg + 1) * cg_w)
            win = pad_ref[pl.ds(base, win_rows), cols]
            acc = jnp.zeros((rt, cg_w), F32) + misc_ref[0:1, cols]
            for phase in range(SUBLANES):
                shifted = win if phase == 0 else pltpu.roll(win, win_rows - phase, 0)
                for off in range(phase, lo + half + 1, SUBLANES):
                    k = off - (lo - half)
                    if 0 <= k < CONV_W:
                        a0 = off - phase
                        acc = acc + shifted[a0:a0 + rt, :] * dw_ref[k:k + 1, cols]
            accs.append(acc)
        acc = jnp.concatenate(accs, axis=1)
        dlt = acc - jnp.mean(acc, axis=-1, keepdims=True)
        var = jnp.mean(dlt * dlt, axis=-1, keepdims=True)
        y = dlt * lax.rsqrt(var + EPS) * misc_ref[1:2, :] + misc_ref[2:3, :]
        y = y * _sigmoid(y)
        o_ref[pl.ds(base, rt), :] = _dot(y.astype(BF16), pw_ref[...]).astype(BF16)
        return carry

    lax.fori_loop(0, t // rt, conv, 0, unroll=CONV_UNROLL)


def _conv(p, dw, misc, pw, batch, t):
    rt = CONV_RT
    return pl.pallas_call(
        functools.partial(_conv_kernel, t=t, rt=rt),
        grid=(batch,),
        in_specs=[
            pl.BlockSpec((t, 2 * CONV_CH), lambda b: (b, P_GLU // (2 * CONV_CH))),
            pl.BlockSpec((CONV_W, CONV_CH), lambda b: (0, 0)),
            pl.BlockSpec((3, CONV_CH), lambda b: (0, 0)),
            pl.BlockSpec((CONV_CH, CONV_CH), lambda b: (0, 0)),
        ],
        out_specs=pl.BlockSpec((t, CONV_CH), lambda b: (b, 0)),
        out_shape=jax.ShapeDtypeStruct((batch * t, CONV_CH), BF16),
        scratch_shapes=[pltpu.VMEM((t + 2 * CONV_MARGIN, CONV_CH), F32)],
        compiler_params=_cparams(("parallel",), VMEM_SMALL_MIB),
        name="conv",
    )(p, dw, misc, pw)


def _mla_proj_kernel(cq_ref, ckv_ref, kr_ref, nq_ref, nkv_ref, wq_ref, wkv_ref, *rest, rope):
    if rope:
        rc_ref, rs1_ref, rs2_ref, q_ref, k_ref, v_ref = rest
        rot = lambda z: _rope(z, rc_ref[...], rs1_ref[...], rs2_ref[...])
    else:
        q_ref, k_ref, v_ref = rest
        rot = lambda z: z
    qq = _dot(_rms(cq_ref[...], nq_ref[...]).astype(BF16), wq_ref[...]) * (MLA_SCALE * LOG2_E)
    kv = _dot(_rms(ckv_ref[...], nkv_ref[...]).astype(BF16), wkv_ref[...])
    kr = rot(kr_ref[...]).astype(BF16)
    hw = MLA_NOPE + LANES
    for h in range(MLA_HEADS):
        q_ref[:, h * hw:h * hw + MLA_NOPE] = qq[:, h * MLA_NOPE:(h + 1) * MLA_NOPE].astype(BF16)
        qr = qq[:, MLA_HEADS * MLA_NOPE + h * LANES:MLA_HEADS * MLA_NOPE + (h + 1) * LANES]
        q_ref[:, h * hw + MLA_NOPE:(h + 1) * hw] = rot(qr).astype(BF16)
        k_ref[:, h * hw:h * hw + MLA_NOPE] = kv[:, 2 * h * LANES:(2 * h + 1) * LANES].astype(BF16)
        k_ref[:, h * hw + MLA_NOPE:(h + 1) * hw] = kr
        v_ref[:, h * MLA_DV:(h + 1) * MLA_DV] = kv[:, (2 * h + 1) * LANES:(2 * h + 2) * LANES].astype(BF16)


def _mla_proj(p, nq, nkv, wq, wkv, rope_tabs, tm):
    r = p.shape[0]
    hw = MLA_NOPE + LANES
    rope = rope_tabs is not None
    in_specs = [
        pl.BlockSpec((tm, MLA_Q_RANK), lambda i: (i, P_CQ // MLA_Q_RANK)),
        pl.BlockSpec((tm, MLA_KV_RANK), lambda i: (i, P_CKV // MLA_KV_RANK)),
        pl.BlockSpec((tm, LANES), lambda i: (i, P_KR // LANES)),
        pl.BlockSpec((1, MLA_Q_RANK), lambda i: (0, 0)),
        pl.BlockSpec((1, MLA_KV_RANK), lambda i: (0, 0)),
        pl.BlockSpec(wq.shape, lambda i: (0, 0)),
        pl.BlockSpec(wkv.shape, lambda i: (0, 0)),
    ]
    args = [p, p, p, nq, nkv, wq, wkv]
    if rope:
        n_pos = SEQ // tm
        in_specs += [pl.BlockSpec((tm, LANES), lambda i: (i % n_pos, 0))] * 3
        args += list(rope_tabs)
    return pl.pallas_call(
        functools.partial(_mla_proj_kernel, rope=rope),
        grid=(r // tm,),
        in_specs=in_specs,
        out_specs=[
            pl.BlockSpec((tm, MLA_HEADS * hw), lambda i: (i, 0)),
            pl.BlockSpec((tm, MLA_HEADS * hw), lambda i: (i, 0)),
            pl.BlockSpec((tm, MLA_W), lambda i: (i, 0)),
        ],
        out_shape=[
            jax.ShapeDtypeStruct((r, MLA_HEADS * hw), BF16),
            jax.ShapeDtypeStruct((r, MLA_HEADS * hw), BF16),
            jax.ShapeDtypeStruct((r, MLA_W), BF16),
        ],
        compiler_params=_cparams(("parallel",), VMEM_MIX_MIB),
        name="mla_proj",
    )(*args)


def _attn_kernel(q_ref, kl_ref, kc_ref, vl_ref, vc_ref, o_ref, *, n_heads, sub):
    hw = MLA_NOPE + LANES
    for e in range(n_heads):
        kcols = slice(e * hw, (e + 1) * hw)
        vcols = slice(e * MLA_DV, (e + 1) * MLA_DV)
        for i in range(q_ref.shape[0] // sub):
            rows = slice(i * sub, (i + 1) * sub)
            q = q_ref[rows, kcols]
            sl = _dot_nt(q, kl_ref[:, kcols])
            sc = _dot_nt(q, kc_ref[:, kcols])
            m = jnp.maximum(jnp.max(sl, axis=-1, keepdims=True), jnp.max(sc, axis=-1, keepdims=True))
            el = jnp.exp2(sl - m)
            ec = jnp.exp2(sc - m)
            den = jnp.sum(el, axis=-1, keepdims=True) + jnp.sum(ec, axis=-1, keepdims=True)
            o = _dot(el.astype(BF16), vl_ref[:, vcols]) + _dot(ec.astype(BF16), vc_ref[:, vcols])
            o_ref[rows, vcols] = (o / den).astype(BF16)


def _attn(q, k_lat, k_ctx, v_lat, v_ctx, batch, n_heads, sub):
    hw = n_heads * (MLA_NOPE + LANES)
    vw = n_heads * MLA_DV
    return pl.pallas_call(
        functools.partial(_attn_kernel, n_heads=n_heads, sub=sub),
        grid=(batch, MLA_HEADS // n_heads),
        in_specs=[
            pl.BlockSpec((SEQ, hw), lambda b, h: (b, h)),
            pl.BlockSpec((SEQ, hw), lambda b, h: (b, h)),
            pl.BlockSpec((CTX_LEN, hw), lambda b, h: (b, h)),
            pl.BlockSpec((SEQ, vw), lambda b, h: (b, h)),
            pl.BlockSpec((CTX_LEN, vw), lambda b, h: (b, h)),
        ],
        out_specs=pl.BlockSpec((SEQ, vw), lambda b, h: (b, h)),
        out_shape=jax.ShapeDtypeStruct((batch * SEQ, MLA_W), BF16),
        compiler_params=_cparams(("parallel", "parallel"), VMEM_MIX_MIB),
        name="mla_attn",
    )(q, k_lat, k_ctx, v_lat, v_ctx)


def _attn_ctx_kernel(q_ref, k_ref, v_ref, o_ref):
    s = _dot_nt(q_ref[...], k_ref[...])
    e = jnp.exp2(s - jnp.max(s, axis=-1, keepdims=True))
    o = _dot(e.astype(BF16), v_ref[...])
    o_ref[...] = (o / jnp.sum(e, axis=-1, keepdims=True)).astype(BF16)


def _attn_ctx(q, k, v, batch):
    hw = MLA_NOPE + LANES
    return pl.pallas_call(
        _attn_ctx_kernel,
        grid=(batch, MLA_HEADS),
        in_specs=[
            pl.BlockSpec((CTX_LEN, hw), lambda b, h: (b, h)),
            pl.BlockSpec((CTX_LEN, hw), lambda b, h: (b, h)),
            pl.BlockSpec((CTX_LEN, MLA_DV), lambda b, h: (b, h)),
        ],
        out_specs=pl.BlockSpec((CTX_LEN, MLA_DV), lambda b, h: (b, h)),
        out_shape=jax.ShapeDtypeStruct((batch * CTX_LEN, MLA_W), BF16),
        compiler_params=_cparams(("parallel", "parallel"), VMEM_SMALL_MIB),
        name="mla_attn_ctx",
    )(q, k, v)


def _tables():
    rows = SEQ // GRID_W
    quarter = ROPE_DIM // 4
    inv = ROPE_BASE ** (-jnp.arange(quarter, dtype=F32) / quarter)
    r_idx = jnp.repeat(jnp.arange(rows, dtype=F32), GRID_W)
    c_idx = jnp.tile(jnp.arange(GRID_W, dtype=F32), rows)
    ang = jnp.concatenate([r_idx[:, None] * inv, c_idx[:, None] * inv], axis=-1)
    cos, sin = jnp.cos(ang), jnp.sin(ang)
    z = jnp.zeros_like(cos)
    tabs = {
        "ret_c": jnp.concatenate([cos, cos, cos, cos], axis=1),
        "ret_s1": jnp.concatenate([-sin, z, -sin, z], axis=1),
        "ret_s2": jnp.concatenate([z, sin, z, sin], axis=1),
        "mla_c": jnp.concatenate([cos, cos, z, z], axis=1),
        "mla_s1": jnp.concatenate([-sin, z, z, z], axis=1),
        "mla_s2": jnp.concatenate([z, sin, z, z], axis=1),
    }
    c_len = RET_CHUNK
    log_g = jnp.log1p(-jnp.exp2(-5.0 - jnp.arange(RET_HEADS, dtype=F32)))
    i = jnp.arange(c_len, dtype=F32)
    k_scale = RET_DK ** -0.5
    lg = log_g[:, None, None]
    dm = jnp.exp(lg * jnp.abs(i[:, None] - i[None, :])) * k_scale
    tabs["dm"] = dm.reshape(RET_HEADS // 2, 2, c_len, c_len)

    def lanes_by_head(per_head_rows):
        t = jnp.repeat(per_head_rows[:, :, None], RET_DK, axis=2)
        t = t.reshape(RET_HEADS // 2, 2, c_len, RET_DK)
        return jnp.concatenate([t[:, 0], t[:, 1]], axis=-1)

    tabs["kf"] = lanes_by_head(jnp.exp(log_g[:, None] * (c_len - 1.0 - i)) * k_scale)
    tabs["kb"] = lanes_by_head(jnp.exp(log_g[:, None] * i) * k_scale)
    tabs["qf"] = lanes_by_head(jnp.exp(log_g[:, None] * (i + 1.0)))
    tabs["qb"] = lanes_by_head(jnp.exp(log_g[:, None] * (c_len - i)))
    cdec = jnp.exp(log_g * c_len).reshape(RET_HEADS // 2, 2)
    cd = jnp.repeat(cdec[:, :, None], RET_DK, axis=2).reshape(RET_HEADS // 2, 2 * RET_DK, 1)
    tabs["cd"] = jnp.broadcast_to(cd, (RET_HEADS // 2, 2 * RET_DK, 2 * RET_DV))
    return tabs


def _cast_kernel(x_ref, o_ref):
    o_ref[...] = x_ref[...].astype(BF16)


def _cast_bf16(w, tr):
    depth, r, c = w.shape
    return pl.pallas_call(
        _cast_kernel,
        grid=(depth, r // tr),
        in_specs=[pl.BlockSpec((1, tr, c), lambda l, i: (l, i, 0))],
        out_specs=pl.BlockSpec((1, tr, c), lambda l, i: (l, i, 0)),
        out_shape=jax.ShapeDtypeStruct(w.shape, BF16),
        compiler_params=_cparams(("parallel", "parallel"), VMEM_MIX_MIB),
        name="cast",
    )(w)


_W_IN_SRC = ((2304, 1024), (3328, 512), (768, 768), (1536, 768), (0, 384), (384, 384), (3840, 256),
             (4096, 64))


def _prep_w_in_kernel(w_ref, o_ref):
    dst = 0
    for src, width in _W_IN_SRC:
        o_ref[0, dst:dst + width, :] = w_ref[0, src:src + width, :].astype(BF16)
        dst += width
    o_ref[0, dst:, :] = jnp.zeros((P_W - dst, o_ref.shape[2]), BF16)


def _prep_w_in(w):
    depth, d, n = w.shape
    wt = jnp.swapaxes(w, 1, 2)
    tk = W_IN_PREP_COLS
    return pl.pallas_call(
        _prep_w_in_kernel,
        grid=(depth, d // tk),
        in_specs=[pl.BlockSpec((1, n, tk), lambda l, i: (l, 0, i))],
        out_specs=pl.BlockSpec((1, P_W, tk), lambda l, i: (l, 0, i)),
        out_shape=jax.ShapeDtypeStruct((depth, P_W, d), BF16),
        compiler_params=_cparams(("parallel", "parallel"), VMEM_MIX_MIB),
        name="prep_w_in",
    )(wt)


def _prep_w_uq(w):
    w = w.reshape(MLA_Q_RANK, MLA_HEADS, MLA_NOPE + MLA_ROPE)
    nope = w[:, :, :MLA_NOPE].reshape(MLA_Q_RANK, MLA_HEADS * MLA_NOPE)
    rope = jnp.pad(w[:, :, MLA_NOPE:], ((0, 0), (0, 0), (0, LANES - MLA_ROPE)))
    return jnp.concatenate([nope, rope.reshape(MLA_Q_RANK, MLA_HEADS * LANES)], axis=1).astype(BF16)


def _lat_mod_row(i, tm):
    return (i * tm) // SEQ


def _ctx_mod_row(i, tm):
    return CTX_MOD_ROW


def kernel(x, c, ctx, c_ctx, w_ada, b_ada, norm_g, w_ffa_in, w_ffa_out, w_ffb_in, w_ffb_out,
           w_in, w_out, ret_gn, cv_dw, cv_dw_b, cv_ln_g, cv_ln_b, cv_pw,
           mla_q_norm, w_uq, mla_kv_norm, w_ukv):
    batch, seq, d = x.shape
    assert (seq, d, ctx.shape[1]) == (SEQ, D_MODEL, CTX_LEN)
    xl = x.reshape(batch * seq, d)
    xc = ctx.reshape(batch * CTX_LEN, d)
    s_in = jnp.concatenate([c, c_ctx[None, :], jnp.zeros((ROWS_PAD - batch - 1, d), F32)], axis=0)
    mod_all = _ada(s_in, w_ada, b_ada).reshape(DEPTH, ROWS_PAD, N_MOD, d)
    tabs = _tables()
    mla_rope = (tabs["mla_c"], tabs["mla_s1"], tabs["mla_s2"])

    wi = _prep_w_in(w_in)
    wo = _cast_bf16(w_out, W_OUT_CAST_ROWS)

    lat = (_lat_mod_row, FFN_TM)
    cx = (_ctx_mod_row, FFN_TM)
    for l in range(DEPTH):
        last = l == DEPTH - 1
        mod = mod_all[l]
        ng = norm_g[l]
        wq = _prep_w_uq(w_uq[l])
        wkv = w_ukv[l].astype(BF16)
        nq, nkv = mla_q_norm[l][None, :], mla_kv_norm[l][None, :]
        gn = ret_gn[l][None, :]
        conv_misc = jnp.stack([cv_dw_b[l], cv_ln_g[l], cv_ln_b[l]], axis=0)
        pw = cv_pw[l].astype(BF16)

        xl = _ffn(xl, mod, ng[0:2], w_ffa_in, w_ffa_out, l, 0, *lat, FFN_TF)
        xc = _ffn(xc, mod, ng[0:2], w_ffa_in, w_ffa_out, l, 0, *cx, FFN_TF)

        p_lat = _inproj(xl, mod, ng[2:3], wi, l, _lat_mod_row, INPROJ_TM, INPROJ_TN)
        p_ctx = _inproj(xc, mod, ng[2:3], wi, l, _ctx_mod_row, INPROJ_TM, INPROJ_TN)

        y_ret_l, y_ret_c = _retention(p_lat, p_ctx, tabs, gn, batch)
        y_conv_l = _conv(p_lat, cv_dw[l], conv_misc, pw, batch, SEQ)
        q_l, k_l, v_l = _mla_proj(p_lat, nq, nkv, wq, wkv, mla_rope, MLA_PROJ_TM)
        q_c, k_c, v_c = _mla_proj(p_ctx, nq, nkv, wq, wkv, None, MLA_PROJ_TM)
        y_mla_l = _attn(q_l, k_l, k_c, v_l, v_c, batch, ATTN_HEADS, ATTN_SUB)

        xl = _outproj(xl, y_ret_l, y_conv_l, y_mla_l, mod, ng[3:4], wo, l, _lat_mod_row, OUTPROJ_TM)
        xl = _ffn(xl, mod, ng[4:6], w_ffb_in, w_ffb_out, l, 6, *lat, FFN_TF)
        if not last:
            y_conv_c = _conv(p_ctx, cv_dw[l], conv_misc, pw, batch, CTX_LEN)
            y_mla_c = _attn_ctx(q_c, k_c, v_c, batch)
            xc = _outproj(xc, y_ret_c, y_conv_c, y_mla_c, mod, ng[3:4], wo, l, _ctx_mod_row, OUTPROJ_TM)
            xc = _ffn(xc, mod, ng[4:6], w_ffb_in, w_ffb_out, l, 6, *cx, FFN_TF)
    return xl.reshape(batch, seq, d)
```

```python
import functools

import jax
import jax.numpy as jnp
import numpy as np
from jax import lax
from jax.experimental import pallas as pl
from jax.experimental.pallas import tpu as pltpu

F32 = jnp.float32
BF16 = jnp.bfloat16

D_MODEL = 2048
SEQ = 2048
DEPTH = 2
GRID_W = 64
CTX_LEN = 256
D_FF = 5632
FFN_RES = 0.5
N_MOD = 9
ROPE_DIM = 64
ROPE_BASE = 10000.0
EPS = 1e-6
RET_HEADS = 6
RET_DK = 64
RET_DV = 128
RET_CHUNK = 128
CONV_CH = 512
CONV_W = 31
MLA_HEADS = 6
MLA_Q_RANK = 512
MLA_KV_RANK = 256
MLA_NOPE = 128
MLA_ROPE = 64
MLA_DV = 128
MLA_SCALE = (MLA_NOPE + MLA_ROPE) ** -0.5
LOG2_E = 1.4426950408889634
RET_W = RET_HEADS * RET_DV
MLA_W = MLA_HEADS * MLA_DV

LANES = 128
SUBLANES = 8
MIB = 1024 * 1024
VMEM_V7X_MIB = 64
VMEM_MATMUL_MIB = VMEM_V7X_MIB - 4
VMEM_INPROJ_MIB = 56
VMEM_MIX_MIB = 48
VMEM_SMALL_MIB = 40

FFN_TM, FFN_TF = 1024, 512
INPROJ_TM, INPROJ_TN = 1024, 1408
OUTPROJ_TM = 1024
MLA_PROJ_TM = 1024
ATTN_HEADS, ATTN_SUB = 2, 512
CONV_RT, CONV_UNROLL = 64, 4
ADA_TN = 2048
W_OUT_CAST_ROWS = 1024
W_IN_PREP_COLS = 512
ROW_CHUNK = 128
RET_UNROLL = 16

P_GLU, P_CQ, P_RV, P_RG, P_RQ, P_RK, P_CKV, P_KR = 0, 1024, 1536, 2304, 3072, 3456, 3840, 4096
P_W = 4224
CTX_MOD_ROW = 4
ROWS_PAD = 8


def _cparams(sem, vmem_mib):
    return pltpu.CompilerParams(dimension_semantics=sem, vmem_limit_bytes=int(vmem_mib * MIB))


def _sigmoid(x):
    return 1.0 / (1.0 + jnp.exp(-x))


def _rms(x, g):
    ms = jnp.mean(x * x, axis=-1, keepdims=True)
    return x * lax.rsqrt(ms + EPS) * g


def _dot(a, b):
    return jnp.dot(a, b, preferred_element_type=F32)


def _half_rows(tm):
    return (slice(0, tm // 2), slice(tm // 2, tm))


def _prenorm_modulate(x_ref, h_ref, gain_row, shift_row, rows):
    for r0 in range(rows.start, rows.stop, ROW_CHUNK):
        x = x_ref[r0:r0 + ROW_CHUNK, :]
        r = lax.rsqrt(jnp.mean(x * x, axis=-1, keepdims=True) + EPS)
        h_ref[r0:r0 + ROW_CHUNK, :] = (x * r * gain_row + shift_row).astype(BF16)


def _postnorm_residual(y_ref, x_ref, gain_row, rows, on_chunk_done=None):
    for r0 in range(rows.start, rows.stop, ROW_CHUNK):
        y = y_ref[r0:r0 + ROW_CHUNK, :]
        r = lax.rsqrt(jnp.mean(y * y, axis=-1, keepdims=True) + EPS)
        y_ref[r0:r0 + ROW_CHUNK, :] = x_ref[r0:r0 + ROW_CHUNK, :] + y * r * gain_row
        if on_chunk_done is not None:
            on_chunk_done(r0 // ROW_CHUNK)


def _dot_nt(a, b):
    return lax.dot_general(a, b, (((1,), (1,)), ((), ())), preferred_element_type=F32)


def _dot_tn(a, b):
    return lax.dot_general(a, b, (((0,), (0,)), ((), ())), preferred_element_type=F32)


def _rope(x, c, s1, s2):
    return x * c + pltpu.roll(x, 96, 1) * s1 + pltpu.roll(x, 32, 1) * s2


def _ada_kernel(s_ref, w_ref, b_ref, o_ref):
    s = s_ref[...]
    s = s * _sigmoid(s)
    o_ref[0] = _dot(s.astype(BF16), w_ref[0].astype(BF16)) + b_ref[0]


def _ada(s_in, w_ada, b_ada):
    depth, d, n = w_ada.shape
    tn = ADA_TN
    return pl.pallas_call(
        _ada_kernel,
        grid=(depth, n // tn),
        in_specs=[
            pl.BlockSpec((ROWS_PAD, d), lambda l, j: (0, 0)),
            pl.BlockSpec((1, d, tn), lambda l, j: (l, 0, j)),
            pl.BlockSpec((1, 1, tn), lambda l, j: (l, 0, j)),
        ],
        out_specs=pl.BlockSpec((1, ROWS_PAD, tn), lambda l, j: (l, 0, j)),
        out_shape=jax.ShapeDtypeStruct((depth, ROWS_PAD, n), F32),
        compiler_params=_cparams(("parallel", "parallel"), VMEM_MIX_MIB),
        name="ada",
    )(s_in, w_ada, b_ada.reshape(depth, 1, n))


def _ffn_kernel(x_ref, mod_ref, g_ref, wa_ref, wu_ref, wo_ref, o_hbm, acc_ref, h_ref, sem, *, k0, nj):
    i = pl.program_id(0)
    j = pl.program_id(1)
    tm = acc_ref.shape[0]
    halves = _half_rows(tm)

    def out_copy(c):
        src = acc_ref.at[pl.ds(c * ROW_CHUNK, ROW_CHUNK), :]
        dst = o_hbm.at[pl.ds(pl.multiple_of(i * tm + c * ROW_CHUNK, ROW_CHUNK), ROW_CHUNK), :]
        return pltpu.make_async_copy(src, dst, sem.at[c])

    def chunk_update(rows, wa, wu, wo):
        h = h_ref[rows, :]
        a = _dot(h, wa)
        u = _dot(h, wu)
        act = (a * _sigmoid(a) * u).astype(BF16)
        return _dot(act, wo)

    def weights():
        return wa_ref[...].astype(BF16), wu_ref[...].astype(BF16), wo_ref[...].astype(BF16)

    @pl.when(j == 0)
    def _():
        w = weights()
        gain = g_ref[0:1, :] * (1.0 + mod_ref[0, k0 + 1:k0 + 2, :])
        shift = mod_ref[0, k0:k0 + 1, :]
        for rows in halves:
            _prenorm_modulate(x_ref, h_ref, gain, shift, rows)
            acc_ref[rows, :] = chunk_update(rows, *w)

    @pl.when((j > 0) & (j < nj - 1))
    def _():
        acc_ref[...] += chunk_update(slice(0, tm), *weights())

    @pl.when(j == nj - 1)
    def _():
        w = weights()
        gain = g_ref[1:2, :] * (FFN_RES * mod_ref[0, k0 + 2:k0 + 3, :])
        for rows in halves:
            acc_ref[rows, :] += chunk_update(rows, *w)
            _postnorm_residual(acc_ref, x_ref, gain, rows, lambda c: out_copy(c).start())
        for c in range(tm // ROW_CHUNK):
            out_copy(c).wait()


def _ffn(x, mod, g2, w_in, w_out, layer, k0, mod_row, tm, tf):
    r, d = x.shape
    nj = D_FF // tf
    assert nj >= 2 and tm % (2 * ROW_CHUNK) == 0
    return pl.pallas_call(
        functools.partial(_ffn_kernel, k0=k0, nj=nj),
        grid=(r // tm, nj),
        in_specs=[
            pl.BlockSpec((tm, d), lambda i, j: (i, 0)),
            pl.BlockSpec((None, 1, N_MOD, d), lambda i, j: (layer, mod_row(i, tm), 0, 0)),
            pl.BlockSpec((2, d), lambda i, j: (0, 0)),
            pl.BlockSpec((None, d, tf), lambda i, j: (layer, 0, j)),
            pl.BlockSpec((None, d, tf), lambda i, j: (layer, 0, nj + j)),
            pl.BlockSpec((None, tf, d), lambda i, j: (layer, j, 0)),
        ],
        out_specs=pl.BlockSpec(memory_space=pl.ANY),
        out_shape=jax.ShapeDtypeStruct((r, d), F32),
        scratch_shapes=[
            pltpu.VMEM((tm, d), F32),
            pltpu.VMEM((tm, d), BF16),
            pltpu.SemaphoreType.DMA((tm // ROW_CHUNK,)),
        ],
        compiler_params=_cparams(("arbitrary", "arbitrary"), VMEM_MATMUL_MIB),
        name="ffn",
    )(x, mod, g2, w_in, w_in, w_out)


def _inproj_kernel(x_ref, mod_ref, g_ref, w_ref, o_ref, h_ref):
    j = pl.program_id(1)

    @pl.when(j == 0)
    def _():
        gain = g_ref[...] * (1.0 + mod_ref[0, 4:5, :])
        shift = mod_ref[0, 3:4, :]
        for rows in _half_rows(x_ref.shape[0]):
            _prenorm_modulate(x_ref, h_ref, gain, shift, rows)
            o_ref[rows, :] = _dot_nt(h_ref[rows, :], w_ref[...])

    @pl.when(j > 0)
    def _():
        o_ref[...] = _dot_nt(h_ref[...], w_ref[...])


def _inproj(x, mod, g, w, layer, mod_row, tm, tn):
    r, d = x.shape
    return pl.pallas_call(
        _inproj_kernel,
        grid=(r // tm, P_W // tn),
        in_specs=[
            pl.BlockSpec((tm, d), lambda i, j: (i, 0)),
            pl.BlockSpec((None, 1, N_MOD, d), lambda i, j: (layer, mod_row(i, tm), 0, 0)),
            pl.BlockSpec((1, d), lambda i, j: (0, 0)),
            pl.BlockSpec((None, tn, d), lambda i, j: (layer, j, 0)),
        ],
        out_specs=pl.BlockSpec((tm, tn), lambda i, j: (i, j)),
        out_shape=jax.ShapeDtypeStruct((r, P_W), F32),
        scratch_shapes=[pltpu.VMEM((tm, d), BF16)],
        compiler_params=_cparams(("parallel", "arbitrary"), VMEM_INPROJ_MIB),
        name="inproj",
    )(x, mod, g, w)


def _outproj_kernel(x_ref, yr_ref, yc_ref, ym_ref, mod_ref, g_ref, w_ref, o_ref):
    gain = g_ref[...] * mod_ref[0, 5:6, :]
    for rows in _half_rows(x_ref.shape[0]):
        y = _dot(yr_ref[rows, :], w_ref[0:RET_W, :])
        y += _dot(yc_ref[rows, :], w_ref[RET_W:RET_W + CONV_CH, :])
        y += _dot(ym_ref[rows, :], w_ref[RET_W + CONV_CH:, :])
        o_ref[rows, :] = y
        _postnorm_residual(o_ref, x_ref, gain, rows)


def _outproj(x, y_ret, y_conv, y_mla, mod, g, w, layer, mod_row, tm):
    r, d = x.shape
    return pl.pallas_call(
        _outproj_kernel,
        grid=(r // tm,),
        in_specs=[
            pl.BlockSpec((tm, d), lambda i: (i, 0)),
            pl.BlockSpec((tm, RET_W), lambda i: (i, 0)),
            pl.BlockSpec((tm, CONV_CH), lambda i: (i, 0)),
            pl.BlockSpec((tm, MLA_W), lambda i: (i, 0)),
            pl.BlockSpec((None, 1, N_MOD, d), lambda i: (layer, mod_row(i, tm), 0, 0)),
            pl.BlockSpec((1, d), lambda i: (0, 0)),
            pl.BlockSpec((None, d, d), lambda i: (layer, 0, 0), pipeline_mode=pl.Buffered(1)),
        ],
        out_specs=pl.BlockSpec((tm, d), lambda i: (i, 0)),
        out_shape=jax.ShapeDtypeStruct((r, d), F32),
        compiler_params=_cparams(("parallel",), VMEM_MATMUL_MIB),
        name="outproj",
    )(x, y_ret, y_conv, y_mla, mod, g, w)


def _ret_kernel(ql_ref, kl_ref, vl_ref, gl_ref, qc_ref, kc_ref, vc_ref, gc_ref,
                rc_ref, rs1_ref, rs2_ref, dm_ref, kf_ref, kb_ref, qf_ref, qb_ref, cd_ref, gn_ref,
                yl_ref, yc_ref, qr_ref, kr_ref, sf_ref, *, n_lat, n_ctx):
    c_len = RET_CHUNK
    lane = lax.broadcasted_iota(jnp.int32, (1, LANES), 1)
    head_masks = (lane < RET_DK, lane >= RET_DK)
    row = lax.broadcasted_iota(jnp.int32, (LANES, 2 * RET_DV), 0)
    col = lax.broadcasted_iota(jnp.int32, (LANES, 2 * RET_DV), 1)
    block_diag = (row < RET_DK) == (col < RET_DV)
    kf, kb, qf, qb, cd = kf_ref[0], kb_ref[0], qf_ref[0], qb_ref[0], cd_ref[0]
    gn = gn_ref[...]
    zero_state = jnp.zeros((LANES, 2 * RET_DV), F32)

    def kv_state(k, v16, dec):
        return jnp.where(block_diag, _dot_tn((k * dec).astype(BF16), v16), 0.0)

    def chunk_out(q, k, v16, g, sf16, sb16):
        k16 = k.astype(BF16)
        parts = []
        for e in range(2):
            q16 = jnp.where(head_masks[e], q, 0.0).astype(BF16)
            a = _dot_nt(q16, k16) * dm_ref[0, e]
            parts.append(_dot(a.astype(BF16), v16[:, e * RET_DV:(e + 1) * RET_DV]))
        o = jnp.concatenate(parts, axis=1)
        o = o + _dot((q * qf).astype(BF16), sf16) + _dot((q * qb).astype(BF16), sb16)
        normed = []
        for e in range(2):
            oe = o[:, e * RET_DV:(e + 1) * RET_DV]
            dlt = oe - jnp.mean(oe, axis=-1, keepdims=True)
            var = jnp.mean(dlt * dlt, axis=-1, keepdims=True)
            normed.append(dlt * lax.rsqrt(var + EPS))
        on = jnp.concatenate(normed, axis=1) * gn
        return ((g * _sigmoid(g)) * on).astype(BF16)

    cq = [qc_ref[c * c_len:(c + 1) * c_len, :] for c in range(n_ctx)]
    ck = [kc_ref[c * c_len:(c + 1) * c_len, :] for c in range(n_ctx)]
    cv = [vc_ref[c * c_len:(c + 1) * c_len, :].astype(BF16) for c in range(n_ctx)]
    kvf = [kv_state(ck[c], cv[c], kf) for c in range(n_ctx)]
    kvb = [kv_state(ck[c], cv[c], kb) for c in range(n_ctx)]
    sf_list = [zero_state]
    for c in range(n_ctx):
        sf_list.append(cd * sf_list[c] + kvf[c])
    sb_list = [zero_state] * (n_ctx + 1)
    for c in range(n_ctx - 1, -1, -1):
        sb_list[c] = cd * sb_list[c + 1] + kvb[c]
    for c in range(n_ctx):
        yc_ref[c * c_len:(c + 1) * c_len, :] = chunk_out(
            cq[c], ck[c], cv[c], gc_ref[c * c_len:(c + 1) * c_len, :],
            sf_list[c].astype(BF16), sb_list[c + 1].astype(BF16))
    s0f, s0b = sf_list[n_ctx], sb_list[0]

    def fwd(c, sf):
        r0 = pl.multiple_of(c * c_len, c_len)
        rows = pl.ds(r0, c_len)
        rc, rs1, rs2 = rc_ref[rows, :], rs1_ref[rows, :], rs2_ref[rows, :]
        q = _rope(ql_ref[rows, :], rc, rs1, rs2)
        k = _rope(kl_ref[rows, :], rc, rs1, rs2)
        qr_ref[rows, :] = q
        kr_ref[rows, :] = k
        sf_ref[c] = sf.astype(BF16)
        return cd * sf + kv_state(k, vl_ref[rows, :].astype(BF16), kf)

    lax.fori_loop(0, n_lat, fwd, s0f, unroll=RET_UNROLL)

    def bwd(i, sb):
        c = n_lat - 1 - i
        r0 = pl.multiple_of(c * c_len, c_len)
        rows = pl.ds(r0, c_len)
        q, k = qr_ref[rows, :], kr_ref[rows, :]
        v16 = vl_ref[rows, :].astype(BF16)
        yl_ref[rows, :] = chunk_out(q, k, v16, gl_ref[rows, :], sf_ref[c], sb.astype(BF16))
        return cd * sb + kv_state(k, v16, kb)

    lax.fori_loop(0, n_lat, bwd, s0b, unroll=RET_UNROLL)


def _retention(p_lat, p_ctx, tabs, gn, batch):
    t, lc = SEQ, CTX_LEN
    n_pairs = RET_HEADS // 2
    n_lat, n_ctx = t // RET_CHUNK, lc // RET_CHUNK
    qw, vw = 2 * RET_DK, 2 * RET_DV

    def pspec(rows, width, col0):
        return pl.BlockSpec((rows, width), lambda b, hp: (b, col0 // width + hp))

    tab3 = pl.BlockSpec((1, LANES, LANES), lambda b, hp: (hp, 0, 0))
    rope_spec = pl.BlockSpec((t, LANES), lambda b, hp: (0, 0))
    return pl.pallas_call(
        functools.partial(_ret_kernel, n_lat=n_lat, n_ctx=n_ctx),
        grid=(batch, n_pairs),
        in_specs=[
            pspec(t, qw, P_RQ), pspec(t, qw, P_RK), pspec(t, vw, P_RV), pspec(t, vw, P_RG),
            pspec(lc, qw, P_RQ), pspec(lc, qw, P_RK), pspec(lc, vw, P_RV), pspec(lc, vw, P_RG),
            rope_spec, rope_spec, rope_spec,
            pl.BlockSpec((1, 2, LANES, LANES), lambda b, hp: (hp, 0, 0, 0)),
            tab3, tab3, tab3, tab3,
            pl.BlockSpec((1, LANES, vw), lambda b, hp: (hp, 0, 0)),
            pl.BlockSpec((1, vw), lambda b, hp: (0, hp)),
        ],
        out_specs=[
            pl.BlockSpec((t, vw), lambda b, hp: (b, hp)),
            pl.BlockSpec((lc, vw), lambda b, hp: (b, hp)),
        ],
        out_shape=[
            jax.ShapeDtypeStruct((batch * t, RET_W), BF16),
            jax.ShapeDtypeStruct((batch * lc, RET_W), BF16),
        ],
        scratch_shapes=[
            pltpu.VMEM((t, LANES), F32),
            pltpu.VMEM((t, LANES), F32),
            pltpu.VMEM((n_lat, LANES, vw), BF16),
        ],
        compiler_params=_cparams(("parallel", "parallel"), VMEM_MIX_MIB),
        name="retention",
    )(p_lat, p_lat, p_lat, p_lat, p_ctx, p_ctx, p_ctx, p_ctx,
      tabs["ret_c"], tabs["ret_s1"], tabs["ret_s2"],
      tabs["dm"], tabs["kf"], tabs["kb"], tabs["qf"], tabs["qb"], tabs["cd"], gn)


CONV_HALF = CONV_W // 2
CONV_MARGIN = -(-CONV_HALF // SUBLANES) * SUBLANES


def _conv_kernel(glu_ref, dw_ref, misc_ref, pw_ref, o_ref, pad_ref, *, t, rt):
    half = CONV_HALF
    lo = CONV_MARGIN
    zeros = jnp.zeros((lo, CONV_CH), F32)
    pad_ref[0:lo, :] = zeros
    pad_ref[lo + t:lo + t + lo, :] = zeros

    def glu(r, carry):
        rows = pl.ds(pl.multiple_of(r * rt, rt), rt)
        a = glu_ref[rows, 0:CONV_CH]
        b = glu_ref[rows, CONV_CH:2 * CONV_CH]
        pad_ref[pl.ds(pl.multiple_of(r * rt + lo, SUBLANES), rt), :] = a * _sigmoid(b)
        return carry

    lax.fori_loop(0, t // rt, glu, 0)

    def conv(r, carry):
        base = pl.multiple_of(r * rt, rt)
        win_rows = rt + 2 * lo
        cg_w = LANES
        accs = []
        for cg in range(CONV_CH // cg_w):
            cols = slice(cg * cg_w, (cg + 1) * cg_w)
            win = pad_ref[pl.ds(base, win_rows), cols]
            acc = jnp.zeros((rt, cg_w), F32) + misc_ref[0:1, cols]
            for phase in range(SUBLANES):
                shifted = win if phase == 0 else pltpu.roll(win, win_rows - phase, 0)
                for off in range(phase, lo + half + 1, SUBLANES):
                    k = off - (lo - half)
                    if 0 <= k < CONV_W:
                        a0 = off - phase
                        acc = acc + shifted[a0:a0 + rt, :] * dw_ref[k:k + 1, cols]
            accs.append(acc)
        acc = jnp.concatenate(accs, axis=1)
        dlt = acc - jnp.mean(acc, axis=-1, keepdims=True)
        var = jnp.mean(dlt * dlt, axis=-1, keepdims=True)
        y = dlt * lax.rsqrt(var + EPS) * misc_ref[1:2, :] + misc_ref[2:3, :]
        y = y * _sigmoid(y)
        o_ref[pl.ds(base, rt), :] = _dot(y.astype(BF16), pw_ref[...]).astype(BF16)
        return carry

    lax.fori_loop(0, t // rt, conv, 0, unroll=CONV_UNROLL)


def _conv(p, dw, misc, pw, batch, t):
    rt = CONV_RT
    return pl.pallas_call(
        functools.partial(_conv_kernel, t=t, rt=rt),
        grid=(batch,),
        in_specs=[
            pl.BlockSpec((t, 2 * CONV_CH), lambda b: (b, P_GLU // (2 * CONV_CH))),
            pl.BlockSpec((CONV_W, CONV_CH), lambda b: (0, 0)),
            pl.BlockSpec((3, CONV_CH), lambda b: (0, 0)),
            pl.BlockSpec((CONV_CH, CONV_CH), lambda b: (0, 0)),
        ],
        out_specs=pl.BlockSpec((t, CONV_CH), lambda b: (b, 0)),
        out_shape=jax.ShapeDtypeStruct((batch * t, CONV_CH), BF16),
        scratch_shapes=[pltpu.VMEM((t + 2 * CONV_MARGIN, CONV_CH), F32)],
        compiler_params=_cparams(("parallel",), VMEM_SMALL_MIB),
        name="conv",
    )(p, dw, misc, pw)


def _mla_proj_kernel(cq_ref, ckv_ref, kr_ref, nq_ref, nkv_ref, wq_ref, wkv_ref, *rest, rope):
    if rope:
        rc_ref, rs1_ref, rs2_ref, q_ref, k_ref, v_ref = rest
        rot = lambda z: _rope(z, rc_ref[...], rs1_ref[...], rs2_ref[...])
    else:
        q_ref, k_ref, v_ref = rest
        rot = lambda z: z
    qq = _dot(_rms(cq_ref[...], nq_ref[...]).astype(BF16), wq_ref[...]) * (MLA_SCALE * LOG2_E)
    kv = _dot(_rms(ckv_ref[...], nkv_ref[...]).astype(BF16), wkv_ref[...])
    kr = rot(kr_ref[...]).astype(BF16)
    hw = MLA_NOPE + LANES
    for h in range(MLA_HEADS):
        q_ref[:, h * hw:h * hw + MLA_NOPE] = qq[:, h * MLA_NOPE:(h + 1) * MLA_NOPE].astype(BF16)
        qr = qq[:, MLA_HEADS * MLA_NOPE + h * LANES:MLA_HEADS * MLA_NOPE + (h + 1) * LANES]
        q_ref[:, h * hw + MLA_NOPE:(h + 1) * hw] = rot(qr).astype(BF16)
        k_ref[:, h * hw:h * hw + MLA_NOPE] = kv[:, 2 * h * LANES:(2 * h + 1) * LANES].astype(BF16)
        k_ref[:, h * hw + MLA_NOPE:(h + 1) * hw] = kr
        v_ref[:, h * MLA_DV:(h + 1) * MLA_DV] = kv[:, (2 * h + 1) * LANES:(2 * h + 2) * LANES].astype(BF16)


def _mla_proj(p, nq, nkv, wq, wkv, rope_tabs, tm):
    r = p.shape[0]
    hw = MLA_NOPE + LANES
    rope = rope_tabs is not None
    in_specs = [
        pl.BlockSpec((tm, MLA_Q_RANK), lambda i: (i, P_CQ // MLA_Q_RANK)),
        pl.BlockSpec((tm, MLA_KV_RANK), lambda i: (i, P_CKV // MLA_KV_RANK)),
        pl.BlockSpec((tm, LANES), lambda i: (i, P_KR // LANES)),
        pl.BlockSpec((1, MLA_Q_RANK), lambda i: (0, 0)),
        pl.BlockSpec((1, MLA_KV_RANK), lambda i: (0, 0)),
        pl.BlockSpec(wq.shape, lambda i: (0, 0)),
        pl.BlockSpec(wkv.shape, lambda i: (0, 0)),
    ]
    args = [p, p, p, nq, nkv, wq, wkv]
    if rope:
        n_pos = SEQ // tm
        in_specs += [pl.BlockSpec((tm, LANES), lambda i: (i % n_pos, 0))] * 3
        args += list(rope_tabs)
    return pl.pallas_call(
        functools.partial(_mla_proj_kernel, rope=rope),
        grid=(r // tm,),
        in_specs=in_specs,
        out_specs=[
            pl.BlockSpec((tm, MLA_HEADS * hw), lambda i: (i, 0)),
            pl.BlockSpec((tm, MLA_HEADS * hw), lambda i: (i, 0)),
            pl.BlockSpec((tm, MLA_W), lambda i: (i, 0)),
        ],
        out_shape=[
            jax.ShapeDtypeStruct((r, MLA_HEADS * hw), BF16),
            jax.ShapeDtypeStruct((r, MLA_HEADS * hw), BF16),
            jax.ShapeDtypeStruct((r, MLA_W), BF16),
        ],
        compiler_params=_cparams(("parallel",), VMEM_MIX_MIB),
        name="mla_proj",
    )(*args)


def _attn_kernel(q_ref, kl_ref, kc_ref, vl_ref, vc_ref, o_ref, *, n_heads, sub):
    hw = MLA_NOPE + LANES
    for e in range(n_heads):
        kcols = slice(e * hw, (e + 1) * hw)
        vcols = slice(e * MLA_DV, (e + 1) * MLA_DV)
        for i in range(q_ref.shape[0] // sub):
            rows = slice(i * sub, (i + 1) * sub)
            q = q_ref[rows, kcols]
            sl = _dot_nt(q, kl_ref[:, kcols])
            sc = _dot_nt(q, kc_ref[:, kcols])
            m = jnp.maximum(jnp.max(sl, axis=-1, keepdims=True), jnp.max(sc, axis=-1, keepdims=True))
            el = jnp.exp2(sl - m)
            ec = jnp.exp2(sc - m)
            den = jnp.sum(el, axis=-1, keepdims=True) + jnp.sum(ec, axis=-1, keepdims=True)
            o = _dot(el.astype(BF16), vl_ref[:, vcols]) + _dot(ec.astype(BF16), vc_ref[:, vcols])
            o_ref[rows, vcols] = (o / den).astype(BF16)


def _attn(q, k_lat, k_ctx, v_lat, v_ctx, batch, n_heads, sub):
    hw = n_heads * (MLA_NOPE + LANES)
    vw = n_heads * MLA_DV
    return pl.pallas_call(
        functools.partial(_attn_kernel, n_heads=n_heads, sub=sub),
        grid=(batch, MLA_HEADS // n_heads),
        in_specs=[
            pl.BlockSpec((SEQ, hw), lambda b, h: (b, h)),
            pl.BlockSpec((SEQ, hw), lambda b, h: (b, h)),
            pl.BlockSpec((CTX_LEN, hw), lambda b, h: (b, h)),
            pl.BlockSpec((SEQ, vw), lambda b, h: (b, h)),
            pl.BlockSpec((CTX_LEN, vw), lambda b, h: (b, h)),
        ],
        out_specs=pl.BlockSpec((SEQ, vw), lambda b, h: (b, h)),
        out_shape=jax.ShapeDtypeStruct((batch * SEQ, MLA_W), BF16),
        compiler_params=_cparams(("parallel", "parallel"), VMEM_MIX_MIB),
        name="mla_attn",
    )(q, k_lat, k_ctx, v_lat, v_ctx)


def _attn_ctx_kernel(q_ref, k_ref, v_ref, o_ref):
    s = _dot_nt(q_ref[...], k_ref[...])
    e = jnp.exp2(s - jnp.max(s, axis=-1, keepdims=True))
    o = _dot(e.astype(BF16), v_ref[...])
    o_ref[...] = (o / jnp.sum(e, axis=-1, keepdims=True)).astype(BF16)


def _attn_ctx(q, k, v, batch):
    hw = MLA_NOPE + LANES
    return pl.pallas_call(
        _attn_ctx_kernel,
        grid=(batch, MLA_HEADS),
        in_specs=[
            pl.BlockSpec((CTX_LEN, hw), lambda b, h: (b, h)),
            pl.BlockSpec((CTX_LEN, hw), lambda b, h: (b, h)),
            pl.BlockSpec((CTX_LEN, MLA_DV), lambda b, h: (b, h)),
        ],
        out_specs=pl.BlockSpec((CTX_LEN, MLA_DV), lambda b, h: (b, h)),
        out_shape=jax.ShapeDtypeStruct((batch * CTX_LEN, MLA_W), BF16),
        compiler_params=_cparams(("parallel", "parallel"), VMEM_SMALL_MIB),
        name="mla_attn_ctx",
    )(q, k, v)


def _tables():
    f32 = np.float32
    rows = SEQ // GRID_W
    quarter = ROPE_DIM // 4
    inv = ROPE_BASE ** (-np.arange(quarter, dtype=f32) / quarter)
    r_idx = np.repeat(np.arange(rows, dtype=f32), GRID_W)
    c_idx = np.tile(np.arange(GRID_W, dtype=f32), rows)
    ang = np.concatenate([r_idx[:, None] * inv, c_idx[:, None] * inv], axis=-1)
    cos, sin = np.cos(ang), np.sin(ang)
    z = np.zeros_like(cos)
    tabs = {
        "ret_c": np.concatenate([cos, cos, cos, cos], axis=1),
        "ret_s1": np.concatenate([-sin, z, -sin, z], axis=1),
        "ret_s2": np.concatenate([z, sin, z, sin], axis=1),
        "mla_c": np.concatenate([cos, cos, z, z], axis=1),
        "mla_s1": np.concatenate([-sin, z, z, z], axis=1),
        "mla_s2": np.concatenate([z, sin, z, z], axis=1),
    }
    c_len = RET_CHUNK
    log_g = np.log1p(-np.exp2(-5.0 - np.arange(RET_HEADS, dtype=f32)))
    i = np.arange(c_len, dtype=f32)
    k_scale = RET_DK ** -0.5
    lg = log_g[:, None, None]
    dm = np.exp(lg * np.abs(i[:, None] - i[None, :])) * k_scale
    tabs["dm"] = dm.reshape(RET_HEADS // 2, 2, c_len, c_len)

    def lanes_by_head(per_head_rows):
        t = np.repeat(per_head_rows[:, :, None], RET_DK, axis=2)
        t = t.reshape(RET_HEADS // 2, 2, c_len, RET_DK)
        return np.concatenate([t[:, 0], t[:, 1]], axis=-1)

    tabs["kf"] = lanes_by_head(np.exp(log_g[:, None] * (c_len - 1.0 - i)) * k_scale)
    tabs["kb"] = lanes_by_head(np.exp(log_g[:, None] * i) * k_scale)
    tabs["qf"] = lanes_by_head(np.exp(log_g[:, None] * (i + 1.0)))
    tabs["qb"] = lanes_by_head(np.exp(log_g[:, None] * (c_len - i)))
    cdec = np.exp(log_g * c_len).reshape(RET_HEADS // 2, 2)
    cd = np.repeat(cdec[:, :, None], RET_DK, axis=2).reshape(RET_HEADS // 2, 2 * RET_DK, 1)
    tabs["cd"] = np.broadcast_to(cd, (RET_HEADS // 2, 2 * RET_DK, 2 * RET_DV))
    assert all(t.dtype == f32 for t in tabs.values())
    return {name: jnp.asarray(t) for name, t in tabs.items()}


def _cast_kernel(x_ref, o_ref):
    o_ref[...] = x_ref[...].astype(BF16)


def _cast_bf16(w, tr):
    depth, r, c = w.shape
    return pl.pallas_call(
        _cast_kernel,
        grid=(depth, r // tr),
        in_specs=[pl.BlockSpec((1, tr, c), lambda l, i: (l, i, 0))],
        out_specs=pl.BlockSpec((1, tr, c), lambda l, i: (l, i, 0)),
        out_shape=jax.ShapeDtypeStruct(w.shape, BF16),
        compiler_params=_cparams(("parallel", "parallel"), VMEM_MIX_MIB),
        name="cast",
    )(w)


_W_IN_SRC = ((2304, 1024), (3328, 512), (768, 768), (1536, 768), (0, 384), (384, 384), (3840, 256),
             (4096, 64))


def _prep_w_in_kernel(w_ref, o_ref):
    dst = 0
    for src, width in _W_IN_SRC:
        o_ref[0, dst:dst + width, :] = w_ref[0, src:src + width, :].astype(BF16)
        dst += width
    o_ref[0, dst:, :] = jnp.zeros((P_W - dst, o_ref.shape[2]), BF16)


def _prep_w_in(w):
    depth, d, n = w.shape
    wt = jnp.swapaxes(w, 1, 2)
    tk = W_IN_PREP_COLS
    return pl.pallas_call(
        _prep_w_in_kernel,
        grid=(depth, d // tk),
        in_specs=[pl.BlockSpec((1, n, tk), lambda l, i: (l, 0, i))],
        out_specs=pl.BlockSpec((1, P_W, tk), lambda l, i: (l, 0, i)),
        out_shape=jax.ShapeDtypeStruct((depth, P_W, d), BF16),
        compiler_params=_cparams(("parallel", "parallel"), VMEM_MIX_MIB),
        name="prep_w_in",
    )(wt)


def _prep_w_uq(w):
    w = w.reshape(MLA_Q_RANK, MLA_HEADS, MLA_NOPE + MLA_ROPE)
    nope = w[:, :, :MLA_NOPE].reshape(MLA_Q_RANK, MLA_HEADS * MLA_NOPE)
    rope = jnp.pad(w[:, :, MLA_NOPE:], ((0, 0), (0, 0), (0, LANES - MLA_ROPE)))
    return jnp.concatenate([nope, rope.reshape(MLA_Q_RANK, MLA_HEADS * LANES)], axis=1).astype(BF16)


def _lat_mod_row(i, tm):
    return (i * tm) // SEQ


def _ctx_mod_row(i, tm):
    return CTX_MOD_ROW


def kernel(x, c, ctx, c_ctx, w_ada, b_ada, norm_g, w_ffa_in, w_ffa_out, w_ffb_in, w_ffb_out,
           w_in, w_out, ret_gn, cv_dw, cv_dw_b, cv_ln_g, cv_ln_b, cv_pw,
           mla_q_norm, w_uq, mla_kv_norm, w_ukv):
    batch, seq, d = x.shape
    assert (seq, d, ctx.shape[1]) == (SEQ, D_MODEL, CTX_LEN)
    xl = x.reshape(batch * seq, d)
    xc = ctx.reshape(batch * CTX_LEN, d)
    s_in = jnp.concatenate([c, c_ctx[None, :], jnp.zeros((ROWS_PAD - batch - 1, d), F32)], axis=0)
    mod_all = _ada(s_in, w_ada, b_ada).reshape(DEPTH, ROWS_PAD, N_MOD, d)
    tabs = _tables()
    mla_rope = (tabs["mla_c"], tabs["mla_s1"], tabs["mla_s2"])

    wi = _prep_w_in(w_in)
    wo = _cast_bf16(w_out, W_OUT_CAST_ROWS)

    lat = (_lat_mod_row, FFN_TM)
    cx = (_ctx_mod_row, FFN_TM)
    for l in range(DEPTH):
        last = l == DEPTH - 1
        mod = mod_all
        ng = norm_g[l]
        wq = _prep_w_uq(w_uq[l])
        wkv = w_ukv[l].astype(BF16)
        nq, nkv = mla_q_norm[l][None, :], mla_kv_norm[l][None, :]
        gn = ret_gn[l][None, :]
        conv_misc = jnp.stack([cv_dw_b[l], cv_ln_g[l], cv_ln_b[l]], axis=0)
        pw = cv_pw[l].astype(BF16)

        xl = _ffn(xl, mod, ng[0:2], w_ffa_in, w_ffa_out, l, 0, *lat, FFN_TF)
        xc = _ffn(xc, mod, ng[0:2], w_ffa_in, w_ffa_out, l, 0, *cx, FFN_TF)

        p_lat = _inproj(xl, mod, ng[2:3], wi, l, _lat_mod_row, INPROJ_TM, INPROJ_TN)
        p_ctx = _inproj(xc, mod, ng[2:3], wi, l, _ctx_mod_row, INPROJ_TM, INPROJ_TN)

        y_ret_l, y_ret_c = _retention(p_lat, p_ctx, tabs, gn, batch)
        y_conv_l = _conv(p_lat, cv_dw[l], conv_misc, pw, batch, SEQ)
        q_l, k_l, v_l = _mla_proj(p_lat, nq, nkv, wq, wkv, mla_rope, MLA_PROJ_TM)
        q_c, k_c, v_c = _mla_proj(p_ctx, nq, nkv, wq, wkv, None, MLA_PROJ_TM)
        y_mla_l = _attn(q_l, k_l, k_c, v_l, v_c, batch, ATTN_HEADS, ATTN_SUB)

        xl = _outproj(xl, y_ret_l, y_conv_l, y_mla_l, mod, ng[3:4], wo, l, _lat_mod_row, OUTPROJ_TM)
        xl = _ffn(xl, mod, ng[4:6], w_ffb_in, w_ffb_out, l, 6, *lat, FFN_TF)
        if not last:
            y_conv_c = _conv(p_ctx, cv_dw[l], conv_misc, pw, batch, CTX_LEN)
            y_mla_c = _attn_ctx(q_c, k_c, v_c, batch)
            xc = _outproj(xc, y_ret_c, y_conv_c, y_mla_c, mod, ng[3:4], wo, l, _ctx_mod_row, OUTPROJ_TM)
            xc = _ffn(xc, mod, ng[4:6], w_ffb_in, w_ffb_out, l, 6, *cx, FFN_TF)
    return xl.reshape(batch, seq, d)
```

```python
import functools

import jax
import jax.numpy as jnp
import numpy as np
from jax import lax
from jax.experimental import pallas as pl
from jax.experimental.pallas import tpu as pltpu

F32 = jnp.float32
BF16 = jnp.bfloat16

D_MODEL = 2048
SEQ = 2048
DEPTH = 2
GRID_W = 64
CTX_LEN = 256
D_FF = 5632
FFN_RES = 0.5
N_MOD = 9
ROPE_DIM = 64
ROPE_BASE = 10000.0
EPS = 1e-6
RET_HEADS = 6
RET_DK = 64
RET_DV = 128
RET_CHUNK = 128
CONV_CH = 512
CONV_W = 31
MLA_HEADS = 6
MLA_Q_RANK = 512
MLA_KV_RANK = 256
MLA_NOPE = 128
MLA_ROPE = 64
MLA_DV = 128
MLA_SCALE = (MLA_NOPE + MLA_ROPE) ** -0.5
LOG2_E = 1.4426950408889634
RET_W = RET_HEADS * RET_DV
MLA_W = MLA_HEADS * MLA_DV

LANES = 128
SUBLANES = 8
MIB = 1024 * 1024
VMEM_V7X_MIB = 64
VMEM_MATMUL_MIB = VMEM_V7X_MIB - 4
VMEM_FFN_MIB = VMEM_MATMUL_MIB
VMEM_INPROJ_MIB = 56
VMEM_MIX_MIB = 48
VMEM_SMALL_MIB = 40

FFN_TM, FFN_TF = 1024, 256
INPROJ_TM, INPROJ_TN = 1024, 1408
OUTPROJ_TM = 1024
MLA_PROJ_TM = 1024
ATTN_HEADS, ATTN_SUB = 2, 512
CONV_RT, CONV_UNROLL = 64, 4
ADA_TN = 2048
W_OUT_CAST_ROWS = 1024
W_IN_PREP_COLS = 512
ROW_CHUNK = 128
RET_UNROLL = 16

P_GLU, P_CQ, P_RV, P_RG, P_RQ, P_RK, P_CKV, P_KR = 0, 1024, 1536, 2304, 3072, 3456, 3840, 4096
P_W = 4224
CTX_MOD_ROW = 4
ROWS_PAD = 8


def _cparams(sem, vmem_mib):
    return pltpu.CompilerParams(dimension_semantics=sem, vmem_limit_bytes=int(vmem_mib * MIB))


def _sigmoid(x):
    return 1.0 / (1.0 + jnp.exp(-x))


def _rms(x, g):
    ms = jnp.mean(x * x, axis=-1, keepdims=True)
    return x * lax.rsqrt(ms + EPS) * g


def _dot(a, b):
    return jnp.dot(a, b, preferred_element_type=F32)


def _half_rows(tm):
    return (slice(0, tm // 2), slice(tm // 2, tm))


def _prenorm_modulate(x_ref, h_ref, gain_row, shift_row, rows):
    for r0 in range(rows.start, rows.stop, ROW_CHUNK):
        x = x_ref[r0:r0 + ROW_CHUNK, :]
        r = lax.rsqrt(jnp.mean(x * x, axis=-1, keepdims=True) + EPS)
        h_ref[r0:r0 + ROW_CHUNK, :] = (x * r * gain_row + shift_row).astype(BF16)


def _postnorm_residual(y_ref, x_ref, gain_row, rows, on_chunk_done=None):
    for r0 in range(rows.start, rows.stop, ROW_CHUNK):
        y = y_ref[r0:r0 + ROW_CHUNK, :]
        r = lax.rsqrt(jnp.mean(y * y, axis=-1, keepdims=True) + EPS)
        y_ref[r0:r0 + ROW_CHUNK, :] = x_ref[r0:r0 + ROW_CHUNK, :] + y * r * gain_row
        if on_chunk_done is not None:
            on_chunk_done(r0 // ROW_CHUNK)


def _dot_nt(a, b):
    return lax.dot_general(a, b, (((1,), (1,)), ((), ())), preferred_element_type=F32)


def _dot_tn(a, b):
    return lax.dot_general(a, b, (((0,), (0,)), ((), ())), preferred_element_type=F32)


def _rope(x, c, s1, s2):
    return x * c + pltpu.roll(x, 96, 1) * s1 + pltpu.roll(x, 32, 1) * s2


def _ada_kernel(s_ref, w_ref, b_ref, o_ref):
    s = s_ref[...]
    s = s * _sigmoid(s)
    o_ref[0] = _dot(s.astype(BF16), w_ref[0].astype(BF16)) + b_ref[0]


def _ada(s_in, w_ada, b_ada):
    depth, d, n = w_ada.shape
    tn = ADA_TN
    return pl.pallas_call(
        _ada_kernel,
        grid=(depth, n // tn),
        in_specs=[
            pl.BlockSpec((ROWS_PAD, d), lambda l, j: (0, 0)),
            pl.BlockSpec((1, d, tn), lambda l, j: (l, 0, j)),
            pl.BlockSpec((1, 1, tn), lambda l, j: (l, 0, j)),
        ],
        out_specs=pl.BlockSpec((1, ROWS_PAD, tn), lambda l, j: (l, 0, j)),
        out_shape=jax.ShapeDtypeStruct((depth, ROWS_PAD, n), F32),
        compiler_params=_cparams(("parallel", "parallel"), VMEM_MIX_MIB),
        name="ada",
    )(s_in, w_ada, b_ada.reshape(depth, 1, n))


def _ffn_kernel(x_ref, mod_ref, g_ref, wi_hbm, wo_hbm, o_hbm,
                wa_buf, wu_buf, wo_buf, acc_ref, h_ref, osem, wsem, *, layer, k0, nj, tf, n_tiles):
    i = pl.program_id(0)
    tm = acc_ref.shape[0]
    halves = _half_rows(tm)

    def w_copies(j, slot):
        cols = pl.ds(pl.multiple_of(j * tf, tf), tf)
        cols_u = pl.ds(pl.multiple_of(D_FF + j * tf, tf), tf)
        return (
            pltpu.make_async_copy(wi_hbm.at[layer, :, cols], wa_buf.at[slot], wsem.at[slot, 0]),
            pltpu.make_async_copy(wi_hbm.at[layer, :, cols_u], wu_buf.at[slot], wsem.at[slot, 1]),
            pltpu.make_async_copy(wo_hbm.at[layer, cols, :], wo_buf.at[slot], wsem.at[slot, 2]),
        )

    def start_w(j, slot):
        for cp in w_copies(j, slot):
            cp.start()

    def wait_w(j, slot):
        for cp in w_copies(j, slot):
            cp.wait()

    def out_copy(c):
        src = acc_ref.at[pl.ds(c * ROW_CHUNK, ROW_CHUNK), :]
        dst = o_hbm.at[pl.ds(pl.multiple_of(i * tm + c * ROW_CHUNK, ROW_CHUNK), ROW_CHUNK), :]
        return pltpu.make_async_copy(src, dst, osem.at[c])

    def weights(slot):
        return (wa_buf[slot].astype(BF16), wu_buf[slot].astype(BF16), wo_buf[slot].astype(BF16))

    def chunk_update(rows, wa, wu, wo):
        h = h_ref[rows, :]
        a = _dot(h, wa)
        u = _dot(h, wu)
        act = (a * _sigmoid(a) * u).astype(BF16)
        return _dot(act, wo)

    def slot_of(j):
        return lax.rem(i * nj + j, 2)

    @pl.when(i == 0)
    def _():
        start_w(0, 0)

    s0 = slot_of(0)
    wait_w(0, s0)
    start_w(1, 1 - s0)
    w = weights(s0)
    gain = g_ref[0:1, :] * (1.0 + mod_ref[0, k0 + 1:k0 + 2, :])
    shift = mod_ref[0, k0:k0 + 1, :]
    for rows in halves:
        _prenorm_modulate(x_ref, h_ref, gain, shift, rows)
        acc_ref[rows, :] = chunk_update(rows, *w)

    def middle(j, carry):
        s = slot_of(j)
        wait_w(j, s)
        start_w(j + 1, 1 - s)
        acc_ref[...] += chunk_update(slice(0, tm), *weights(s))
        return carry

    lax.fori_loop(1, nj - 1, middle, 0)

    s_last = slot_of(nj - 1)
    wait_w(nj - 1, s_last)

    @pl.when(i < n_tiles - 1)
    def _():
        start_w(0, 1 - s_last)

    w = weights(s_last)
    gain = g_ref[1:2, :] * (FFN_RES * mod_ref[0, k0 + 2:k0 + 3, :])
    for rows in halves:
        acc_ref[rows, :] += chunk_update(rows, *w)
        _postnorm_residual(acc_ref, x_ref, gain, rows, lambda c: out_copy(c).start())
    for c in range(tm // ROW_CHUNK):
        out_copy(c).wait()


def _ffn(x, mod, g2, w_in, w_out, layer, k0, mod_row, tm, tf):
    r, d = x.shape
    nj = D_FF // tf
    n_tiles = r // tm
    assert nj >= 3 and tm % (2 * ROW_CHUNK) == 0
    return pl.pallas_call(
        functools.partial(_ffn_kernel, layer=layer, k0=k0, nj=nj, tf=tf, n_tiles=n_tiles),
        grid=(n_tiles,),
        in_specs=[
            pl.BlockSpec((tm, d), lambda i: (i, 0)),
            pl.BlockSpec((None, 1, N_MOD, d), lambda i: (layer, mod_row(i, tm), 0, 0)),
            pl.BlockSpec((2, d), lambda i: (0, 0)),
            pl.BlockSpec(memory_space=pl.ANY),
            pl.BlockSpec(memory_space=pl.ANY),
        ],
        out_specs=pl.BlockSpec(memory_space=pl.ANY),
        out_shape=jax.ShapeDtypeStruct((r, d), F32),
        scratch_shapes=[
            pltpu.VMEM((2, d, tf), F32),
            pltpu.VMEM((2, d, tf), F32),
            pltpu.VMEM((2, tf, d), F32),
            pltpu.VMEM((tm, d), F32),
            pltpu.VMEM((tm, d), BF16),
            pltpu.SemaphoreType.DMA((tm // ROW_CHUNK,)),
            pltpu.SemaphoreType.DMA((2, 3)),
        ],
        compiler_params=_cparams(("arbitrary",), VMEM_FFN_MIB),
        name="ffn",
    )(x, mod, g2, w_in, w_out)


def _inproj_kernel(x_ref, mod_ref, g_ref, w_ref, o_ref, h_ref):
    j = pl.program_id(1)

    @pl.when(j == 0)
    def _():
        gain = g_ref[...] * (1.0 + mod_ref[0, 4:5, :])
        shift = mod_ref[0, 3:4, :]
        for rows in _half_rows(x_ref.shape[0]):
            _prenorm_modulate(x_ref, h_ref, gain, shift, rows)
            o_ref[rows, :] = _dot_nt(h_ref[rows, :], w_ref[...])

    @pl.when(j > 0)
    def _():
        o_ref[...] = _dot_nt(h_ref[...], w_ref[...])


def _inproj(x, mod, g, w, layer, mod_row, tm, tn):
    r, d = x.shape
    return pl.pallas_call(
        _inproj_kernel,
        grid=(r // tm, P_W // tn),
        in_specs=[
            pl.BlockSpec((tm, d), lambda i, j: (i, 0)),
            pl.BlockSpec((None, 1, N_MOD, d), lambda i, j: (layer, mod_row(i, tm), 0, 0)),
            pl.BlockSpec((1, d), lambda i, j: (0, 0)),
            pl.BlockSpec((None, tn, d), lambda i, j: (layer, j, 0)),
        ],
        out_specs=pl.BlockSpec((tm, tn), lambda i, j: (i, j)),
        out_shape=jax.ShapeDtypeStruct((r, P_W), F32),
        scratch_shapes=[pltpu.VMEM((tm, d), BF16)],
        compiler_params=_cparams(("parallel", "arbitrary"), VMEM_INPROJ_MIB),
        name="inproj",
    )(x, mod, g, w)


def _outproj_kernel(x_ref, yr_ref, yc_ref, ym_ref, mod_ref, g_ref, w_ref, o_ref):
    gain = g_ref[...] * mod_ref[0, 5:6, :]
    for rows in _half_rows(x_ref.shape[0]):
        y = _dot(yr_ref[rows, :], w_ref[0:RET_W, :])
        y += _dot(yc_ref[rows, :], w_ref[RET_W:RET_W + CONV_CH, :])
        y += _dot(ym_ref[rows, :], w_ref[RET_W + CONV_CH:, :])
        o_ref[rows, :] = y
        _postnorm_residual(o_ref, x_ref, gain, rows)


def _outproj(x, y_ret, y_conv, y_mla, mod, g, w, layer, mod_row, tm):
    r, d = x.shape
    return pl.pallas_call(
        _outproj_kernel,
        grid=(r // tm,),
        in_specs=[
            pl.BlockSpec((tm, d), lambda i: (i, 0)),
            pl.BlockSpec((tm, RET_W), lambda i: (i, 0)),
            pl.BlockSpec((tm, CONV_CH), lambda i: (i, 0)),
            pl.BlockSpec((tm, MLA_W), lambda i: (i, 0)),
            pl.BlockSpec((None, 1, N_MOD, d), lambda i: (layer, mod_row(i, tm), 0, 0)),
            pl.BlockSpec((1, d), lambda i: (0, 0)),
            pl.BlockSpec((None, d, d), lambda i: (layer, 0, 0), pipeline_mode=pl.Buffered(1)),
        ],
        out_specs=pl.BlockSpec((tm, d), lambda i: (i, 0)),
        out_shape=jax.ShapeDtypeStruct((r, d), F32),
        compiler_params=_cparams(("parallel",), VMEM_MATMUL_MIB),
        name="outproj",
    )(x, y_ret, y_conv, y_mla, mod, g, w)


def _ret_kernel(ql_ref, kl_ref, vl_ref, gl_ref, qc_ref, kc_ref, vc_ref, gc_ref,
                rc_ref, rs1_ref, rs2_ref, dm_ref, kf_ref, kb_ref, qf_ref, qb_ref, cd_ref, gn_ref,
                yl_ref, yc_ref, qr_ref, kr_ref, sf_ref, *, n_lat, n_ctx):
    c_len = RET_CHUNK
    lane = lax.broadcasted_iota(jnp.int32, (1, LANES), 1)
    head_masks = (lane < RET_DK, lane >= RET_DK)
    row = lax.broadcasted_iota(jnp.int32, (LANES, 2 * RET_DV), 0)
    col = lax.broadcasted_iota(jnp.int32, (LANES, 2 * RET_DV), 1)
    block_diag = (row < RET_DK) == (col < RET_DV)
    kf, kb, qf, qb, cd = kf_ref[0], kb_ref[0], qf_ref[0], qb_ref[0], cd_ref[0]
    gn = gn_ref[...]
    zero_state = jnp.zeros((LANES, 2 * RET_DV), F32)

    def kv_state(k, v16, dec):
        return jnp.where(block_diag, _dot_tn((k * dec).astype(BF16), v16), 0.0)

    def chunk_out(q, k, v16, g, sf16, sb16):
        k16 = k.astype(BF16)
        parts = []
        for e in range(2):
            q16 = jnp.where(head_masks[e], q, 0.0).astype(BF16)
            a = _dot_nt(q16, k16) * dm_ref[0, e]
            parts.append(_dot(a.astype(BF16), v16[:, e * RET_DV:(e + 1) * RET_DV]))
        o = jnp.concatenate(parts, axis=1)
        o = o + _dot((q * qf).astype(BF16), sf16) + _dot((q * qb).astype(BF16), sb16)
        normed = []
        for e in range(2):
            oe = o[:, e * RET_DV:(e + 1) * RET_DV]
            dlt = oe - jnp.mean(oe, axis=-1, keepdims=True)
            var = jnp.mean(dlt * dlt, axis=-1, keepdims=True)
            normed.append(dlt * lax.rsqrt(var + EPS))
        on = jnp.concatenate(normed, axis=1) * gn
        return ((g * _sigmoid(g)) * on).astype(BF16)

    cq = [qc_ref[c * c_len:(c + 1) * c_len, :] for c in range(n_ctx)]
    ck = [kc_ref[c * c_len:(c + 1) * c_len, :] for c in range(n_ctx)]
    cv = [vc_ref[c * c_len:(c + 1) * c_len, :].astype(BF16) for c in range(n_ctx)]
    kvf = [kv_state(ck[c], cv[c], kf) for c in range(n_ctx)]
    kvb = [kv_state(ck[c], cv[c], kb) for c in range(n_ctx)]
    sf_list = [zero_state]
    for c in range(n_ctx):
        sf_list.append(cd * sf_list[c] + kvf[c])
    sb_list = [zero_state] * (n_ctx + 1)
    for c in range(n_ctx - 1, -1, -1):
        sb_list[c] = cd * sb_list[c + 1] + kvb[c]
    for c in range(n_ctx):
        yc_ref[c * c_len:(c + 1) * c_len, :] = chunk_out(
            cq[c], ck[c], cv[c], gc_ref[c * c_len:(c + 1) * c_len, :],
            sf_list[c].astype(BF16), sb_list[c + 1].astype(BF16))
    s0f, s0b = sf_list[n_ctx], sb_list[0]

    def fwd(c, sf):
        r0 = pl.multiple_of(c * c_len, c_len)
        rows = pl.ds(r0, c_len)
        rc, rs1, rs2 = rc_ref[rows, :], rs1_ref[rows, :], rs2_ref[rows, :]
        q = _rope(ql_ref[rows, :], rc, rs1, rs2)
        k = _rope(kl_ref[rows, :], rc, rs1, rs2)
        qr_ref[rows, :] = q
        kr_ref[rows, :] = k
        sf_ref[c] = sf.astype(BF16)
        return cd * sf + kv_state(k, vl_ref[rows, :].astype(BF16), kf)

    lax.fori_loop(0, n_lat, fwd, s0f, unroll=RET_UNROLL)

    def bwd(i, sb):
        c = n_lat - 1 - i
        r0 = pl.multiple_of(c * c_len, c_len)
        rows = pl.ds(r0, c_len)
        q, k = qr_ref[rows, :], kr_ref[rows, :]
        v16 = vl_ref[rows, :].astype(BF16)
        yl_ref[rows, :] = chunk_out(q, k, v16, gl_ref[rows, :], sf_ref[c], sb.astype(BF16))
        return cd * sb + kv_state(k, v16, kb)

    lax.fori_loop(0, n_lat, bwd, s0b, unroll=RET_UNROLL)


def _retention(p_lat, p_ctx, tabs, gn, batch):
    t, lc = SEQ, CTX_LEN
    n_pairs = RET_HEADS // 2
    n_lat, n_ctx = t // RET_CHUNK, lc // RET_CHUNK
    qw, vw = 2 * RET_DK, 2 * RET_DV

    def pspec(rows, width, col0):
        return pl.BlockSpec((rows, width), lambda b, hp: (b, col0 // width + hp))

    tab3 = pl.BlockSpec((1, LANES, LANES), lambda b, hp: (hp, 0, 0))
    rope_spec = pl.BlockSpec((t, LANES), lambda b, hp: (0, 0))
    return pl.pallas_call(
        functools.partial(_ret_kernel, n_lat=n_lat, n_ctx=n_ctx),
        grid=(batch, n_pairs),
        in_specs=[
            pspec(t, qw, P_RQ), pspec(t, qw, P_RK), pspec(t, vw, P_RV), pspec(t, vw, P_RG),
            pspec(lc, qw, P_RQ), pspec(lc, qw, P_RK), pspec(lc, vw, P_RV), pspec(lc, vw, P_RG),
            rope_spec, rope_spec, rope_spec,
            pl.BlockSpec((1, 2, LANES, LANES), lambda b, hp: (hp, 0, 0, 0)),
            tab3, tab3, tab3, tab3,
            pl.BlockSpec((1, LANES, vw), lambda b, hp: (hp, 0, 0)),
            pl.BlockSpec((1, vw), lambda b, hp: (0, hp)),
        ],
        out_specs=[
            pl.BlockSpec((t, vw), lambda b, hp: (b, hp)),
            pl.BlockSpec((lc, vw), lambda b, hp: (b, hp)),
        ],
        out_shape=[
            jax.ShapeDtypeStruct((batch * t, RET_W), BF16),
            jax.ShapeDtypeStruct((batch * lc, RET_W), BF16),
        ],
        scratch_shapes=[
            pltpu.VMEM((t, LANES), F32),
            pltpu.VMEM((t, LANES), F32),
            pltpu.VMEM((n_lat, LANES, vw), BF16),
        ],
        compiler_params=_cparams(("parallel", "parallel"), VMEM_MIX_MIB),
        name="retention",
    )(p_lat, p_lat, p_lat, p_lat, p_ctx, p_ctx, p_ctx, p_ctx,
      tabs["ret_c"], tabs["ret_s1"], tabs["ret_s2"],
      tabs["dm"], tabs["kf"], tabs["kb"], tabs["qf"], tabs["qb"], tabs["cd"], gn)


CONV_HALF = CONV_W // 2
CONV_MARGIN = -(-CONV_HALF // SUBLANES) * SUBLANES


def _conv_kernel(glu_ref, dw_ref, misc_ref, pw_ref, o_ref, pad_ref, *, t, rt):
    half = CONV_HALF
    lo = CONV_MARGIN
    zeros = jnp.zeros((lo, CONV_CH), F32)
    pad_ref[0:lo, :] = zeros
    pad_ref[lo + t:lo + t + lo, :] = zeros

    def glu(r, carry):
        rows = pl.ds(pl.multiple_of(r * rt, rt), rt)
        a = glu_ref[rows, 0:CONV_CH]
        b = glu_ref[rows, CONV_CH:2 * CONV_CH]
        pad_ref[pl.ds(pl.multiple_of(r * rt + lo, SUBLANES), rt), :] = a * _sigmoid(b)
        return carry

    lax.fori_loop(0, t // rt, glu, 0)

    def conv(r, carry):
        base = pl.multiple_of(r * rt, rt)
        win_rows = rt + 2 * lo
        cg_w = LANES
        accs = []
        for cg in range(CONV_CH // cg_w):
            cols = slice(cg * cg_w, (cg + 1) * cg_w)
            win = pad_ref[pl.ds(base, win_rows), cols]
            acc = jnp.zeros((rt, cg_w), F32) + misc_ref[0:1, cols]
            for phase in range(SUBLANES):
                shifted = win if phase == 0 else pltpu.roll(win, win_rows - phase, 0)
                for off in range(phase, lo + half + 1, SUBLANES):
                    k = off - (lo - half)
                    if 0 <= k < CONV_W:
                        a0 = off - phase
                        acc = acc + shifted[a0:a0 + rt, :] * dw_ref[k:k + 1, cols]
            accs.append(acc)
        acc = jnp.concatenate(accs, axis=1)
        dlt = acc - jnp.mean(acc, axis=-1, keepdims=True)
        var = jnp.mean(dlt * dlt, axis=-1, keepdims=True)
        y = dlt * lax.rsqrt(var + EPS) * misc_ref[1:2, :] + misc_ref[2:3, :]
        y = y * _sigmoid(y)
        o_ref[pl.ds(base, rt), :] = _dot(y.astype(BF16), pw_ref[...]).astype(BF16)
        return carry

    lax.fori_loop(0, t // rt, conv, 0, unroll=CONV_UNROLL)


def _conv(p, dw, misc, pw, batch, t):
    rt = CONV_RT
    return pl.pallas_call(
        functools.partial(_conv_kernel, t=t, rt=rt),
        grid=(batch,),
        in_specs=[
            pl.BlockSpec((t, 2 * CONV_CH), lambda b: (b, P_GLU // (2 * CONV_CH))),
            pl.BlockSpec((CONV_W, CONV_CH), lambda b: (0, 0)),
            pl.BlockSpec((3, CONV_CH), lambda b: (0, 0)),
            pl.BlockSpec((CONV_CH, CONV_CH), lambda b: (0, 0)),
        ],
        out_specs=pl.BlockSpec((t, CONV_CH), lambda b: (b, 0)),
        out_shape=jax.ShapeDtypeStruct((batch * t, CONV_CH), BF16),
        scratch_shapes=[pltpu.VMEM((t + 2 * CONV_MARGIN, CONV_CH), F32)],
        compiler_params=_cparams(("parallel",), VMEM_SMALL_MIB),
        name="conv",
    )(p, dw, misc, pw)


def _mla_proj_kernel(cq_ref, ckv_ref, kr_ref, nq_ref, nkv_ref, wq_ref, wkv_ref, *rest, rope):
    if rope:
        rc_ref, rs1_ref, rs2_ref, q_ref, k_ref, v_ref = rest
        rot = lambda z: _rope(z, rc_ref[...], rs1_ref[...], rs2_ref[...])
    else:
        q_ref, k_ref, v_ref = rest
        rot = lambda z: z
    qq = _dot(_rms(cq_ref[...], nq_ref[...]).astype(BF16), wq_ref[...]) * (MLA_SCALE * LOG2_E)
    kv = _dot(_rms(ckv_ref[...], nkv_ref[...]).astype(BF16), wkv_ref[...])
    kr = rot(kr_ref[...]).astype(BF16)
    hw = MLA_NOPE + LANES
    for h in range(MLA_HEADS):
        q_ref[:, h * hw:h * hw + MLA_NOPE] = qq[:, h * MLA_NOPE:(h + 1) * MLA_NOPE].astype(BF16)
        qr = qq[:, MLA_HEADS * MLA_NOPE + h * LANES:MLA_HEADS * MLA_NOPE + (h + 1) * LANES]
        q_ref[:, h * hw + MLA_NOPE:(h + 1) * hw] = rot(qr).astype(BF16)
        k_ref[:, h * hw:h * hw + MLA_NOPE] = kv[:, 2 * h * LANES:(2 * h + 1) * LANES].astype(BF16)
        k_ref[:, h * hw + MLA_NOPE:(h + 1) * hw] = kr
        v_ref[:, h * MLA_DV:(h + 1) * MLA_DV] = kv[:, (2 * h + 1) * LANES:(2 * h + 2) * LANES].astype(BF16)


def _mla_proj(p, nq, nkv, wq, wkv, rope_tabs, tm):
    r = p.shape[0]
    hw = MLA_NOPE + LANES
    rope = rope_tabs is not None
    in_specs = [
        pl.BlockSpec((tm, MLA_Q_RANK), lambda i: (i, P_CQ // MLA_Q_RANK)),
        pl.BlockSpec((tm, MLA_KV_RANK), lambda i: (i, P_CKV // MLA_KV_RANK)),
        pl.BlockSpec((tm, LANES), lambda i: (i, P_KR // LANES)),
        pl.BlockSpec((1, MLA_Q_RANK), lambda i: (0, 0)),
        pl.BlockSpec((1, MLA_KV_RANK), lambda i: (0, 0)),
        pl.BlockSpec(wq.shape, lambda i: (0, 0)),
        pl.BlockSpec(wkv.shape, lambda i: (0, 0)),
    ]
    args = [p, p, p, nq, nkv, wq, wkv]
    if rope:
        n_pos = SEQ // tm
        in_specs += [pl.BlockSpec((tm, LANES), lambda i: (i % n_pos, 0))] * 3
        args += list(rope_tabs)
    return pl.pallas_call(
        functools.partial(_mla_proj_kernel, rope=rope),
        grid=(r // tm,),
        in_specs=in_specs,
        out_specs=[
            pl.BlockSpec((tm, MLA_HEADS * hw), lambda i: (i, 0)),
            pl.BlockSpec((tm, MLA_HEADS * hw), lambda i: (i, 0)),
            pl.BlockSpec((tm, MLA_W), lambda i: (i, 0)),
        ],
        out_shape=[
            jax.ShapeDtypeStruct((r, MLA_HEADS * hw), BF16),
            jax.ShapeDtypeStruct((r, MLA_HEADS * hw), BF16),
            jax.ShapeDtypeStruct((r, MLA_W), BF16),
        ],
        compiler_params=_cparams(("parallel",), VMEM_MIX_MIB),
        name="mla_proj",
    )(*args)


def _attn_kernel(q_ref, kl_ref, kc_ref, vl_ref, vc_ref, o_ref, *, n_heads, sub):
    hw = MLA_NOPE + LANES
    for e in range(n_heads):
        kcols = slice(e * hw, (e + 1) * hw)
        vcols = slice(e * MLA_DV, (e + 1) * MLA_DV)
        for i in range(q_ref.shape[0] // sub):
            rows = slice(i * sub, (i + 1) * sub)
            q = q_ref[rows, kcols]
            sl = _dot_nt(q, kl_ref[:, kcols])
            sc = _dot_nt(q, kc_ref[:, kcols])
            m = jnp.maximum(jnp.max(sl, axis=-1, keepdims=True), jnp.max(sc, axis=-1, keepdims=True))
            el = jnp.exp2(sl - m)
            ec = jnp.exp2(sc - m)
            den = jnp.sum(el, axis=-1, keepdims=True) + jnp.sum(ec, axis=-1, keepdims=True)
            o = _dot(el.astype(BF16), vl_ref[:, vcols]) + _dot(ec.astype(BF16), vc_ref[:, vcols])
            o_ref[rows, vcols] = (o / den).astype(BF16)


def _attn(q, k_lat, k_ctx, v_lat, v_ctx, batch, n_heads, sub):
    hw = n_heads * (MLA_NOPE + LANES)
    vw = n_heads * MLA_DV
    return pl.pallas_call(
        functools.partial(_attn_kernel, n_heads=n_heads, sub=sub),
        grid=(batch, MLA_HEADS // n_heads),
        in_specs=[
            pl.BlockSpec((SEQ, hw), lambda b, h: (b, h)),
            pl.BlockSpec((SEQ, hw), lambda b, h: (b, h)),
            pl.BlockSpec((CTX_LEN, hw), lambda b, h: (b, h)),
            pl.BlockSpec((SEQ, vw), lambda b, h: (b, h)),
            pl.BlockSpec((CTX_LEN, vw), lambda b, h: (b, h)),
        ],
        out_specs=pl.BlockSpec((SEQ, vw), lambda b, h: (b, h)),
        out_shape=jax.ShapeDtypeStruct((batch * SEQ, MLA_W), BF16),
        compiler_params=_cparams(("parallel", "parallel"), VMEM_MIX_MIB),
        name="mla_attn",
    )(q, k_lat, k_ctx, v_lat, v_ctx)


def _attn_ctx_kernel(q_ref, k_ref, v_ref, o_ref):
    s = _dot_nt(q_ref[...], k_ref[...])
    e = jnp.exp2(s - jnp.max(s, axis=-1, keepdims=True))
    o = _dot(e.astype(BF16), v_ref[...])
    o_ref[...] = (o / jnp.sum(e, axis=-1, keepdims=True)).astype(BF16)


def _attn_ctx(q, k, v, batch):
    hw = MLA_NOPE + LANES
    return pl.pallas_call(
        _attn_ctx_kernel,
        grid=(batch, MLA_HEADS),
        in_specs=[
            pl.BlockSpec((CTX_LEN, hw), lambda b, h: (b, h)),
            pl.BlockSpec((CTX_LEN, hw), lambda b, h: (b, h)),
            pl.BlockSpec((CTX_LEN, MLA_DV), lambda b, h: (b, h)),
        ],
        out_specs=pl.BlockSpec((CTX_LEN, MLA_DV), lambda b, h: (b, h)),
        out_shape=jax.ShapeDtypeStruct((batch * CTX_LEN, MLA_W), BF16),
        compiler_params=_cparams(("parallel", "parallel"), VMEM_SMALL_MIB),
        name="mla_attn_ctx",
    )(q, k, v)


def _tables():
    f32 = np.float32
    rows = SEQ // GRID_W
    quarter = ROPE_DIM // 4
    inv = ROPE_BASE ** (-np.arange(quarter, dtype=f32) / quarter)
    r_idx = np.repeat(np.arange(rows, dtype=f32), GRID_W)
    c_idx = np.tile(np.arange(GRID_W, dtype=f32), rows)
    ang = np.concatenate([r_idx[:, None] * inv, c_idx[:, None] * inv], axis=-1)
    cos, sin = np.cos(ang), np.sin(ang)
    z = np.zeros_like(cos)
    tabs = {
        "ret_c": np.concatenate([cos, cos, cos, cos], axis=1),
        "ret_s1": np.concatenate([-sin, z, -sin, z], axis=1),
        "ret_s2": np.concatenate([z, sin, z, sin], axis=1),
        "mla_c": np.concatenate([cos, cos, z, z], axis=1),
        "mla_s1": np.concatenate([-sin, z, z, z], axis=1),
        "mla_s2": np.concatenate([z, sin, z, z], axis=1),
    }
    c_len = RET_CHUNK
    log_g = np.log1p(-np.exp2(-5.0 - np.arange(RET_HEADS, dtype=f32)))
    i = np.arange(c_len, dtype=f32)
    k_scale = RET_DK ** -0.5
    lg = log_g[:, None, None]
    dm = np.exp(lg * np.abs(i[:, None] - i[None, :])) * k_scale
    tabs["dm"] = dm.reshape(RET_HEADS // 2, 2, c_len, c_len)

    def lanes_by_head(per_head_rows):
        t = np.repeat(per_head_rows[:, :, None], RET_DK, axis=2)
        t = t.reshape(RET_HEADS // 2, 2, c_len, RET_DK)
        return np.concatenate([t[:, 0], t[:, 1]], axis=-1)

    tabs["kf"] = lanes_by_head(np.exp(log_g[:, None] * (c_len - 1.0 - i)) * k_scale)
    tabs["kb"] = lanes_by_head(np.exp(log_g[:, None] * i) * k_scale)
    tabs["qf"] = lanes_by_head(np.exp(log_g[:, None] * (i + 1.0)))
    tabs["qb"] = lanes_by_head(np.exp(log_g[:, None] * (c_len - i)))
    cdec = np.exp(log_g * c_len).reshape(RET_HEADS // 2, 2)
    cd = np.repeat(cdec[:, :, None], RET_DK, axis=2).reshape(RET_HEADS // 2, 2 * RET_DK, 1)
    tabs["cd"] = np.broadcast_to(cd, (RET_HEADS // 2, 2 * RET_DK, 2 * RET_DV))
    assert all(t.dtype == f32 for t in tabs.values())
    return {name: jnp.asarray(t) for name, t in tabs.items()}


def _cast_kernel(x_ref, o_ref):
    o_ref[...] = x_ref[...].astype(BF16)


def _cast_bf16(w, tr):
    depth, r, c = w.shape
    return pl.pallas_call(
        _cast_kernel,
        grid=(depth, r // tr),
        in_specs=[pl.BlockSpec((1, tr, c), lambda l, i: (l, i, 0))],
        out_specs=pl.BlockSpec((1, tr, c), lambda l, i: (l, i, 0)),
        out_shape=jax.ShapeDtypeStruct(w.shape, BF16),
        compiler_params=_cparams(("parallel", "parallel"), VMEM_MIX_MIB),
        name="cast",
    )(w)


_W_IN_SRC = ((2304, 1024), (3328, 512), (768, 768), (1536, 768), (0, 384), (384, 384), (3840, 256),
             (4096, 64))


def _prep_w_in_kernel(w_ref, o_ref):
    dst = 0
    for src, width in _W_IN_SRC:
        o_ref[0, dst:dst + width, :] = w_ref[0, src:src + width, :].astype(BF16)
        dst += width
    o_ref[0, dst:, :] = jnp.zeros((P_W - dst, o_ref.shape[2]), BF16)


def _prep_w_in(w):
    depth, d, n = w.shape
    wt = jnp.swapaxes(w, 1, 2)
    tk = W_IN_PREP_COLS
    return pl.pallas_call(
        _prep_w_in_kernel,
        grid=(depth, d // tk),
        in_specs=[pl.BlockSpec((1, n, tk), lambda l, i: (l, 0, i))],
        out_specs=pl.BlockSpec((1, P_W, tk), lambda l, i: (l, 0, i)),
        out_shape=jax.ShapeDtypeStruct((depth, P_W, d), BF16),
        compiler_params=_cparams(("parallel", "parallel"), VMEM_MIX_MIB),
        name="prep_w_in",
    )(wt)


def _prep_w_uq(w):
    w = w.reshape(MLA_Q_RANK, MLA_HEADS, MLA_NOPE + MLA_ROPE)
    nope = w[:, :, :MLA_NOPE].reshape(MLA_Q_RANK, MLA_HEADS * MLA_NOPE)
    rope = jnp.pad(w[:, :, MLA_NOPE:], ((0, 0), (0, 0), (0, LANES - MLA_ROPE)))
    return jnp.concatenate([nope, rope.reshape(MLA_Q_RANK, MLA_HEADS * LANES)], axis=1).astype(BF16)


def _lat_mod_row(i, tm):
    return (i * tm) // SEQ


def _ctx_mod_row(i, tm):
    return CTX_MOD_ROW


def kernel(x, c, ctx, c_ctx, w_ada, b_ada, norm_g, w_ffa_in, w_ffa_out, w_ffb_in, w_ffb_out,
           w_in, w_out, ret_gn, cv_dw, cv_dw_b, cv_ln_g, cv_ln_b, cv_pw,
           mla_q_norm, w_uq, mla_kv_norm, w_ukv):
    batch, seq, d = x.shape
    assert (seq, d, ctx.shape[1]) == (SEQ, D_MODEL, CTX_LEN)
    xl = x.reshape(batch * seq, d)
    xc = ctx.reshape(batch * CTX_LEN, d)
    s_in = jnp.concatenate([c, c_ctx[None, :], jnp.zeros((ROWS_PAD - batch - 1, d), F32)], axis=0)
    mod_all = _ada(s_in, w_ada, b_ada).reshape(DEPTH, ROWS_PAD, N_MOD, d)
    tabs = _tables()
    mla_rope = (tabs["mla_c"], tabs["mla_s1"], tabs["mla_s2"])

    wi = _prep_w_in(w_in)
    wo = _cast_bf16(w_out, W_OUT_CAST_ROWS)

    lat = (_lat_mod_row, FFN_TM)
    cx = (_ctx_mod_row, FFN_TM)
    for l in range(DEPTH):
        last = l == DEPTH - 1
        mod = mod_all
        ng = norm_g[l]
        wq = _prep_w_uq(w_uq[l])
        wkv = w_ukv[l].astype(BF16)
        nq, nkv = mla_q_norm[l][None, :], mla_kv_norm[l][None, :]
        gn = ret_gn[l][None, :]
        conv_misc = jnp.stack([cv_dw_b[l], cv_ln_g[l], cv_ln_b[l]], axis=0)
        pw = cv_pw[l].astype(BF16)

        xl = _ffn(xl, mod, ng[0:2], w_ffa_in, w_ffa_out, l, 0, *lat, FFN_TF)
        xc = _ffn(xc, mod, ng[0:2], w_ffa_in, w_ffa_out, l, 0, *cx, FFN_TF)

        p_lat = _inproj(xl, mod, ng[2:3], wi, l, _lat_mod_row, INPROJ_TM, INPROJ_TN)
        p_ctx = _inproj(xc, mod, ng[2:3], wi, l, _ctx_mod_row, INPROJ_TM, INPROJ_TN)

        y_ret_l, y_ret_c = _retention(p_lat, p_ctx, tabs, gn, batch)
        y_conv_l = _conv(p_lat, cv_dw[l], conv_misc, pw, batch, SEQ)
        q_l, k_l, v_l = _mla_proj(p_lat, nq, nkv, wq, wkv, mla_rope, MLA_PROJ_TM)
        q_c, k_c, v_c = _mla_proj(p_ctx, nq, nkv, wq, wkv, None, MLA_PROJ_TM)
        y_mla_l = _attn(q_l, k_l, k_c, v_l, v_c, batch, ATTN_HEADS, ATTN_SUB)

        xl = _outproj(xl, y_ret_l, y_conv_l, y_mla_l, mod, ng[3:4], wo, l, _lat_mod_row, OUTPROJ_TM)
        xl = _ffn(xl, mod, ng[4:6], w_ffb_in, w_ffb_out, l, 6, *lat, FFN_TF)
        if not last:
            y_conv_c = _conv(p_ctx, cv_dw[l], conv_misc, pw, batch, CTX_LEN)
            y_mla_c = _attn_ctx(q_c, k_c, v_c, batch)
            xc = _outproj(xc, y_ret_c, y_conv_c, y_mla_c, mod, ng[3:4], wo, l, _ctx_mod_row, OUTPROJ_TM)
            xc = _ffn(xc, mod, ng[4:6], w_ffb_in, w_ffb_out, l, 6, *cx, FFN_TF)
    return xl.reshape(batch, seq, d)
```
